```python
import jax, jax.numpy as jnp
from jax import lax
import numpy as np

D_MODEL = 1024
BATCH = 8
SEQ = 2048
DEPTH = 1
DEC_BATCH = 128
DEC_SEQ = 8
PAST_LEN = 16384
PAGE_SIZE = 128

MIX_WIDTH = D_MODEL
R_WIDTH = MIX_WIDTH // 2
G_WIDTH = MIX_WIDTH - R_WIDTH
N_R = 64
H_R = R_WIDTH // N_R
H_C = 8
N_C = G_WIDTH // H_C
CHUNK = 128
W_RANK = 64
A_RANK = 64
G_RANK = 128
R_COLS = 3 * R_WIDTH + W_RANK + A_RANK + G_RANK
P_COLS = R_COLS + 2 * G_WIDTH
R_SPLITS = (R_WIDTH, 2 * R_WIDTH, 3 * R_WIDTH, 3 * R_WIDTH + W_RANK, 3 * R_WIDTH + W_RANK + A_RANK)
D_FF = ((8 * D_MODEL // 3 + 127) // 128) * 128
N_MOD = 9
RMS_EPS = 1e-6
LN_EPS = 1e-5
GN_EPS = 64e-5

kernel_name = 'rwkv7_gmlp_hymba_macaron_adaln_step'


def _rms(x):
    xf = x.astype(jnp.float32)
    return (xf * lax.rsqrt(jnp.mean(xf * xf, -1, keepdims=True) + RMS_EPS)).astype(x.dtype)


def _swiglu(h, w_gu, w_dn):
    g, u = jnp.split(h @ w_gu, 2, axis=-1)
    return (jax.nn.silu(g) * u) @ w_dn


def _wkv_scan(r, w, k, v, aa, bb, s0):
    def step(s, inp):
        r_t, w_t, k_t, v_t, a_t, b_t = inp
        sa = jnp.einsum('bhvk,bhk->bhv', s, a_t)
        s = s * w_t[:, :, None, :] + sa[..., None] * b_t[:, :, None, :] + v_t[..., None] * k_t[:, :, None, :]
        return s, jnp.einsum('bhvk,bhk->bhv', s, r_t)
    xs = tuple(jnp.moveaxis(t, 1, 0) for t in (r, w, k, v, aa, bb))
    s, ys = lax.scan(step, s0, xs)
    return jnp.moveaxis(ys, 0, 1), s


def _chunk_mix(vn, w_s, b_s):
    B, T = vn.shape[0], vn.shape[1]
    n_chunks = -(-T // CHUNK)
    pad = n_chunks * CHUNK - T
    vp = jnp.pad(vn, ((0, 0), (0, pad), (0, 0), (0, 0))).reshape(B, n_chunks, CHUNK, H_C, N_C)
    ws = jnp.where(jnp.tril(jnp.ones((CHUNK, CHUNK), bool)), w_s, jnp.zeros_like(w_s))
    mixed = jnp.einsum('hij,bcjhd->bcihd', ws, vp) + b_s.T[None, None, :, :, None]
    return mixed.reshape(B, n_chunks * CHUNK, H_C, N_C)[:, :T]


def _layer(x, c, shift0, wkv0, w_ada, b_ada, ffn1_gu, ffn1_dn, w_in, mu_shift, w0, w_lora_up, a0,
           a_lora_up, g_lora_up, k_k, k_a, r_k, gn_w, gn_b, ln_v_g, ln_v_b, w_s, b_s, w_out,
           ffn2_gu, ffn2_dn):
    B, T, _ = x.shape
    dt = x.dtype
    mod = jax.nn.silu(c) @ w_ada + b_ada
    sh1, sc1, g1, sh2, sc2, g2, sh3, sc3, g3 = [m[:, None, :] for m in jnp.split(mod, N_MOD, axis=-1)]

    h = x + 0.5 * g1 * _swiglu(_rms(x) * (1 + sc1) + sh1, ffn1_gu, ffn1_dn)

    n = _rms(h) * (1 + sc2) + sh2
    p = n @ w_in
    p_r = p[..., :R_COLS]
    p_u = p[..., R_COLS:R_COLS + G_WIDTH]
    p_v = p[..., R_COLS + G_WIDTH:]

    prev = jnp.concatenate([shift0[:, None, :].astype(dt), p_r[:, :-1]], axis=1)
    xm = p_r + (prev - p_r) * mu_shift
    r, k, v, wd, ad, gd = jnp.split(xm, R_SPLITS, axis=-1)
    w_log = -jax.nn.softplus(-(w0 + jnp.tanh(wd) @ w_lora_up)) - 0.5
    decay = jnp.exp(-jnp.exp(w_log.astype(jnp.float32)))
    a = jax.nn.sigmoid(a0 + ad @ a_lora_up)
    g = jax.nn.sigmoid(gd) @ g_lora_up
    heads = lambda t: t.reshape(B, T, H_R, N_R)
    kk = heads(k * k_k).astype(jnp.float32)
    kk = kk / jnp.maximum(jnp.sqrt(jnp.sum(kk * kk, -1, keepdims=True)), 1e-12)
    k = k * (1 + (a - 1) * k_a)
    r_h, k_h, v_h, a_h = heads(r), heads(k), heads(v), heads(a)
    f32 = lambda t: t.astype(jnp.float32)
    y, s_new = _wkv_scan(f32(r_h), heads(decay), f32(k_h), f32(v_h), -kk, kk * f32(a_h),
                         wkv0.astype(jnp.float32))
    mu = jnp.mean(y, -1, keepdims=True)
    var = jnp.mean(jnp.square(y - mu), -1, keepdims=True)
    y = ((y - mu) * lax.rsqrt(var + GN_EPS)).reshape(B, T, R_WIDTH).astype(dt) * gn_w + gn_b
    y = y + (jnp.sum(r_h * k_h * r_k, -1, keepdims=True) * v_h).reshape(B, T, R_WIDTH)
    y_r = y * g

    pv = p_v.astype(jnp.float32)
    pm = jnp.mean(pv, -1, keepdims=True)
    pvar = jnp.mean(jnp.square(pv - pm), -1, keepdims=True)
    vn = ((pv - pm) * lax.rsqrt(pvar + LN_EPS)).astype(dt) * ln_v_g + ln_v_b
    mixed = _chunk_mix(vn.reshape(B, T, H_C, N_C), w_s, b_s).reshape(B, T, G_WIDTH)
    y_c = p_u * mixed

    h = h + g2 * (jnp.concatenate([y_r, y_c], axis=-1) @ w_out)

    h = h + 0.5 * g3 * _swiglu(_rms(h) * (1 + sc3) + sh3, ffn2_gu, ffn2_dn)
    return h, p_r[:, -1], s_new.astype(wkv0.dtype), vn


def setup_inputs(seed: int = 0) -> dict:
    key = jax.random.key(seed)
    ks = iter(jax.random.split(key, 40))
    nrm = lambda shape, s: jax.random.normal(next(ks), shape, jnp.float32) * s
    L, D = DEPTH, D_MODEL
    return {
        'x_prompt': nrm((BATCH, SEQ, D), 1.0),
        'x_sample': nrm((DEC_BATCH, DEC_SEQ, D), 1.0),
        'state_shift': nrm((L, DEC_BATCH, R_COLS), 1.0),
        'state_wkv': nrm((L, DEC_BATCH, H_R, N_R, N_R), 0.5),
        'c_prompt': nrm((BATCH, D), 1.0),
        'c_sample': nrm((DEC_BATCH, D), 1.0),
        'w_ada': nrm((L, D, N_MOD * D), D ** -0.5),
        'b_ada': nrm((L, N_MOD * D), 0.01),
        'ffn1_gu': nrm((L, D, 2 * D_FF), D ** -0.5),
        'ffn1_dn': nrm((L, D_FF, D), D_FF ** -0.5),
        'w_in': nrm((L, D, P_COLS), D ** -0.5),
        'mu_shift': jax.random.uniform(next(ks), (L, R_COLS), jnp.float32),
        'w0': jax.random.uniform(next(ks), (L, R_WIDTH), jnp.float32, minval=-6.0, maxval=1.0),
        'w_lora_up': nrm((L, W_RANK, R_WIDTH), 0.1),
        'a0': nrm((L, R_WIDTH), 0.1),
        'a_lora_up': nrm((L, A_RANK, R_WIDTH), 0.1),
        'g_lora_up': nrm((L, G_RANK, R_WIDTH), G_RANK ** -0.5),
        'k_k': 0.85 + nrm((L, R_WIDTH), 0.02),
        'k_a': 1.0 + nrm((L, R_WIDTH), 0.02),
        'r_k': nrm((L, H_R, N_R), 0.1),
        'gn_w': 1.0 + nrm((L, R_WIDTH), 0.02),
        'gn_b': nrm((L, R_WIDTH), 0.01),
        'ln_v_g': 1.0 + nrm((L, G_WIDTH), 0.02),
        'ln_v_b': nrm((L, G_WIDTH), 0.01),
        'w_s': nrm((L, H_C, CHUNK, CHUNK), CHUNK ** -0.5),
        'b_s': 1.0 + nrm((L, H_C, CHUNK), 0.02),
        'w_out': nrm((L, MIX_WIDTH, D), MIX_WIDTH ** -0.5),
        'ffn2_gu': nrm((L, D, 2 * D_FF), D ** -0.5),
        'ffn2_dn': nrm((L, D_FF, D), D_FF ** -0.5),
        'final_g': 1.0 + nrm((D,), 0.02),
    }


def reference(x_prompt, x_sample, state_shift, state_wkv, c_prompt, c_sample, w_ada, b_ada, ffn1_gu,
              ffn1_dn, w_in, mu_shift, w0, w_lora_up, a0, a_lora_up, g_lora_up, k_k, k_a, r_k, gn_w,
              gn_b, ln_v_g, ln_v_b, w_s, b_s, w_out, ffn2_gu, ffn2_dn, final_g):
    xp, xs = x_prompt, x_sample
    shp, wkp, shs, wks, cvs = [], [], [], [], []
    for l in range(DEPTH):
        lw = (w_ada[l], b_ada[l], ffn1_gu[l], ffn1_dn[l], w_in[l], mu_shift[l], w0[l], w_lora_up[l],
              a0[l], a_lora_up[l], g_lora_up[l], k_k[l], k_a[l], r_k[l], gn_w[l], gn_b[l], ln_v_g[l],
              ln_v_b[l], w_s[l], b_s[l], w_out[l], ffn2_gu[l], ffn2_dn[l])
        zero_shift = jnp.zeros((xp.shape[0], R_COLS), xp.dtype)
        zero_wkv = jnp.zeros((xp.shape[0], H_R, N_R, N_R), state_wkv.dtype)
        xp, sh_p, wk_p, _ = _layer(xp, c_prompt, zero_shift, zero_wkv, *lw)
        xs, sh_s, wk_s, v_s = _layer(xs, c_sample, state_shift[l], state_wkv[l], *lw)
        shp.append(sh_p); wkp.append(wk_p); shs.append(sh_s); wks.append(wk_s); cvs.append(v_s)
    y_prompt = _rms(xp) * final_g
    y_sample = _rms(xs) * final_g
    return (y_prompt, y_sample, jnp.stack(shp), jnp.stack(wkp), jnp.stack(shs), jnp.stack(wks), jnp.stack(cvs))
```

```python
import functools

import jax
import jax.numpy as jnp
from jax import lax
from jax.experimental import pallas as pl
from jax.experimental.pallas import tpu as pltpu

_F32 = jnp.float32
_BF16 = jnp.bfloat16

RMS_EPS = 1e-6
LN_EPS = 1e-5
GN_EPS = 64e-5
N_MOD = 9
WKV_CHUNK = 64
HEAD_GROUP = 4
INV_BLOCK = 16
GMLP_CHUNK = 128

_V7X_VMEM_BYTES = 64 * 1024 * 1024
_VMEM_LIMIT = _V7X_VMEM_BYTES - 8 * 1024 * 1024


def _dot(a, b):
    return jnp.dot(a.astype(_BF16), b.astype(_BF16), preferred_element_type=_F32)


def _dot_nt(a, b):
    return lax.dot_general(a.astype(_BF16), b.astype(_BF16), (((1,), (1,)), ((), ())),
                           preferred_element_type=_F32)


def _dot_tn(a, b):
    return lax.dot_general(a.astype(_BF16), b.astype(_BF16), (((0,), (0,)), ((), ())),
                           preferred_element_type=_F32)


def _split_hi_lo(x):
    hi = x.astype(_BF16)
    lo = (x - hi.astype(_F32)).astype(_BF16)
    return hi, lo


def _dot_sel_rhs(x, sel):
    hi, lo = _split_hi_lo(x)
    return (jnp.dot(hi, sel, preferred_element_type=_F32)
            + jnp.dot(lo, sel, preferred_element_type=_F32))


def _dot_sel_lhs(sel, x):
    hi, lo = _split_hi_lo(x)
    return (jnp.dot(sel, hi, preferred_element_type=_F32)
            + jnp.dot(sel, lo, preferred_element_type=_F32))


def _div_pow2(x, n):
    assert n & (n - 1) == 0
    return lax.shift_right_logical(x, jnp.int32(n.bit_length() - 1))


def _mod_pow2(x, n):
    assert n & (n - 1) == 0
    return lax.bitwise_and(x, jnp.int32(n - 1))


def _rms(x):
    return x * lax.rsqrt(jnp.mean(x * x, -1, keepdims=True) + RMS_EPS)


def _sigmoid(x):
    return 1.0 / (1.0 + jnp.exp(-x))


def _affine(x, scale, shift=None):
    mb = scale.shape[0]
    if mb == 1:
        y = x * scale
        return y if shift is None else y + shift
    tm, d = x.shape
    y = x.reshape(tm // mb, mb, d) * scale[None]
    if shift is not None:
        y = y + shift[None]
    return y.reshape(tm, d)


def _swiglu(n_bf, wgu_ref, wdn_ref, tf):
    d_ff = wdn_ref.shape[0]
    acc = None
    for j in range(d_ff // tf):
        g = jnp.dot(n_bf, wgu_ref[:, j * tf:(j + 1) * tf], preferred_element_type=_F32)
        u = jnp.dot(n_bf, wgu_ref[:, d_ff + j * tf:d_ff + (j + 1) * tf], preferred_element_type=_F32)
        hm = (g * _sigmoid(g) * u).astype(_BF16)
        part = jnp.dot(hm, wdn_ref[j * tf:(j + 1) * tf, :], preferred_element_type=_F32)
        acc = part if acc is None else acc + part
    return acc


def _mod_kernel(c_ref, w_ref, b_ref, o_ref):
    c = c_ref[...]
    s = c * _sigmoid(c)
    o_ref[...] = _dot(s, w_ref[...]) + b_ref[...]


def _mod_call(c_all, w_ada, b_ada):
    n, d = c_all.shape
    return pl.pallas_call(
        _mod_kernel,
        out_shape=jax.ShapeDtypeStruct((n, N_MOD * d), _F32),
        grid=(N_MOD,),
        in_specs=[pl.BlockSpec((n, d), lambda j: (0, 0)),
                  pl.BlockSpec((d, d), lambda j: (0, j)),
                  pl.BlockSpec((1, d), lambda j: (0, j))],
        out_specs=pl.BlockSpec((n, d), lambda j: (0, j)),
        compiler_params=pltpu.CompilerParams(dimension_semantics=("arbitrary",)),
        name="adaln_mod",
    )(c_all, w_ada, b_ada.reshape(1, N_MOD * d))


def _ffn_in_kernel(x_ref, mod_ref, wgu_ref, wdn_ref, win_ref, h_ref, p_ref, *, tf):
    x = x_ref[...]
    sh1, sc1, g1, sh2, sc2 = (mod_ref[j] for j in range(5))
    n1 = _affine(_rms(x), 1.0 + sc1, sh1).astype(_BF16)
    h = x + _affine(_swiglu(n1, wgu_ref, wdn_ref, tf), 0.5 * g1)
    h_ref[...] = h
    n2 = _affine(_rms(h), 1.0 + sc2, sh2).astype(_BF16)
    p_ref[...] = jnp.dot(n2, win_ref[...], preferred_element_type=_F32)


def _resident(shape):
    nd = len(shape)
    return pl.BlockSpec(shape, lambda *_: (0,) * nd, pipeline_mode=pl.Buffered(1))


def _mod_spec(mod, tiles_per_seq):
    _, n_mod, mb, d = mod.shape
    return pl.BlockSpec((None, n_mod, mb, d), lambda i: (i // tiles_per_seq, 0, 0, 0))


def _ffn_in_call(x, mod, wgu, wdn, win, *, tm, tiles_per_seq, tf):
    n, d = x.shape
    pc = win.shape[1]
    return pl.pallas_call(
        functools.partial(_ffn_in_kernel, tf=tf),
        out_shape=(jax.ShapeDtypeStruct((n, d), _F32), jax.ShapeDtypeStruct((n, pc), _F32)),
        grid=(n // tm,),
        in_specs=[pl.BlockSpec((tm, d), lambda i: (i, 0)),
                  _mod_spec(mod, tiles_per_seq),
                  _resident(wgu.shape), _resident(wdn.shape), _resident(win.shape)],
        out_specs=(pl.BlockSpec((tm, d), lambda i: (i, 0)),
                   pl.BlockSpec((tm, pc), lambda i: (i, 0))),
        compiler_params=pltpu.CompilerParams(dimension_semantics=("arbitrary",),
                                             vmem_limit_bytes=_VMEM_LIMIT),
        name="ffn1_inproj",
    )(x, mod, wgu, wdn, win)


def _out_ffn_kernel(ym_ref, h_ref, mod_ref, wout_ref, wgu_ref, wdn_ref, fg_ref, o_ref, *, tf, final):
    g2, sh3, sc3, g3 = (mod_ref[j] for j in range(5, 9))
    h = h_ref[...] + _affine(jnp.dot(ym_ref[...].astype(_BF16), wout_ref[...],
                                     preferred_element_type=_F32), g2)
    n3 = _affine(_rms(h), 1.0 + sc3, sh3).astype(_BF16)
    h = h + _affine(_swiglu(n3, wgu_ref, wdn_ref, tf), 0.5 * g3)
    if final:
        h = _rms(h) * fg_ref[...]
    o_ref[...] = h


def _out_ffn_call(ym, h, mod, wout, wgu, wdn, fg, *, tm, tiles_per_seq, tf, final):
    n, d = h.shape
    return pl.pallas_call(
        functools.partial(_out_ffn_kernel, tf=tf, final=final),
        out_shape=jax.ShapeDtypeStruct((n, d), _F32),
        grid=(n // tm,),
        in_specs=[pl.BlockSpec((tm, ym.shape[1]), lambda i: (i, 0)),
                  pl.BlockSpec((tm, d), lambda i: (i, 0)),
                  _mod_spec(mod, tiles_per_seq),
                  _resident(wout.shape), _resident(wgu.shape), _resident(wdn.shape),
                  _resident(fg.shape)],
        out_specs=pl.BlockSpec((tm, d), lambda i: (i, 0)),
        compiler_params=pltpu.CompilerParams(dimension_semantics=("arbitrary",),
                                             vmem_limit_bytes=_VMEM_LIMIT),
        name="outproj_ffn2",
    )(ym, h, mod, wout, wgu, wdn, fg)


def _block_diag(y, mask):
    reps = mask.shape[0] // y.shape[0]
    return (jnp.concatenate([y] * reps, axis=0) * mask).astype(_BF16)


def _wkv_chunk(rt, at, bt, kt, vv, bdec, kdec, pc, st, mask, strict, incl, diag):
    c = rt.shape[0]
    bb = _block_diag(bt, mask)
    kb = _block_diag(kt, mask)
    x = jnp.concatenate([at, rt], axis=0)
    xb = _dot_nt(x, bb)
    xk = _dot_nt(x, kb)
    lab = xb[:c] * strict
    mrb = xb[c:] * incl
    lak = xk[:c] * strict
    mrk = xk[c:] * incl
    assert c == 4 * INV_BLOCK
    nd = lab * diag
    noff = lab - nd
    tdm = nd
    npow = _dot(nd, _block_diag(nd, mask))
    levels = INV_BLOCK.bit_length() - 2
    for lvl in range(levels):
        nb = _block_diag(npow, mask)
        if lvl + 1 < levels:
            res = _dot(jnp.concatenate([tdm, npow], axis=0), nb)
            tdm = tdm + npow + res[:c]
            npow = res[c:]
        else:
            tdm = tdm + npow + _dot(tdm, nb)
    m1 = noff + _dot(tdm, _block_diag(noff, mask))
    m2 = _dot(m1, _block_diag(m1, mask))
    q = m1 + m2 + _dot(m1, _block_diag(m2, mask))
    tm = q + tdm + _dot(q, _block_diag(tdm, mask))
    vb = _block_diag(vv, mask)
    ap = at + _dot(tm, _block_diag(at, mask))
    w1 = _dot(lak, vb)
    u0 = w1 + _dot(tm, _block_diag(w1, mask))
    ar = _dot_nt(jnp.concatenate([ap, rt], axis=0), st)
    u = ar[:c] + u0
    y = ar[c:] + _dot(mrb, _block_diag(u, mask)) + _dot(mrk, vb)
    upd = _dot_tn(jnp.concatenate([u, vv], axis=0), jnp.concatenate([bdec, kdec], axis=0))
    st_new = st * pc + upd * mask
    return y, st_new


def _mixer_kernel(*refs, tr, trp, has_state, emit_vn, ct, n_heads, head_dim):
    it = iter(refs)
    p_ref = next(it)
    shift0_ref = next(it) if has_state else None
    wkv0_ref = next(it) if has_state else None
    (mu_ref, w0_ref, a0_ref, wwa_ref, wg_ref, kk_ref, ka_ref, rk_ref, gnw_ref, gnb_ref,
     lng_ref, lnb_ref, ws2_ref, bsf_ref) = (next(it) for _ in range(14))
    y_ref = next(it)
    shift_ref = next(it)
    wkv_ref = next(it)
    vn_ref = next(it) if emit_vn else None
    (st_s, carry_s, rt_s, at_s, bt_s, kt_s, vv_s, bd_s, kd_s, pp_s, yy_s) = (next(it) for _ in range(11))

    c = WKV_CHUNK
    gw = HEAD_GROUP * head_dim
    n_groups = n_heads // HEAD_GROUP
    rw = n_heads * head_dim
    ti = pl.program_id(1)
    n_t = pl.num_programs(1)

    @pl.when(ti == 0)
    def _():
        if has_state:
            carry_s[...] = shift0_ref[0]
            zero = jnp.zeros((head_dim, head_dim), _F32)
            for g in range(n_groups):
                rows = []
                for h in range(HEAD_GROUP):
                    blk = wkv0_ref[0, g * HEAD_GROUP + h]
                    rows.append(jnp.concatenate([zero] * h + [blk] + [zero] * (HEAD_GROUP - 1 - h), axis=1))
                st_s[g] = jnp.concatenate(rows, axis=0)
        else:
            carry_s[...] = jnp.zeros_like(carry_s)
            st_s[...] = jnp.zeros_like(st_s)

    ri = lax.broadcasted_iota(jnp.int32, (gw, gw), 0)
    ci = lax.broadcasted_iota(jnp.int32, (gw, gw), 1)
    mask = jnp.where(_div_pow2(ri, head_dim) == _div_pow2(ci, head_dim), 1.0, 0.0).astype(_F32)
    mask_bf = mask.astype(_BF16)
    tt = lax.broadcasted_iota(jnp.int32, (c, HEAD_GROUP * c), 0)
    ss = _mod_pow2(lax.broadcasted_iota(jnp.int32, (c, HEAD_GROUP * c), 1), c)
    strict = jnp.where(ss < tt, 1.0, 0.0).astype(_F32)
    incl = jnp.where(ss <= tt, 1.0, 0.0).astype(_F32)
    diag = jnp.where(_div_pow2(ss, INV_BLOCK) == _div_pow2(tt, INV_BLOCK), strict, 0.0)
    rr = lax.broadcasted_iota(jnp.int32, (trp, trp), 0)
    cc = lax.broadcasted_iota(jnp.int32, (trp, trp), 1)
    same = _div_pow2(rr, c) == _div_pow2(cc, c)
    tri_bf = jnp.where(same & (cc <= rr), 1.0, 0.0).astype(_BF16)
    all_bf = jnp.where(same, 1.0, 0.0).astype(_BF16)

    def segsum(x):
        return jnp.concatenate([_dot_sel_rhs(x[:, g * gw:(g + 1) * gw], mask_bf) for g in range(n_groups)],
                               axis=1)

    p = p_ref[0]
    if trp != tr:
        p = jnp.concatenate([p, jnp.zeros((trp - tr, p.shape[1]), _F32)], axis=0)
    r_cols = mu_ref.shape[1]
    pr = p[:, :r_cols]
    row = lax.broadcasted_iota(jnp.int32, (trp, 1), 0)
    prev = jnp.where(row == 0, carry_s[...], pltpu.roll(pr, 1, 0))
    carry_s[...] = pr[tr - 1:tr]
    xm = pr + (prev - pr) * mu_ref[...]
    r = xm[:, 0:rw]
    k = xm[:, rw:2 * rw]
    v = xm[:, 2 * rw:3 * rw]
    lo_w = wwa_ref.shape[0]
    x_lo = xm[:, 3 * rw:3 * rw + lo_w]
    gd = xm[:, 3 * rw + lo_w:r_cols]
    lane = lax.broadcasted_iota(jnp.int32, (1, lo_w), 1)
    x_lo = jnp.where(lane < lo_w // 2, jnp.tanh(x_lo), x_lo)
    wa = _dot(x_lo, wwa_ref[...])
    wpre = w0_ref[...] + wa[:, :rw]
    w_log = -(jnp.maximum(-wpre, 0.0) + jnp.log1p(jnp.exp(-jnp.abs(wpre)))) - 0.5
    lw = -jnp.exp(w_log)
    a = _sigmoid(a0_ref[...] + wa[:, rw:])
    gate = _dot(_sigmoid(gd), wg_ref[...])
    kkv = k * kk_ref[...]
    kkn = kkv / jnp.maximum(jnp.sqrt(segsum(kkv * kkv)), 1e-12)
    k2 = k * (1.0 + (a - 1.0) * ka_ref[...])
    if trp != tr:
        valid = row < tr
        lw = jnp.where(valid, lw, 0.0)
        kkn = jnp.where(valid, kkn, 0.0)
        k2 = jnp.where(valid, k2, 0.0)
        v = jnp.where(valid, v, 0.0)
    cum = _dot_sel_lhs(tri_bf, lw)
    tot = _dot_sel_lhs(all_bf, lw)
    pinc = jnp.exp(cum)
    pinv = jnp.exp(-cum)
    pend = jnp.exp(tot - cum)
    beta = kkn * a
    rt_s[...] = r * pinc
    at_s[...] = -kkn * jnp.exp(cum - lw)
    bt_s[...] = beta * pinv
    kt_s[...] = k2 * pinv
    vv_s[...] = v
    bd_s[...] = beta * pend
    kd_s[...] = k2 * pend
    pp_s[...] = pinc

    def chunk_step(r0):
        for g in range(n_groups):
            cols = slice(g * gw, (g + 1) * gw)
            rows = pl.ds(r0, c)
            pc = pp_s[pl.ds(r0 + c - 1, 1), cols]
            y, st_new = _wkv_chunk(rt_s[rows, cols], at_s[rows, cols], bt_s[rows, cols], kt_s[rows, cols],
                                   vv_s[rows, cols], bd_s[rows, cols], kd_s[rows, cols], pc, st_s[g],
                                   mask, strict, incl, diag)
            yy_s[rows, cols] = y
            st_s[g] = st_new

    n_chunks = trp // c
    if n_chunks == 1:
        chunk_step(0)
    else:
        def body(i, carry):
            chunk_step(pl.multiple_of(i * c, c))
            return carry
        lax.fori_loop(0, n_chunks, body, 0)

    yw = yy_s[...]
    inv_n = 1.0 / head_dim
    mean = segsum(yw) * inv_n
    dev = yw - mean
    var = segsum(dev * dev) * inv_n
    yn = dev * lax.rsqrt(var + GN_EPS) * gnw_ref[...] + gnb_ref[...]
    bonus = segsum(r * k2 * rk_ref[...]) * vv_s[...]
    y_r = (yn + bonus) * gate

    gwid = lng_ref.shape[1]
    pu = p[:, r_cols:r_cols + gwid]
    pv = p[:, r_cols + gwid:r_cols + 2 * gwid]
    pm = jnp.mean(pv, -1, keepdims=True)
    pd = pv - pm
    pvar = jnp.mean(pd * pd, -1, keepdims=True)
    vn = pd * lax.rsqrt(pvar + LN_EPS) * lng_ref[...] + lnb_ref[...]
    if emit_vn:
        vn_ref[0] = vn[:tr]
    n_pair = ws2_ref.shape[0]
    pw = gwid // n_pair
    tr2 = lax.broadcasted_iota(jnp.int32, (ct, 2 * ct), 0)
    tc2 = _mod_pow2(lax.broadcasted_iota(jnp.int32, (ct, 2 * ct), 1), ct)
    tril2 = jnp.where(tc2 <= tr2, 1.0, 0.0).astype(_F32)
    first = lax.broadcasted_iota(jnp.int32, (1, pw), 1) < pw // 2
    mixed_rows = []
    for q in range(trp // ct):
        outs = []
        for j in range(n_pair):
            vp = vn[q * ct:(q + 1) * ct, j * pw:(j + 1) * pw]
            rhs = jnp.concatenate([jnp.where(first, vp, 0.0), jnp.where(first, 0.0, vp)], axis=0)
            outs.append(_dot(ws2_ref[j] * tril2, rhs))
        mixed_rows.append(jnp.concatenate(outs, axis=1) + bsf_ref[...])
    mixed = mixed_rows[0] if len(mixed_rows) == 1 else jnp.concatenate(mixed_rows, axis=0)
    y_c = pu * mixed
    y_ref[0] = jnp.concatenate([y_r, y_c], axis=1)[:tr]

    @pl.when(ti == n_t - 1)
    def _():
        shift_ref[0] = carry_s[...]
        for g in range(n_groups):
            st = st_s[g]
            for h in range(HEAD_GROUP):
                wkv_ref[0, g * HEAD_GROUP + h] = st[h * head_dim:(h + 1) * head_dim,
                                                    h * head_dim:(h + 1) * head_dim]


def _mixer_call(p, shift0, wkv0, prm, *, tr, n_heads, head_dim, emit_vn):
    nseq, t, pcols = p.shape
    has_state = shift0 is not None
    trp = -(-tr // WKV_CHUNK) * WKV_CHUNK
    ct = min(trp, GMLP_CHUNK)
    rw = n_heads * head_dim
    r_cols = prm["mu"].shape[1]
    gwid = prm["lng"].shape[1]
    gw = HEAD_GROUP * head_dim
    n_groups = n_heads // HEAD_GROUP

    ws = prm["w_s"][:, :ct, :ct]
    hc = ws.shape[0]
    ws2 = ws.reshape(hc // 2, 2, ct, ct).transpose(0, 2, 1, 3).reshape(hc // 2, ct, 2 * ct)
    bsf = jnp.repeat(prm["b_s"][:, :ct].T, gwid // hc, axis=1)

    def full(a):
        nd = a.ndim
        return pl.BlockSpec(a.shape, lambda b, i: (0,) * nd)

    params = [prm["mu"], prm["w0"], prm["a0"], prm["wwa"], prm["wg"], prm["k_k"], prm["k_a"], prm["r_k"],
              prm["gn_w"], prm["gn_b"], prm["lng"], prm["lnb"], ws2, bsf]
    inputs = [p]
    in_specs = [pl.BlockSpec((1, tr, pcols), lambda b, i: (b, i, 0))]
    if has_state:
        inputs += [shift0, wkv0]
        in_specs += [pl.BlockSpec((1, 1, r_cols), lambda b, i: (b, 0, 0)),
                     pl.BlockSpec((1, n_heads, head_dim, head_dim), lambda b, i: (b, 0, 0, 0))]
    inputs += params
    in_specs += [full(a) for a in params]

    out_shape = [jax.ShapeDtypeStruct((nseq, t, rw + gwid), _F32),
                 jax.ShapeDtypeStruct((nseq, 1, r_cols), _F32),
                 jax.ShapeDtypeStruct((nseq, n_heads, head_dim, head_dim), _F32)]
    out_specs = [pl.BlockSpec((1, tr, rw + gwid), lambda b, i: (b, i, 0)),
                 pl.BlockSpec((1, 1, r_cols), lambda b, i: (b, 0, 0)),
                 pl.BlockSpec((1, n_heads, head_dim, head_dim), lambda b, i: (b, 0, 0, 0))]
    if emit_vn:
        out_shape.append(jax.ShapeDtypeStruct((nseq, t, gwid), _F32))
        out_specs.append(pl.BlockSpec((1, tr, gwid), lambda b, i: (b, i, 0)))

    scratch = [pltpu.VMEM((n_groups, gw, gw), _F32), pltpu.VMEM((1, r_cols), _F32)]
    scratch += [pltpu.VMEM((trp, rw), _F32) for _ in range(9)]

    return pl.pallas_call(
        functools.partial(_mixer_kernel, tr=tr, trp=trp, has_state=has_state, emit_vn=emit_vn, ct=ct,
                          n_heads=n_heads, head_dim=head_dim),
        out_shape=tuple(out_shape),
        grid=(nseq, t // tr),
        in_specs=in_specs,
        out_specs=tuple(out_specs),
        scratch_shapes=scratch,
        compiler_params=pltpu.CompilerParams(dimension_semantics=("arbitrary", "arbitrary"),
                                             vmem_limit_bytes=_VMEM_LIMIT),
        name="mixers_state" if has_state else "mixers_fresh",
    )(*inputs)


def _pick_tile(n, pref):
    t = min(n, pref)
    while n % t:
        t //= 2
    return t


def kernel(x_prompt, x_sample, state_shift, state_wkv, c_prompt, c_sample, w_ada, b_ada, ffn1_gu, ffn1_dn, w_in, mu_shift, w0, w_lora_up, a0, a_lora_up, g_lora_up, k_k, k_a, r_k, gn_w, gn_b, ln_v_g, ln_v_b, w_s, b_s, w_out, ffn2_gu, ffn2_dn, final_g):
    depth = w_ada.shape[0]
    bp, tp, d = x_prompt.shape
    bs, ts, _ = x_sample.shape
    n_heads, head_dim = r_k.shape[1], r_k.shape[2]
    rw = n_heads * head_dim
    w_rank, a_rank = w_lora_up.shape[1], a_lora_up.shape[1]
    assert w_rank == a_rank and n_heads % HEAD_GROUP == 0 and head_dim == WKV_CHUNK
    d_ff = ffn1_dn.shape[1]
    tf = 256 if d_ff % 256 == 0 else 128

    xp = x_prompt.reshape(bp * tp, d)
    xs = x_sample.transpose(1, 0, 2).reshape(ts * bs, d)
    n_c = bp + bs
    n_c_pad = -(-n_c // 16) * 16
    c_all = jnp.concatenate([c_prompt, c_sample, jnp.zeros((n_c_pad - n_c, d), _F32)], axis=0)

    tm_p = _pick_tile(tp, 512)
    tr_p = _pick_tile(tp, 256)
    tm_s = ts * bs

    shp, wkp, shs, wks, cvs = [], [], [], [], []
    for l in range(depth):
        mod = _mod_call(c_all, w_ada[l], b_ada[l])
        mod_p = mod[:bp].reshape(bp, N_MOD, 1, d)
        mod_s = mod[bp:n_c].reshape(bs, N_MOD, d).transpose(1, 0, 2)[None]

        wgu1, wdn1 = ffn1_gu[l].astype(_BF16), ffn1_dn[l].astype(_BF16)
        wgu2, wdn2 = ffn2_gu[l].astype(_BF16), ffn2_dn[l].astype(_BF16)
        win, wout = w_in[l].astype(_BF16), w_out[l].astype(_BF16)
        zw = jnp.zeros((w_rank, rw), _F32)
        wwa = jnp.concatenate([jnp.concatenate([w_lora_up[l], zw], axis=1),
                               jnp.concatenate([zw, a_lora_up[l]], axis=1)], axis=0).astype(_BF16)
        row = lambda a: a.reshape(1, -1)
        prm = dict(mu=row(mu_shift[l]), w0=row(w0[l]), a0=row(a0[l]), wwa=wwa,
                   wg=g_lora_up[l].astype(_BF16), k_k=row(k_k[l]), k_a=row(k_a[l]), r_k=row(r_k[l]),
                   gn_w=row(gn_w[l]), gn_b=row(gn_b[l]), lng=row(ln_v_g[l]), lnb=row(ln_v_b[l]),
                   w_s=w_s[l], b_s=b_s[l])
        fg = final_g.reshape(1, d)
        final = l == depth - 1

        hp, pp = _ffn_in_call(xp, mod_p, wgu1, wdn1, win, tm=tm_p, tiles_per_seq=tp // tm_p, tf=tf)
        ymp, sh_p, wk_p = _mixer_call(pp.reshape(bp, tp, -1), None, None, prm, tr=tr_p,
                                      n_heads=n_heads, head_dim=head_dim, emit_vn=False)
        xp = _out_ffn_call(ymp.reshape(bp * tp, -1), hp, mod_p, wout, wgu2, wdn2, fg,
                           tm=tm_p, tiles_per_seq=tp // tm_p, tf=tf, final=final)

        hs, ps = _ffn_in_call(xs, mod_s, wgu1, wdn1, win, tm=tm_s, tiles_per_seq=1, tf=tf)
        ps_bm = ps.reshape(ts, bs, -1).transpose(1, 0, 2)
        yms, sh_s, wk_s, v_s = _mixer_call(ps_bm, state_shift[l][:, None, :], state_wkv[l], prm, tr=ts,
                                           n_heads=n_heads, head_dim=head_dim, emit_vn=True)
        yms_tm = yms.transpose(1, 0, 2).reshape(ts * bs, -1)
        xs = _out_ffn_call(yms_tm, hs, mod_s, wout, wgu2, wdn2, fg,
                           tm=tm_s, tiles_per_seq=1, tf=tf, final=final)

        shp.append(sh_p[:, 0]); wkp.append(wk_p); shs.append(sh_s[:, 0]); wks.append(wk_s); cvs.append(v_s)

    y_prompt = xp.reshape(bp, tp, d)
    y_sample = xs.reshape(ts, bs, d).transpose(1, 0, 2)
    return (y_prompt, y_sample, jnp.stack(shp), jnp.stack(wkp), jnp.stack(shs), jnp.stack(wks), jnp.stack(cvs))
```

```python
import functools

import jax
import jax.numpy as jnp
from jax import lax
from jax.experimental import pallas as pl
from jax.experimental.pallas import tpu as pltpu

_F32 = jnp.float32
_BF16 = jnp.bfloat16

RMS_EPS = 1e-6
LN_EPS = 1e-5
GN_EPS = 64e-5
N_MOD = 9
WKV_CHUNK = 64
HEAD_GROUP = 4
INV_BLOCK = 16
GMLP_CHUNK = 128

_V7X_VMEM_BYTES = 64 * 1024 * 1024
_VMEM_LIMIT = _V7X_VMEM_BYTES - 8 * 1024 * 1024


def _dot(a, b):
    return jnp.dot(a.astype(_BF16), b.astype(_BF16), preferred_element_type=_F32)


def _dot_nt(a, b):
    return lax.dot_general(a.astype(_BF16), b.astype(_BF16), (((1,), (1,)), ((), ())),
                           preferred_element_type=_F32)


def _dot_tn(a, b):
    return lax.dot_general(a.astype(_BF16), b.astype(_BF16), (((0,), (0,)), ((), ())),
                           preferred_element_type=_F32)


def _split_hi_lo(x):
    hi = x.astype(_BF16)
    lo = (x - hi.astype(_F32)).astype(_BF16)
    return hi, lo


def _dot_sel_rhs(x, sel):
    hi, lo = _split_hi_lo(x)
    return (jnp.dot(hi, sel, preferred_element_type=_F32)
            + jnp.dot(lo, sel, preferred_element_type=_F32))


def _dot_sel_lhs(sel, x):
    hi, lo = _split_hi_lo(x)
    return (jnp.dot(sel, hi, preferred_element_type=_F32)
            + jnp.dot(sel, lo, preferred_element_type=_F32))


def _div_pow2(x, n):
    assert n & (n - 1) == 0
    return lax.shift_right_logical(x, jnp.int32(n.bit_length() - 1))


def _mod_pow2(x, n):
    assert n & (n - 1) == 0
    return lax.bitwise_and(x, jnp.int32(n - 1))


def _rms(x):
    return x * lax.rsqrt(jnp.mean(x * x, -1, keepdims=True) + RMS_EPS)


def _sigmoid(x):
    return 1.0 / (1.0 + jnp.exp(-x))


def _affine(x, scale, shift=None):
    mb = scale.shape[0]
    if mb == 1:
        y = x * scale
        return y if shift is None else y + shift
    tm, d = x.shape
    y = x.reshape(tm // mb, mb, d) * scale[None]
    if shift is not None:
        y = y + shift[None]
    return y.reshape(tm, d)


def _swiglu(n_bf, wgu_ref, wdn_ref, tf):
    d_ff = wdn_ref.shape[0]
    acc = None
    for j in range(d_ff // tf):
        g = jnp.dot(n_bf, wgu_ref[:, j * tf:(j + 1) * tf], preferred_element_type=_F32)
        u = jnp.dot(n_bf, wgu_ref[:, d_ff + j * tf:d_ff + (j + 1) * tf], preferred_element_type=_F32)
        hm = (g * _sigmoid(g) * u).astype(_BF16)
        part = jnp.dot(hm, wdn_ref[j * tf:(j + 1) * tf, :], preferred_element_type=_F32)
        acc = part if acc is None else acc + part
    return acc


def _mod_kernel(c_ref, w_ref, b_ref, o_ref):
    c = c_ref[...]
    s = c * _sigmoid(c)
    o_ref[...] = _dot(s, w_ref[...]) + b_ref[...]


def _mod_call(c_all, w_ada, b_ada):
    n, d = c_all.shape
    return pl.pallas_call(
        _mod_kernel,
        out_shape=jax.ShapeDtypeStruct((n, N_MOD * d), _F32),
        grid=(N_MOD,),
        in_specs=[pl.BlockSpec((n, d), lambda j: (0, 0)),
                  pl.BlockSpec((d, d), lambda j: (0, j)),
                  pl.BlockSpec((1, d), lambda j: (0, j))],
        out_specs=pl.BlockSpec((n, d), lambda j: (0, j)),
        compiler_params=pltpu.CompilerParams(dimension_semantics=("arbitrary",)),
        name="adaln_mod",
    )(c_all, w_ada, b_ada.reshape(1, N_MOD * d))


def _ffn_in_kernel(x_ref, mod_ref, wgu_ref, wdn_ref, win_ref, h_ref, p_ref, *, tf):
    x = x_ref[...]
    sh1, sc1, g1, sh2, sc2 = (mod_ref[j] for j in range(5))
    n1 = _affine(_rms(x), 1.0 + sc1, sh1).astype(_BF16)
    h = x + _affine(_swiglu(n1, wgu_ref, wdn_ref, tf), 0.5 * g1)
    h_ref[...] = h
    n2 = _affine(_rms(h), 1.0 + sc2, sh2).astype(_BF16)
    p_ref[...] = jnp.dot(n2, win_ref[...], preferred_element_type=_F32)


def _resident(shape):
    nd = len(shape)
    return pl.BlockSpec(shape, lambda *_: (0,) * nd, pipeline_mode=pl.Buffered(1))


def _mod_spec(mod, tiles_per_seq):
    _, n_mod, mb, d = mod.shape
    return pl.BlockSpec((None, n_mod, mb, d), lambda i: (i // tiles_per_seq, 0, 0, 0))


def _ffn_in_call(x, mod, wgu, wdn, win, *, tm, tiles_per_seq, tf):
    n, d = x.shape
    pc = win.shape[1]
    return pl.pallas_call(
        functools.partial(_ffn_in_kernel, tf=tf),
        out_shape=(jax.ShapeDtypeStruct((n, d), _F32), jax.ShapeDtypeStruct((n, pc), _F32)),
        grid=(n // tm,),
        in_specs=[pl.BlockSpec((tm, d), lambda i: (i, 0)),
                  _mod_spec(mod, tiles_per_seq),
                  _resident(wgu.shape), _resident(wdn.shape), _resident(win.shape)],
        out_specs=(pl.BlockSpec((tm, d), lambda i: (i, 0)),
                   pl.BlockSpec((tm, pc), lambda i: (i, 0))),
        compiler_params=pltpu.CompilerParams(dimension_semantics=("arbitrary",),
                                             vmem_limit_bytes=_VMEM_LIMIT),
        name="ffn1_inproj",
    )(x, mod, wgu, wdn, win)


def _out_ffn_kernel(ym_ref, h_ref, mod_ref, wout_ref, wgu_ref, wdn_ref, fg_ref, o_ref, *, tf, final):
    g2, sh3, sc3, g3 = (mod_ref[j] for j in range(5, 9))
    h = h_ref[...] + _affine(jnp.dot(ym_ref[...].astype(_BF16), wout_ref[...],
                                     preferred_element_type=_F32), g2)
    n3 = _affine(_rms(h), 1.0 + sc3, sh3).astype(_BF16)
    h = h + _affine(_swiglu(n3, wgu_ref, wdn_ref, tf), 0.5 * g3)
    if final:
        h = _rms(h) * fg_ref[...]
    o_ref[...] = h


def _out_ffn_call(ym, h, mod, wout, wgu, wdn, fg, *, tm, tiles_per_seq, tf, final):
    n, d = h.shape
    return pl.pallas_call(
        functools.partial(_out_ffn_kernel, tf=tf, final=final),
        out_shape=jax.ShapeDtypeStruct((n, d), _F32),
        grid=(n // tm,),
        in_specs=[pl.BlockSpec((tm, ym.shape[1]), lambda i: (i, 0)),
                  pl.BlockSpec((tm, d), lambda i: (i, 0)),
                  _mod_spec(mod, tiles_per_seq),
                  _resident(wout.shape), _resident(wgu.shape), _resident(wdn.shape),
                  _resident(fg.shape)],
        out_specs=pl.BlockSpec((tm, d), lambda i: (i, 0)),
        compiler_params=pltpu.CompilerParams(dimension_semantics=("arbitrary",),
                                             vmem_limit_bytes=_VMEM_LIMIT),
        name="outproj_ffn2",
    )(ym, h, mod, wout, wgu, wdn, fg)


def _block_diag(y, mask):
    reps = mask.shape[0] // y.shape[0]
    return (jnp.concatenate([y] * reps, axis=0) * mask).astype(_BF16)


def _wkv_local(rt, at, bt, kt, vv, mask, strict, incl, diag):
    n = len(rt)
    c = rt[0].shape[0]
    every = range(n)
    bd = lambda xs: [_block_diag(x, mask) for x in xs]
    x = [jnp.concatenate([at[i], rt[i]], axis=0) for i in every]
    bb, kb = bd(bt), bd(kt)
    xb = [_dot_nt(x[i], bb[i]) for i in every]
    xk = [_dot_nt(x[i], kb[i]) for i in every]
    lab = [xb[i][:c] * strict for i in every]
    mrb = [xb[i][c:] * incl for i in every]
    lak = [xk[i][:c] * strict for i in every]
    mrk = [xk[i][c:] * incl for i in every]
    assert c == 4 * INV_BLOCK
    nd = [lab[i] * diag for i in every]
    noff = [lab[i] - nd[i] for i in every]
    tdm = nd
    ndb = bd(nd)
    npow = [_dot(nd[i], ndb[i]) for i in every]
    levels = INV_BLOCK.bit_length() - 2
    for lvl in range(levels):
        nb = bd(npow)
        if lvl + 1 < levels:
            res = [_dot(jnp.concatenate([tdm[i], npow[i]], axis=0), nb[i]) for i in every]
            tdm = [tdm[i] + npow[i] + res[i][:c] for i in every]
            npow = [res[i][c:] for i in every]
        else:
            tdm = [tdm[i] + npow[i] + _dot(tdm[i], nb[i]) for i in every]
    noffb = bd(noff)
    m1 = [noff[i] + _dot(tdm[i], noffb[i]) for i in every]
    m1b = bd(m1)
    m2 = [_dot(m1[i], m1b[i]) for i in every]
    m2b = bd(m2)
    q = [m1[i] + m2[i] + _dot(m1[i], m2b[i]) for i in every]
    tdmb = bd(tdm)
    tm = [q[i] + tdm[i] + _dot(q[i], tdmb[i]) for i in every]
    vb, atb = bd(vv), bd(at)
    ap = [at[i] + _dot(tm[i], atb[i]) for i in every]
    w1 = [_dot(lak[i], vb[i]) for i in every]
    w1b = bd(w1)
    u0 = [w1[i] + _dot(tm[i], w1b[i]) for i in every]
    y0 = [_dot(mrk[i], vb[i]) for i in every]
    return [(ap[i], u0[i], mrb[i], y0[i]) for i in every]


def _wkv_chain(loc, rt, vv, bdec, kdec, pc, st, mask):
    every = range(len(rt))
    c = rt[0].shape[0]
    ar = [_dot_nt(jnp.concatenate([loc[i][0], rt[i]], axis=0), st[i]) for i in every]
    u = [ar[i][:c] + loc[i][1] for i in every]
    upd = [_dot_tn(jnp.concatenate([u[i], vv[i]], axis=0), jnp.concatenate([bdec[i], kdec[i]], axis=0))
           for i in every]
    st_new = [st[i] * pc[i] + upd[i] * mask for i in every]
    ub = [_block_diag(u[i], mask) for i in every]
    y = [ar[i][c:] + _dot(loc[i][2], ub[i]) + loc[i][3] for i in every]
    return y, st_new


def _mixer_kernel(*refs, nb, tr, trp, has_state, emit_vn, ct, n_heads, head_dim):
    it = iter(refs)
    p_ref = next(it)
    shift0_ref = next(it) if has_state else None
    wkv0_ref = next(it) if has_state else None
    (mu_ref, w0_ref, a0_ref, wwa_ref, wg_ref, kk_ref, ka_ref, rk_ref, gnw_ref, gnb_ref,
     lng_ref, lnb_ref, ws2_ref, bsf_ref) = (next(it) for _ in range(14))
    y_ref = next(it)
    shift_ref = next(it)
    wkv_ref = next(it)
    vn_ref = next(it) if emit_vn else None
    st_s = next(it)
    carry_s = next(it)

    c = WKV_CHUNK
    gw = HEAD_GROUP * head_dim
    n_groups = n_heads // HEAD_GROUP
    rw = n_heads * head_dim
    rows = nb * trp
    ti = pl.program_id(1)
    n_t = pl.num_programs(1)

    @pl.when(ti == 0)
    def _():
        if has_state:
            carry_s[...] = shift0_ref[...]
            zero = jnp.zeros((head_dim, head_dim), _F32)
            for s in range(nb):
                for g in range(n_groups):
                    blocks = []
                    for h in range(HEAD_GROUP):
                        blk = wkv0_ref[s, g * HEAD_GROUP + h]
                        blocks.append(jnp.concatenate([zero] * h + [blk] + [zero] * (HEAD_GROUP - 1 - h),
                                                      axis=1))
                    st_s[s, g] = jnp.concatenate(blocks, axis=0)
        else:
            carry_s[...] = jnp.zeros_like(carry_s)
            st_s[...] = jnp.zeros_like(st_s)

    ri = lax.broadcasted_iota(jnp.int32, (gw, gw), 0)
    ci = lax.broadcasted_iota(jnp.int32, (gw, gw), 1)
    mask = jnp.where(_div_pow2(ri, head_dim) == _div_pow2(ci, head_dim), 1.0, 0.0).astype(_F32)
    mask_bf = mask.astype(_BF16)
    tt = lax.broadcasted_iota(jnp.int32, (c, HEAD_GROUP * c), 0)
    ss = _mod_pow2(lax.broadcasted_iota(jnp.int32, (c, HEAD_GROUP * c), 1), c)
    strict = jnp.where(ss < tt, 1.0, 0.0).astype(_F32)
    incl = jnp.where(ss <= tt, 1.0, 0.0).astype(_F32)
    diag = jnp.where(_div_pow2(ss, INV_BLOCK) == _div_pow2(tt, INV_BLOCK), strict, 0.0)
    rr = lax.broadcasted_iota(jnp.int32, (rows, rows), 0)
    cc = lax.broadcasted_iota(jnp.int32, (rows, rows), 1)
    same = _div_pow2(rr, c) == _div_pow2(cc, c)
    tri_bf = jnp.where(same & (cc <= rr), 1.0, 0.0).astype(_BF16)
    all_bf = jnp.where(same, 1.0, 0.0).astype(_BF16)

    def segsum(x):
        return jnp.concatenate([_dot_sel_rhs(x[:, g * gw:(g + 1) * gw], mask_bf) for g in range(n_groups)],
                               axis=1)

    pad = trp - tr
    parts = []
    for s in range(nb):
        parts.append(p_ref[s])
        if pad:
            parts.append(jnp.zeros((pad, p_ref.shape[2]), _F32))
    p = parts[0] if len(parts) == 1 else jnp.concatenate(parts, axis=0)
    r_cols = mu_ref.shape[1]
    pr = p[:, :r_cols]
    trow = _mod_pow2(lax.broadcasted_iota(jnp.int32, (rows, 1), 0), trp)
    if nb == 1:
        first = carry_s[0]
    else:
        first = jnp.concatenate([jnp.broadcast_to(carry_s[s], (trp, r_cols)) for s in range(nb)], axis=0)
    prev = jnp.where(trow == 0, first, pltpu.roll(pr, 1, 0))
    for s in range(nb):
        carry_s[s] = pr[s * trp + tr - 1:s * trp + tr]
    xm = pr + (prev - pr) * mu_ref[...]
    r = xm[:, 0:rw]
    k = xm[:, rw:2 * rw]
    v = xm[:, 2 * rw:3 * rw]
    lo_w = wwa_ref.shape[0]
    x_lo = xm[:, 3 * rw:3 * rw + lo_w]
    gd = xm[:, 3 * rw + lo_w:r_cols]
    lane = lax.broadcasted_iota(jnp.int32, (1, lo_w), 1)
    x_lo = jnp.where(lane < lo_w // 2, jnp.tanh(x_lo), x_lo)
    wa = _dot(x_lo, wwa_ref[...])
    wpre = w0_ref[...] + wa[:, :rw]
    w_log = -(jnp.maximum(-wpre, 0.0) + jnp.log1p(jnp.exp(-jnp.abs(wpre)))) - 0.5
    lw = -jnp.exp(w_log)
    a = _sigmoid(a0_ref[...] + wa[:, rw:])
    gate = _dot(_sigmoid(gd), wg_ref[...])
    kkv = k * kk_ref[...]
    kkn = kkv / jnp.maximum(jnp.sqrt(segsum(kkv * kkv)), 1e-12)
    k2 = k * (1.0 + (a - 1.0) * ka_ref[...])
    if pad:
        valid = trow < tr
        lw = jnp.where(valid, lw, 0.0)
        kkn = jnp.where(valid, kkn, 0.0)
        k2 = jnp.where(valid, k2, 0.0)
        v = jnp.where(valid, v, 0.0)
    cum = _dot_sel_lhs(tri_bf, lw)
    tot = _dot_sel_lhs(all_bf, lw)
    pinc = jnp.exp(cum)
    pinv = jnp.exp(-cum)
    pend = jnp.exp(tot - cum)
    beta = kkn * a
    rt = r * pinc
    at = -kkn * jnp.exp(cum - lw)
    bt = beta * pinv
    kt = k2 * pinv
    bdec = beta * pend
    kdec = k2 * pend

    n_chunks = rows // c
    chunks_per_seq = trp // c

    def blk(x, j, g):
        return x[j * c:(j + 1) * c, g * gw:(g + 1) * gw]

    probs = [(j, g) for j in range(n_chunks) for g in range(n_groups)]
    cut = lambda x, ps: [blk(x, j, g) for j, g in ps]
    local = dict(zip(probs, _wkv_local(cut(rt, probs), cut(at, probs), cut(bt, probs), cut(kt, probs),
                                       cut(v, probs), mask, strict, incl, diag)))
    y_blk = {}
    for jj in range(chunks_per_seq):
        ps = [(s * chunks_per_seq + jj, g) for s in range(nb) for g in range(n_groups)]
        sts = [st_s[j // chunks_per_seq, g] for j, g in ps]
        pcs = [blk(pinc, j, g)[c - 1:c] for j, g in ps]
        ys, st_new = _wkv_chain([local[pr_] for pr_ in ps], cut(rt, ps), cut(v, ps), cut(bdec, ps),
                                cut(kdec, ps), pcs, sts, mask)
        for (j, g), y, st in zip(ps, ys, st_new):
            st_s[j // chunks_per_seq, g] = st
            y_blk[j, g] = y
    y_rows = [jnp.concatenate([y_blk[j, g] for g in range(n_groups)], axis=1) for j in range(n_chunks)]
    yw = y_rows[0] if n_chunks == 1 else jnp.concatenate(y_rows, axis=0)

    inv_n = 1.0 / head_dim
    mean = segsum(yw) * inv_n
    dev = yw - mean
    var = segsum(dev * dev) * inv_n
    yn = dev * lax.rsqrt(var + GN_EPS) * gnw_ref[...] + gnb_ref[...]
    bonus = segsum(r * k2 * rk_ref[...]) * v
    y_r = (yn + bonus) * gate

    gwid = lng_ref.shape[1]
    pu = p[:, r_cols:r_cols + gwid]
    pv = p[:, r_cols + gwid:r_cols + 2 * gwid]
    pm = jnp.mean(pv, -1, keepdims=True)
    pd = pv - pm
    pvar = jnp.mean(pd * pd, -1, keepdims=True)
    vn = pd * lax.rsqrt(pvar + LN_EPS) * lng_ref[...] + lnb_ref[...]
    n_pair = ws2_ref.shape[0]
    pw = gwid // n_pair
    tr2 = lax.broadcasted_iota(jnp.int32, (ct, 2 * ct), 0)
    tc2 = _mod_pow2(lax.broadcasted_iota(jnp.int32, (ct, 2 * ct), 1), ct)
    tril2 = jnp.where(tc2 <= tr2, 1.0, 0.0).astype(_F32)
    first_half = lax.broadcasted_iota(jnp.int32, (1, pw), 1) < pw // 2
    ws_tril = [(ws2_ref[j] * tril2).astype(_BF16) for j in range(n_pair)]
    mixed_rows = []
    for q in range(rows // ct):
        outs = []
        for j in range(n_pair):
            vp = vn[q * ct:(q + 1) * ct, j * pw:(j + 1) * pw]
            rhs = jnp.concatenate([jnp.where(first_half, vp, 0.0), jnp.where(first_half, 0.0, vp)], axis=0)
            outs.append(_dot(ws_tril[j], rhs))
        mixed_rows.append(jnp.concatenate(outs, axis=1) + bsf_ref[...])
    mixed = mixed_rows[0] if len(mixed_rows) == 1 else jnp.concatenate(mixed_rows, axis=0)
    y_all = jnp.concatenate([y_r, pu * mixed], axis=1)
    for s in range(nb):
        y_ref[s] = y_all[s * trp:s * trp + tr]
        if emit_vn:
            vn_ref[s] = vn[s * trp:s * trp + tr]

    @pl.when(ti == n_t - 1)
    def _():
        shift_ref[...] = carry_s[...]
        for s in range(nb):
            for g in range(n_groups):
                st = st_s[s, g]
                for h in range(HEAD_GROUP):
                    wkv_ref[s, g * HEAD_GROUP + h] = st[h * head_dim:(h + 1) * head_dim,
                                                        h * head_dim:(h + 1) * head_dim]


def _mixer_call(p, shift0, wkv0, prm, *, nb, tr, n_heads, head_dim, emit_vn):
    nseq, t, pcols = p.shape
    has_state = shift0 is not None
    trp = -(-tr // WKV_CHUNK) * WKV_CHUNK
    ct = min(trp, GMLP_CHUNK)
    rw = n_heads * head_dim
    r_cols = prm["mu"].shape[1]
    gwid = prm["lng"].shape[1]
    gw = HEAD_GROUP * head_dim
    n_groups = n_heads // HEAD_GROUP
    assert nseq % nb == 0 and t % tr == 0 and (nb == 1 or t == tr)

    ws = prm["w_s"][:, :ct, :ct]
    hc = ws.shape[0]
    ws2 = ws.reshape(hc // 2, 2, ct, ct).transpose(0, 2, 1, 3).reshape(hc // 2, ct, 2 * ct)
    bsf = jnp.repeat(prm["b_s"][:, :ct].T, gwid // hc, axis=1)

    def full(a):
        nd = a.ndim
        return pl.BlockSpec(a.shape, lambda b, i: (0,) * nd)

    params = [prm["mu"], prm["w0"], prm["a0"], prm["wwa"], prm["wg"], prm["k_k"], prm["k_a"], prm["r_k"],
              prm["gn_w"], prm["gn_b"], prm["lng"], prm["lnb"], ws2, bsf]
    inputs = [p]
    in_specs = [pl.BlockSpec((nb, tr, pcols), lambda b, i: (b, i, 0))]
    if has_state:
        inputs += [shift0, wkv0]
        in_specs += [pl.BlockSpec((nb, 1, r_cols), lambda b, i: (b, 0, 0)),
                     pl.BlockSpec((nb, n_heads, head_dim, head_dim), lambda b, i: (b, 0, 0, 0))]
    inputs += params
    in_specs += [full(a) for a in params]

    out_shape = [jax.ShapeDtypeStruct((nseq, t, rw + gwid), _F32),
                 jax.ShapeDtypeStruct((nseq, 1, r_cols), _F32),
                 jax.ShapeDtypeStruct((nseq, n_heads, head_dim, head_dim), _F32)]
    out_specs = [pl.BlockSpec((nb, tr, rw + gwid), lambda b, i: (b, i, 0)),
                 pl.BlockSpec((nb, 1, r_cols), lambda b, i: (b, 0, 0)),
                 pl.BlockSpec((nb, n_heads, head_dim, head_dim), lambda b, i: (b, 0, 0, 0))]
    if emit_vn:
        out_shape.append(jax.ShapeDtypeStruct((nseq, t, gwid), _F32))
        out_specs.append(pl.BlockSpec((nb, tr, gwid), lambda b, i: (b, i, 0)))

    scratch = [pltpu.VMEM((nb, n_groups, gw, gw), _F32), pltpu.VMEM((nb, 1, r_cols), _F32)]

    return pl.pallas_call(
        functools.partial(_mixer_kernel, nb=nb, tr=tr, trp=trp, has_state=has_state, emit_vn=emit_vn, ct=ct,
                          n_heads=n_heads, head_dim=head_dim),
        out_shape=tuple(out_shape),
        grid=(nseq // nb, t // tr),
        in_specs=in_specs,
        out_specs=tuple(out_specs),
        scratch_shapes=scratch,
        compiler_params=pltpu.CompilerParams(dimension_semantics=("arbitrary", "arbitrary"),
                                             vmem_limit_bytes=_VMEM_LIMIT),
        name="mixers_state" if has_state else "mixers_fresh",
    )(*inputs)


def _pick_tile(n, pref):
    t = min(n, pref)
    while n % t:
        t //= 2
    return t


def kernel(x_prompt, x_sample, state_shift, state_wkv, c_prompt, c_sample, w_ada, b_ada, ffn1_gu, ffn1_dn, w_in, mu_shift, w0, w_lora_up, a0, a_lora_up, g_lora_up, k_k, k_a, r_k, gn_w, gn_b, ln_v_g, ln_v_b, w_s, b_s, w_out, ffn2_gu, ffn2_dn, final_g):
    depth = w_ada.shape[0]
    bp, tp, d = x_prompt.shape
    bs, ts, _ = x_sample.shape
    n_heads, head_dim = r_k.shape[1], r_k.shape[2]
    rw = n_heads * head_dim
    w_rank, a_rank = w_lora_up.shape[1], a_lora_up.shape[1]
    assert w_rank == a_rank and n_heads % HEAD_GROUP == 0 and head_dim == WKV_CHUNK
    d_ff = ffn1_dn.shape[1]
    tf = 256 if d_ff % 256 == 0 else 128

    xp = x_prompt.reshape(bp * tp, d)
    xs = x_sample.transpose(1, 0, 2).reshape(ts * bs, d)
    n_c = bp + bs
    n_c_pad = -(-n_c // 16) * 16
    c_all = jnp.concatenate([c_prompt, c_sample, jnp.zeros((n_c_pad - n_c, d), _F32)], axis=0)

    tm_p = _pick_tile(tp, 512)
    tr_p = _pick_tile(tp, 256)
    tm_s = ts * bs
    nb_s = _pick_tile(bs, 4)

    shp, wkp, shs, wks, cvs = [], [], [], [], []
    for l in range(depth):
        mod = _mod_call(c_all, w_ada[l], b_ada[l])
        mod_p = mod[:bp].reshape(bp, N_MOD, 1, d)
        mod_s = mod[bp:n_c].reshape(bs, N_MOD, d).transpose(1, 0, 2)[None]

        wgu1, wdn1 = ffn1_gu[l].astype(_BF16), ffn1_dn[l].astype(_BF16)
        wgu2, wdn2 = ffn2_gu[l].astype(_BF16), ffn2_dn[l].astype(_BF16)
        win, wout = w_in[l].astype(_BF16), w_out[l].astype(_BF16)
        zw = jnp.zeros((w_rank, rw), _F32)
        wwa = jnp.concatenate([jnp.concatenate([w_lora_up[l], zw], axis=1),
                               jnp.concatenate([zw, a_lora_up[l]], axis=1)], axis=0).astype(_BF16)
        row = lambda a: a.reshape(1, -1)
        prm = dict(mu=row(mu_shift[l]), w0=row(w0[l]), a0=row(a0[l]), wwa=wwa,
                   wg=g_lora_up[l].astype(_BF16), k_k=row(k_k[l]), k_a=row(k_a[l]), r_k=row(r_k[l]),
                   gn_w=row(gn_w[l]), gn_b=row(gn_b[l]), lng=row(ln_v_g[l]), lnb=row(ln_v_b[l]),
                   w_s=w_s[l], b_s=b_s[l])
        fg = final_g.reshape(1, d)
        final = l == depth - 1

        hp, pp = _ffn_in_call(xp, mod_p, wgu1, wdn1, win, tm=tm_p, tiles_per_seq=tp // tm_p, tf=tf)
        ymp, sh_p, wk_p = _mixer_call(pp.reshape(bp, tp, -1), None, None, prm, nb=1, tr=tr_p,
                                      n_heads=n_heads, head_dim=head_dim, emit_vn=False)
        xp = _out_ffn_call(ymp.reshape(bp * tp, -1), hp, mod_p, wout, wgu2, wdn2, fg,
                           tm=tm_p, tiles_per_seq=tp // tm_p, tf=tf, final=final)

        hs, ps = _ffn_in_call(xs, mod_s, wgu1, wdn1, win, tm=tm_s, tiles_per_seq=1, tf=tf)
        ps_bm = ps.reshape(ts, bs, -1).transpose(1, 0, 2)
        yms, sh_s, wk_s, v_s = _mixer_call(ps_bm, state_shift[l][:, None, :], state_wkv[l], prm, nb=nb_s, tr=ts,
                                           n_heads=n_heads, head_dim=head_dim, emit_vn=True)
        yms_tm = yms.transpose(1, 0, 2).reshape(ts * bs, -1)
        xs = _out_ffn_call(yms_tm, hs, mod_s, wout, wgu2, wdn2, fg,
                           tm=tm_s, tiles_per_seq=1, tf=tf, final=final)

        shp.append(sh_p[:, 0]); wkp.append(wk_p); shs.append(sh_s[:, 0]); wks.append(wk_s); cvs.append(v_s)

    y_prompt = xp.reshape(bp, tp, d)
    y_sample = xs.reshape(ts, bs, d).transpose(1, 0, 2)
    return (y_prompt, y_sample, jnp.stack(shp), jnp.stack(wkp), jnp.stack(shs), jnp.stack(wks), jnp.stack(cvs))
```

```python
import functools

import jax
import jax.numpy as jnp
import numpy as np
from jax import lax
from jax.experimental import pallas as pl
from jax.experimental.pallas import tpu as pltpu

_F32 = jnp.float32
_BF16 = jnp.bfloat16

RMS_EPS = 1e-6
LN_EPS = 1e-5
GN_EPS = 64e-5
N_MOD = 9
WKV_CHUNK = 64
HEAD_GROUP = 4
INV_BLOCK = 16
GMLP_CHUNK = 128
SEG_ROWS = 256

_V7X_VMEM_BYTES = 64 * 1024 * 1024
_VMEM_LIMIT = _V7X_VMEM_BYTES - 8 * 1024 * 1024


def _dot(a, b):
    return jnp.dot(a.astype(_BF16), b.astype(_BF16), preferred_element_type=_F32)


def _dot_nt(a, b):
    return lax.dot_general(a.astype(_BF16), b.astype(_BF16), (((1,), (1,)), ((), ())),
                           preferred_element_type=_F32)


def _dot_tn(a, b):
    return lax.dot_general(a.astype(_BF16), b.astype(_BF16), (((0,), (0,)), ((), ())),
                           preferred_element_type=_F32)


def _split_hi_lo(x):
    hi = x.astype(_BF16)
    lo = (x - hi.astype(_F32)).astype(_BF16)
    return hi, lo


def _dot_sel_rhs(x, sel):
    hi, lo = _split_hi_lo(x)
    return (jnp.dot(hi, sel, preferred_element_type=_F32)
            + jnp.dot(lo, sel, preferred_element_type=_F32))


def _dot_sel_lhs(sel, x):
    hi, lo = _split_hi_lo(x)
    return (jnp.dot(sel, hi, preferred_element_type=_F32)
            + jnp.dot(sel, lo, preferred_element_type=_F32))


def _div_pow2(x, n):
    assert n & (n - 1) == 0
    return lax.shift_right_logical(x, jnp.int32(n.bit_length() - 1))


def _mod_pow2(x, n):
    assert n & (n - 1) == 0
    return lax.bitwise_and(x, jnp.int32(n - 1))


def _rms(x):
    return x * lax.rsqrt(jnp.mean(x * x, -1, keepdims=True) + RMS_EPS)


def _sigmoid(x):
    return 1.0 / (1.0 + jnp.exp(-x))


def _affine(x, scale, shift=None):
    mb = scale.shape[0]
    if mb == 1:
        y = x * scale
        return y if shift is None else y + shift
    tm, d = x.shape
    y = x.reshape(tm // mb, mb, d) * scale[None]
    if shift is not None:
        y = y + shift[None]
    return y.reshape(tm, d)


def _swiglu(n_bf, wgu_ref, wdn_ref, tf):
    d_ff = wdn_ref.shape[0]
    acc = None
    for j in range(d_ff // tf):
        g = jnp.dot(n_bf, wgu_ref[:, j * tf:(j + 1) * tf], preferred_element_type=_F32)
        u = jnp.dot(n_bf, wgu_ref[:, d_ff + j * tf:d_ff + (j + 1) * tf], preferred_element_type=_F32)
        hm = (g * _sigmoid(g) * u).astype(_BF16)
        part = jnp.dot(hm, wdn_ref[j * tf:(j + 1) * tf, :], preferred_element_type=_F32)
        acc = part if acc is None else acc + part
    return acc


def _mod_kernel(c_ref, w_ref, b_ref, o_ref):
    c = c_ref[...]
    s = c * _sigmoid(c)
    o_ref[...] = _dot(s, w_ref[...]) + b_ref[...]


def _mod_call(c_all, w_ada, b_ada):
    n, d = c_all.shape
    return pl.pallas_call(
        _mod_kernel,
        out_shape=jax.ShapeDtypeStruct((n, N_MOD * d), _F32),
        grid=(N_MOD,),
        in_specs=[pl.BlockSpec((n, d), lambda j: (0, 0)),
                  pl.BlockSpec((d, d), lambda j: (0, j)),
                  pl.BlockSpec((1, d), lambda j: (0, j))],
        out_specs=pl.BlockSpec((n, d), lambda j: (0, j)),
        compiler_params=pltpu.CompilerParams(dimension_semantics=("arbitrary",)),
        name="adaln_mod",
    )(c_all, w_ada, b_ada.reshape(1, N_MOD * d))


def _ffn_in_kernel(x_ref, mod_ref, wgu_ref, wdn_ref, win_ref, h_ref, p_ref, *, tf):
    x = x_ref[...]
    sh1, sc1, g1, sh2, sc2 = (mod_ref[j] for j in range(5))
    n1 = _affine(_rms(x), 1.0 + sc1, sh1).astype(_BF16)
    h = x + _affine(_swiglu(n1, wgu_ref, wdn_ref, tf), 0.5 * g1)
    h_ref[...] = h
    n2 = _affine(_rms(h), 1.0 + sc2, sh2).astype(_BF16)
    p_ref[...] = jnp.dot(n2, win_ref[...], preferred_element_type=_F32)


def _resident(shape):
    nd = len(shape)
    return pl.BlockSpec(shape, lambda *_: (0,) * nd, pipeline_mode=pl.Buffered(1))


def _mod_spec(mod, tiles_per_seq):
    _, n_mod, mb, d = mod.shape
    return pl.BlockSpec((None, n_mod, mb, d), lambda i: (i // tiles_per_seq, 0, 0, 0))


def _ffn_in_call(x, mod, wgu, wdn, win, *, tm, tiles_per_seq, tf):
    n, d = x.shape
    pc = win.shape[1]
    return pl.pallas_call(
        functools.partial(_ffn_in_kernel, tf=tf),
        out_shape=(jax.ShapeDtypeStruct((n, d), _F32), jax.ShapeDtypeStruct((n, pc), _F32)),
        grid=(n // tm,),
        in_specs=[pl.BlockSpec((tm, d), lambda i: (i, 0)),
                  _mod_spec(mod, tiles_per_seq),
                  _resident(wgu.shape), _resident(wdn.shape), _resident(win.shape)],
        out_specs=(pl.BlockSpec((tm, d), lambda i: (i, 0)),
                   pl.BlockSpec((tm, pc), lambda i: (i, 0))),
        compiler_params=pltpu.CompilerParams(dimension_semantics=("arbitrary",),
                                             vmem_limit_bytes=_VMEM_LIMIT),
        name="ffn1_inproj",
    )(x, mod, wgu, wdn, win)


def _out_ffn_kernel(ym_ref, h_ref, mod_ref, wout_ref, wgu_ref, wdn_ref, fg_ref, o_ref, *, tf, final):
    g2, sh3, sc3, g3 = (mod_ref[j] for j in range(5, 9))
    h = h_ref[...] + _affine(jnp.dot(ym_ref[...].astype(_BF16), wout_ref[...],
                                     preferred_element_type=_F32), g2)
    n3 = _affine(_rms(h), 1.0 + sc3, sh3).astype(_BF16)
    h = h + _affine(_swiglu(n3, wgu_ref, wdn_ref, tf), 0.5 * g3)
    if final:
        h = _rms(h) * fg_ref[...]
    o_ref[...] = h


def _out_ffn_call(ym, h, mod, wout, wgu, wdn, fg, *, tm, tiles_per_seq, tf, final):
    n, d = h.shape
    return pl.pallas_call(
        functools.partial(_out_ffn_kernel, tf=tf, final=final),
        out_shape=jax.ShapeDtypeStruct((n, d), _F32),
        grid=(n // tm,),
        in_specs=[pl.BlockSpec((tm, ym.shape[1]), lambda i: (i, 0)),
                  pl.BlockSpec((tm, d), lambda i: (i, 0)),
                  _mod_spec(mod, tiles_per_seq),
                  _resident(wout.shape), _resident(wgu.shape), _resident(wdn.shape),
                  _resident(fg.shape)],
        out_specs=pl.BlockSpec((tm, d), lambda i: (i, 0)),
        compiler_params=pltpu.CompilerParams(dimension_semantics=("arbitrary",),
                                             vmem_limit_bytes=_VMEM_LIMIT),
        name="outproj_ffn2",
    )(ym, h, mod, wout, wgu, wdn, fg)


def _block_diag(y, mask_bf):
    reps = mask_bf.shape[0] // y.shape[0]
    return jnp.concatenate([y.astype(_BF16)] * reps, axis=0) * mask_bf


def _wkv_local(rt, at, bt, kt, vv, mask_bf, strict, incl, diag):
    n = len(rt)
    c = rt[0].shape[0]
    every = range(n)
    bd = lambda xs: [_block_diag(x, mask_bf) for x in xs]
    x = [jnp.concatenate([at[i], rt[i]], axis=0) for i in every]
    bb, kb = bd(bt), bd(kt)
    xb = [_dot_nt(x[i], bb[i]) for i in every]
    xk = [_dot_nt(x[i], kb[i]) for i in every]
    lab = [xb[i][:c] * strict for i in every]
    mrb = [xb[i][c:] * incl for i in every]
    lak = [xk[i][:c] * strict for i in every]
    mrk = [xk[i][c:] * incl for i in every]
    assert c == 4 * INV_BLOCK
    nd = [lab[i] * diag for i in every]
    noff = [lab[i] - nd[i] for i in every]
    tdm = nd
    ndb = bd(nd)
    npow = [_dot(nd[i], ndb[i]) for i in every]
    levels = INV_BLOCK.bit_length() - 2
    for lvl in range(levels):
        nb = bd(npow)
        if lvl + 1 < levels:
            res = [_dot(jnp.concatenate([tdm[i], npow[i]], axis=0), nb[i]) for i in every]
            tdm = [tdm[i] + npow[i] + res[i][:c] for i in every]
            npow = [res[i][c:] for i in every]
        else:
            tdm = [tdm[i] + npow[i] + _dot(tdm[i], nb[i]) for i in every]
    noffb = bd(noff)
    m1 = [noff[i] + _dot(tdm[i], noffb[i]) for i in every]
    m1b = bd(m1)
    m2 = [_dot(m1[i], m1b[i]) for i in every]
    m2b = bd(m2)
    q = [m1[i] + m2[i] + _dot(m1[i], m2b[i]) for i in every]
    tdmb = bd(tdm)
    tm = [q[i] + tdm[i] + _dot(q[i], tdmb[i]) for i in every]
    vb, atb = bd(vv), bd(at)
    ap = [at[i] + _dot(tm[i], atb[i]) for i in every]
    w1 = [_dot(lak[i], vb[i]) for i in every]
    w1b = bd(w1)
    u0 = [w1[i] + _dot(tm[i], w1b[i]) for i in every]
    y0 = [_dot(mrk[i], vb[i]) for i in every]
    return [(ap[i], u0[i], mrb[i], y0[i]) for i in every]


def _wkv_chain(loc, rt, vv, bdec, kdec, pc, st, mask, mask_bf):
    every = range(len(rt))
    c = rt[0].shape[0]
    ar = [_dot_nt(jnp.concatenate([loc[i][0], rt[i]], axis=0), st[i]) for i in every]
    u = [ar[i][:c] + loc[i][1] for i in every]
    upd = [_dot_tn(jnp.concatenate([u[i], vv[i]], axis=0), jnp.concatenate([bdec[i], kdec[i]], axis=0))
           for i in every]
    st_new = [st[i] * pc[i] + upd[i] * mask for i in every]
    ub = [_block_diag(u[i], mask_bf) for i in every]
    y = [ar[i][c:] + _dot(loc[i][2], ub[i]) + loc[i][3] for i in every]
    return y, st_new


def _mixer_kernel(*refs, nb, tr, trp, has_state, emit_vn, ct, n_heads, head_dim):
    it = iter(refs)
    p_ref = next(it)
    shift0_ref = next(it) if has_state else None
    wkv0_ref = next(it) if has_state else None
    (mu_ref, w0_ref, a0_ref, wwa_ref, wg_ref, kk_ref, ka_ref, rk_ref, gnw_ref, gnb_ref,
     lng_ref, lnb_ref, ws2_ref, bsf_ref, hmask_ref, hmaskb_ref, tmask_ref, segm_ref, tril2_ref) = (
        next(it) for _ in range(19))
    y_ref = next(it)
    shift_ref = next(it)
    wkv_ref = next(it)
    vn_ref = next(it) if emit_vn else None
    st_s = next(it)
    carry_s = next(it)

    c = WKV_CHUNK
    gw = HEAD_GROUP * head_dim
    n_groups = n_heads // HEAD_GROUP
    rw = n_heads * head_dim
    rows = nb * trp
    ti = pl.program_id(1)
    n_t = pl.num_programs(1)

    @pl.when(ti == 0)
    def _():
        if has_state:
            carry_s[...] = shift0_ref[...]
            zero = jnp.zeros((head_dim, head_dim), _F32)
            for s in range(nb):
                for g in range(n_groups):
                    blocks = []
                    for h in range(HEAD_GROUP):
                        blk = wkv0_ref[s, g * HEAD_GROUP + h]
                        blocks.append(jnp.concatenate([zero] * h + [blk] + [zero] * (HEAD_GROUP - 1 - h),
                                                      axis=1))
                    st_s[s, g] = jnp.concatenate(blocks, axis=0)
        else:
            carry_s[...] = jnp.zeros_like(carry_s)
            st_s[...] = jnp.zeros_like(st_s)

    mask = hmask_ref[...]
    mask_bf = hmaskb_ref[...]
    strict, incl, diag = tmask_ref[0], tmask_ref[1], tmask_ref[2]
    tri_bf, all_bf = segm_ref[0], segm_ref[1]

    def segsum(x):
        return jnp.concatenate([_dot_sel_rhs(x[:, g * gw:(g + 1) * gw], mask_bf) for g in range(n_groups)],
                               axis=1)

    pad = trp - tr
    parts = []
    for s in range(nb):
        parts.append(p_ref[s])
        if pad:
            parts.append(jnp.zeros((pad, p_ref.shape[2]), _F32))
    p = parts[0] if len(parts) == 1 else jnp.concatenate(parts, axis=0)
    r_cols = mu_ref.shape[1]
    pr = p[:, :r_cols]
    trow = _mod_pow2(lax.broadcasted_iota(jnp.int32, (rows, 1), 0), trp)
    if nb == 1:
        first = carry_s[0]
    else:
        first = jnp.concatenate([jnp.broadcast_to(carry_s[s], (trp, r_cols)) for s in range(nb)], axis=0)
    prev = jnp.where(trow == 0, first, pltpu.roll(pr, 1, 0))
    for s in range(nb):
        carry_s[s] = pr[s * trp + tr - 1:s * trp + tr]
    xm = pr + (prev - pr) * mu_ref[...]
    r = xm[:, 0:rw]
    k = xm[:, rw:2 * rw]
    v = xm[:, 2 * rw:3 * rw]
    lo_w = wwa_ref.shape[0]
    x_lo = xm[:, 3 * rw:3 * rw + lo_w]
    gd = xm[:, 3 * rw + lo_w:r_cols]
    lane = lax.broadcasted_iota(jnp.int32, (1, lo_w), 1)
    x_lo = jnp.where(lane < lo_w // 2, jnp.tanh(x_lo), x_lo)
    wa = _dot(x_lo, wwa_ref[...])
    wpre = w0_ref[...] + wa[:, :rw]
    w_log = -(jnp.maximum(-wpre, 0.0) + jnp.log(1.0 + jnp.exp(-jnp.abs(wpre)))) - 0.5
    lw = -jnp.exp(w_log)
    a = _sigmoid(a0_ref[...] + wa[:, rw:])
    gate = _dot(_sigmoid(gd), wg_ref[...])
    kkv = k * kk_ref[...]
    kkn = kkv * lax.rsqrt(jnp.maximum(segsum(kkv * kkv), 1e-24))
    k2 = k * (1.0 + (a - 1.0) * ka_ref[...])
    if pad:
        valid = trow < tr
        lw = jnp.where(valid, lw, 0.0)
        kkn = jnp.where(valid, kkn, 0.0)
        k2 = jnp.where(valid, k2, 0.0)
        v = jnp.where(valid, v, 0.0)
    sr = tri_bf.shape[0]
    lw_blocks = [lw[i * sr:(i + 1) * sr] for i in range(rows // sr)]
    cum = jnp.concatenate([_dot_sel_lhs(tri_bf, x) for x in lw_blocks], axis=0)
    tot = jnp.concatenate([_dot_sel_lhs(all_bf, x) for x in lw_blocks], axis=0)
    pinc = jnp.exp(cum)
    pinv = jnp.exp(-cum)
    pend = jnp.exp(tot - cum)
    beta = kkn * a
    rt = r * pinc
    at = -kkn * jnp.exp(cum - lw)
    bt = beta * pinv
    kt = k2 * pinv
    bdec = beta * pend
    kdec = k2 * pend

    n_chunks = rows // c
    chunks_per_seq = trp // c

    def blk(x, j, g):
        return x[j * c:(j + 1) * c, g * gw:(g + 1) * gw]

    probs = [(j, g) for j in range(n_chunks) for g in range(n_groups)]
    cut = lambda x, ps: [blk(x, j, g) for j, g in ps]
    local = dict(zip(probs, _wkv_local(cut(rt, probs), cut(at, probs), cut(bt, probs), cut(kt, probs),
                                       cut(v, probs), mask_bf, strict, incl, diag)))
    y_blk = {}
    for jj in range(chunks_per_seq):
        ps = [(s * chunks_per_seq + jj, g) for s in range(nb) for g in range(n_groups)]
        sts = [st_s[j // chunks_per_seq, g] for j, g in ps]
        pcs = [blk(pinc, j, g)[c - 1:c] for j, g in ps]
        ys, st_new = _wkv_chain([local[pr_] for pr_ in ps], cut(rt, ps), cut(v, ps), cut(bdec, ps),
                                cut(kdec, ps), pcs, sts, mask, mask_bf)
        for (j, g), y, st in zip(ps, ys, st_new):
            st_s[j // chunks_per_seq, g] = st
            y_blk[j, g] = y
    y_rows = [jnp.concatenate([y_blk[j, g] for g in range(n_groups)], axis=1) for j in range(n_chunks)]
    yw = y_rows[0] if n_chunks == 1 else jnp.concatenate(y_rows, axis=0)

    inv_n = 1.0 / head_dim
    mean = segsum(yw) * inv_n
    dev = yw - mean
    var = segsum(dev * dev) * inv_n
    yn = dev * lax.rsqrt(var + GN_EPS) * gnw_ref[...] + gnb_ref[...]
    bonus = segsum(r * k2 * rk_ref[...]) * v
    y_r = (yn + bonus) * gate

    gwid = lng_ref.shape[1]
    pu = p[:, r_cols:r_cols + gwid]
    pv = p[:, r_cols + gwid:r_cols + 2 * gwid]
    pm = jnp.mean(pv, -1, keepdims=True)
    pd = pv - pm
    pvar = jnp.mean(pd * pd, -1, keepdims=True)
    vn = pd * lax.rsqrt(pvar + LN_EPS) * lng_ref[...] + lnb_ref[...]
    n_pair = ws2_ref.shape[0]
    pw = gwid // n_pair
    tril2 = tril2_ref[...]
    first_half = lax.broadcasted_iota(jnp.int32, (1, pw), 1) < pw // 2
    ws_tril = [(ws2_ref[j] * tril2).astype(_BF16) for j in range(n_pair)]
    mixed_rows = []
    for q in range(rows // ct):
        outs = []
        for j in range(n_pair):
            vp = vn[q * ct:(q + 1) * ct, j * pw:(j + 1) * pw]
            rhs = jnp.concatenate([jnp.where(first_half, vp, 0.0), jnp.where(first_half, 0.0, vp)], axis=0)
            outs.append(_dot(ws_tril[j], rhs))
        mixed_rows.append(jnp.concatenate(outs, axis=1) + bsf_ref[...])
    mixed = mixed_rows[0] if len(mixed_rows) == 1 else jnp.concatenate(mixed_rows, axis=0)
    y_all = jnp.concatenate([y_r, pu * mixed], axis=1)
    for s in range(nb):
        y_ref[s] = y_all[s * trp:s * trp + tr]
        if emit_vn:
            vn_ref[s] = vn[s * trp:s * trp + tr]

    @pl.when(ti == n_t - 1)
    def _():
        shift_ref[...] = carry_s[...]
        for s in range(nb):
            for g in range(n_groups):
                st = st_s[s, g]
                for h in range(HEAD_GROUP):
                    wkv_ref[s, g * HEAD_GROUP + h] = st[h * head_dim:(h + 1) * head_dim,
                                                        h * head_dim:(h + 1) * head_dim]


def _mixer_masks(rows, ct, head_dim):
    c, gw = WKV_CHUNK, HEAD_GROUP * head_dim
    blk = np.arange(gw) // head_dim
    hmask = (blk[:, None] == blk[None, :]).astype(np.float32)
    t = np.arange(c)[:, None]
    s = np.arange(HEAD_GROUP * c)[None, :] % c
    strict = s < t
    tmask = np.stack([strict, s <= t, strict & (s // INV_BLOCK == t // INV_BLOCK)]).astype(np.float32)
    r = np.arange(rows)
    same = (r[:, None] // c) == (r[None, :] // c)
    segm = np.stack([same & (r[None, :] <= r[:, None]), same]).astype(np.float32)
    tril2 = ((np.arange(2 * ct)[None, :] % ct) <= np.arange(ct)[:, None]).astype(np.float32)
    return [jnp.asarray(hmask), jnp.asarray(hmask, _BF16), jnp.asarray(tmask), jnp.asarray(segm, _BF16),
            jnp.asarray(tril2)]


def _mixer_call(p, shift0, wkv0, prm, *, nb, tr, n_heads, head_dim, emit_vn):
    nseq, t, pcols = p.shape
    has_state = shift0 is not None
    trp = -(-tr // WKV_CHUNK) * WKV_CHUNK
    ct = min(trp, GMLP_CHUNK)
    rw = n_heads * head_dim
    r_cols = prm["mu"].shape[1]
    gwid = prm["lng"].shape[1]
    gw = HEAD_GROUP * head_dim
    n_groups = n_heads // HEAD_GROUP
    assert nseq % nb == 0 and t % tr == 0

    ws = prm["w_s"][:, :ct, :ct]
    hc = ws.shape[0]
    ws2 = ws.reshape(hc // 2, 2, ct, ct).transpose(0, 2, 1, 3).reshape(hc // 2, ct, 2 * ct)
    bsf = jnp.repeat(prm["b_s"][:, :ct].T, gwid // hc, axis=1)

    def full(a):
        nd = a.ndim
        return pl.BlockSpec(a.shape, lambda b, i: (0,) * nd)

    params = [prm["mu"], prm["w0"], prm["a0"], prm["wwa"], prm["wg"], prm["k_k"], prm["k_a"], prm["r_k"],
              prm["gn_w"], prm["gn_b"], prm["lng"], prm["lnb"], ws2, bsf]
    rows = nb * trp
    params += _mixer_masks(SEG_ROWS if rows % SEG_ROWS == 0 else rows, ct, head_dim)
    inputs = [p]
    in_specs = [pl.BlockSpec((nb, tr, pcols), lambda b, i: (b, i, 0))]
    if has_state:
        inputs += [shift0, wkv0]
        in_specs += [pl.BlockSpec((nb, 1, r_cols), lambda b, i: (b, 0, 0)),
                     pl.BlockSpec((nb, n_heads, head_dim, head_dim), lambda b, i: (b, 0, 0, 0))]
    inputs += params
    in_specs += [full(a) for a in params]

    out_shape = [jax.ShapeDtypeStruct((nseq, t, rw + gwid), _F32),
                 jax.ShapeDtypeStruct((nseq, 1, r_cols), _F32),
                 jax.ShapeDtypeStruct((nseq, n_heads, head_dim, head_dim), _F32)]
    out_specs = [pl.BlockSpec((nb, tr, rw + gwid), lambda b, i: (b, i, 0)),
                 pl.BlockSpec((nb, 1, r_cols), lambda b, i: (b, 0, 0)),
                 pl.BlockSpec((nb, n_heads, head_dim, head_dim), lambda b, i: (b, 0, 0, 0))]
    if emit_vn:
        out_shape.append(jax.ShapeDtypeStruct((nseq, t, gwid), _F32))
        out_specs.append(pl.BlockSpec((nb, tr, gwid), lambda b, i: (b, i, 0)))

    scratch = [pltpu.VMEM((nb, n_groups, gw, gw), _F32), pltpu.VMEM((nb, 1, r_cols), _F32)]

    return pl.pallas_call(
        functools.partial(_mixer_kernel, nb=nb, tr=tr, trp=trp, has_state=has_state, emit_vn=emit_vn, ct=ct,
                          n_heads=n_heads, head_dim=head_dim),
        out_shape=tuple(out_shape),
        grid=(nseq // nb, t // tr),
        in_specs=in_specs,
        out_specs=tuple(out_specs),
        scratch_shapes=scratch,
        compiler_params=pltpu.CompilerParams(dimension_semantics=("arbitrary", "arbitrary"),
                                             vmem_limit_bytes=_VMEM_LIMIT),
        name="mixers_state" if has_state else "mixers_fresh",
    )(*inputs)


def _pick_tile(n, pref):
    t = min(n, pref)
    while n % t:
        t //= 2
    return t


def kernel(x_prompt, x_sample, state_shift, state_wkv, c_prompt, c_sample, w_ada, b_ada, ffn1_gu, ffn1_dn, w_in, mu_shift, w0, w_lora_up, a0, a_lora_up, g_lora_up, k_k, k_a, r_k, gn_w, gn_b, ln_v_g, ln_v_b, w_s, b_s, w_out, ffn2_gu, ffn2_dn, final_g):
    depth = w_ada.shape[0]
    bp, tp, d = x_prompt.shape
    bs, ts, _ = x_sample.shape
    n_heads, head_dim = r_k.shape[1], r_k.shape[2]
    rw = n_heads * head_dim
    w_rank, a_rank = w_lora_up.shape[1], a_lora_up.shape[1]
    assert w_rank == a_rank and n_heads % HEAD_GROUP == 0 and head_dim == WKV_CHUNK
    d_ff = ffn1_dn.shape[1]
    tf = 256 if d_ff % 256 == 0 else 128

    xp = x_prompt.reshape(bp * tp, d)
    xs = x_sample.transpose(1, 0, 2).reshape(ts * bs, d)
    n_c = bp + bs
    n_c_pad = -(-n_c // 16) * 16
    c_all = jnp.concatenate([c_prompt, c_sample, jnp.zeros((n_c_pad - n_c, d), _F32)], axis=0)

    tm_p = _pick_tile(tp, 512)
    tr_p = _pick_tile(tp, 256)
    tm_s = ts * bs
    nb_p = _pick_tile(bp, 2)
    nb_s = _pick_tile(bs, 8)

    shp, wkp, shs, wks, cvs = [], [], [], [], []
    for l in range(depth):
        mod = _mod_call(c_all, w_ada[l], b_ada[l])
        mod_p = mod[:bp].reshape(bp, N_MOD, 1, d)
        mod_s = mod[bp:n_c].reshape(bs, N_MOD, d).transpose(1, 0, 2)[None]

        wgu1, wdn1 = ffn1_gu[l].astype(_BF16), ffn1_dn[l].astype(_BF16)
        wgu2, wdn2 = ffn2_gu[l].astype(_BF16), ffn2_dn[l].astype(_BF16)
        win, wout = w_in[l].astype(_BF16), w_out[l].astype(_BF16)
        zw = jnp.zeros((w_rank, rw), _F32)
        wwa = jnp.concatenate([jnp.concatenate([w_lora_up[l], zw], axis=1),
                               jnp.concatenate([zw, a_lora_up[l]], axis=1)], axis=0).astype(_BF16)
        row = lambda a: a.reshape(1, -1)
        prm = dict(mu=row(mu_shift[l]), w0=row(w0[l]), a0=row(a0[l]), wwa=wwa,
                   wg=g_lora_up[l].astype(_BF16), k_k=row(k_k[l]), k_a=row(k_a[l]), r_k=row(r_k[l]),
                   gn_w=row(gn_w[l]), gn_b=row(gn_b[l]), lng=row(ln_v_g[l]), lnb=row(ln_v_b[l]),
                   w_s=w_s[l], b_s=b_s[l])
        fg = final_g.reshape(1, d)
        final = l == depth - 1

        hp, pp = _ffn_in_call(xp, mod_p, wgu1, wdn1, win, tm=tm_p, tiles_per_seq=tp // tm_p, tf=tf)
        ymp, sh_p, wk_p = _mixer_call(pp.reshape(bp, tp, -1), None, None, prm, nb=nb_p, tr=tr_p,
                                      n_heads=n_heads, head_dim=head_dim, emit_vn=False)
        xp = _out_ffn_call(ymp.reshape(bp * tp, -1), hp, mod_p, wout, wgu2, wdn2, fg,
                           tm=tm_p, tiles_per_seq=tp // tm_p, tf=tf, final=final)

        hs, ps = _ffn_in_call(xs, mod_s, wgu1, wdn1, win, tm=tm_s, tiles_per_seq=1, tf=tf)
        ps_bm = ps.reshape(ts, bs, -1).transpose(1, 0, 2)
        yms, sh_s, wk_s, v_s = _mixer_call(ps_bm, state_shift[l][:, None, :], state_wkv[l], prm, nb=nb_s, tr=ts,
                                           n_heads=n_heads, head_dim=head_dim, emit_vn=True)
        yms_tm = yms.transpose(1, 0, 2).reshape(ts * bs, -1)
        xs = _out_ffn_call(yms_tm, hs, mod_s, wout, wgu2, wdn2, fg,
                           tm=tm_s, tiles_per_seq=1, tf=tf, final=final)

        shp.append(sh_p[:, 0]); wkp.append(wk_p); shs.append(sh_s[:, 0]); wks.append(wk_s); cvs.append(v_s)

    y_prompt = xp.reshape(bp, tp, d)
    y_sample = xs.reshape(ts, bs, d).transpose(1, 0, 2)
    return (y_prompt, y_sample, jnp.stack(shp), jnp.stack(wkp), jnp.stack(shs), jnp.stack(wks), jnp.stack(cvs))
```

```python
import functools
import math

import jax
import jax.numpy as jnp
import numpy as np
from jax import lax
from jax.experimental import pallas as pl
from jax.experimental.pallas import tpu as pltpu

_F32 = jnp.float32
_BF16 = jnp.bfloat16

RMS_EPS = 1e-6
LN_EPS = 1e-5
GN_EPS = 64e-5
N_MOD = 9
WKV_CHUNK = 64
HEAD_GROUP = 4
INV_BLOCK = 16
GMLP_CHUNK = 128
SEG_ROWS = 256
DECAY_SCALE = math.exp(-0.5)

_V7X_VMEM_BYTES = 64 * 1024 * 1024
_VMEM_LIMIT = _V7X_VMEM_BYTES - 8 * 1024 * 1024


def _dot(a, b):
    return jnp.dot(a.astype(_BF16), b.astype(_BF16), preferred_element_type=_F32)


def _dot_nt(a, b):
    return lax.dot_general(a.astype(_BF16), b.astype(_BF16), (((1,), (1,)), ((), ())),
                           preferred_element_type=_F32)


def _dot_tn(a, b):
    return lax.dot_general(a.astype(_BF16), b.astype(_BF16), (((0,), (0,)), ((), ())),
                           preferred_element_type=_F32)


def _split_hi_lo(x):
    hi = x.astype(_BF16)
    lo = (x - hi.astype(_F32)).astype(_BF16)
    return hi, lo


def _dot_sel_rhs(x, sel):
    hi, lo = _split_hi_lo(x)
    return (jnp.dot(hi, sel, preferred_element_type=_F32)
            + jnp.dot(lo, sel, preferred_element_type=_F32))


def _dot_sel_lhs(sel, x):
    hi, lo = _split_hi_lo(x)
    return (jnp.dot(sel, hi, preferred_element_type=_F32)
            + jnp.dot(sel, lo, preferred_element_type=_F32))


def _div_pow2(x, n):
    assert n & (n - 1) == 0
    return lax.shift_right_logical(x, jnp.int32(n.bit_length() - 1))


def _mod_pow2(x, n):
    assert n & (n - 1) == 0
    return lax.bitwise_and(x, jnp.int32(n - 1))


def _rms(x):
    return x * lax.rsqrt(jnp.mean(x * x, -1, keepdims=True) + RMS_EPS)


def _sigmoid(x):
    return 1.0 / (1.0 + jnp.exp(-x))


def _affine(x, scale, shift=None):
    mb = scale.shape[0]
    if mb == 1:
        y = x * scale
        return y if shift is None else y + shift
    tm, d = x.shape
    y = x.reshape(tm // mb, mb, d) * scale[None]
    if shift is not None:
        y = y + shift[None]
    return y.reshape(tm, d)


def _swiglu(n_bf, wgu_ref, wdn_ref, tf):
    d_ff = wdn_ref.shape[0]
    acc = None
    for j in range(d_ff // tf):
        g = jnp.dot(n_bf, wgu_ref[:, j * tf:(j + 1) * tf], preferred_element_type=_F32)
        u = jnp.dot(n_bf, wgu_ref[:, d_ff + j * tf:d_ff + (j + 1) * tf], preferred_element_type=_F32)
        hm = (g * _sigmoid(g) * u).astype(_BF16)
        part = jnp.dot(hm, wdn_ref[j * tf:(j + 1) * tf, :], preferred_element_type=_F32)
        acc = part if acc is None else acc + part
    return acc


def _mod_kernel(c_ref, w_ref, b_ref, o_ref):
    c = c_ref[...]
    s = c * _sigmoid(c)
    o_ref[...] = _dot(s, w_ref[...]) + b_ref[...]


def _mod_call(c_all, w_ada, b_ada):
    n, d = c_all.shape
    return pl.pallas_call(
        _mod_kernel,
        out_shape=jax.ShapeDtypeStruct((n, N_MOD * d), _F32),
        grid=(N_MOD,),
        in_specs=[pl.BlockSpec((n, d), lambda j: (0, 0)),
                  pl.BlockSpec((d, d), lambda j: (0, j)),
                  pl.BlockSpec((1, d), lambda j: (0, j))],
        out_specs=pl.BlockSpec((n, d), lambda j: (0, j)),
        compiler_params=pltpu.CompilerParams(dimension_semantics=("arbitrary",)),
        name="adaln_mod",
    )(c_all, w_ada, b_ada.reshape(1, N_MOD * d))


def _ffn_in_kernel(x_ref, mod_ref, wgu_ref, wdn_ref, win_ref, h_ref, p_ref, *, tf):
    x = x_ref[...]
    sh1, sc1, g1, sh2, sc2 = (mod_ref[j] for j in range(5))
    n1 = _affine(_rms(x), 1.0 + sc1, sh1).astype(_BF16)
    h = x + _affine(_swiglu(n1, wgu_ref, wdn_ref, tf), 0.5 * g1)
    h_ref[...] = h
    n2 = _affine(_rms(h), 1.0 + sc2, sh2).astype(_BF16)
    p_ref[...] = jnp.dot(n2, win_ref[...], preferred_element_type=_F32)


def _resident(shape):
    nd = len(shape)
    return pl.BlockSpec(shape, lambda *_: (0,) * nd, pipeline_mode=pl.Buffered(1))


def _mod_spec(mod, tiles_per_seq):
    _, n_mod, mb, d = mod.shape
    return pl.BlockSpec((None, n_mod, mb, d), lambda i: (i // tiles_per_seq, 0, 0, 0))


def _ffn_in_call(x, mod, wgu, wdn, win, *, tm, tiles_per_seq, tf):
    n, d = x.shape
    pc = win.shape[1]
    return pl.pallas_call(
        functools.partial(_ffn_in_kernel, tf=tf),
        out_shape=(jax.ShapeDtypeStruct((n, d), _F32), jax.ShapeDtypeStruct((n, pc), _F32)),
        grid=(n // tm,),
        in_specs=[pl.BlockSpec((tm, d), lambda i: (i, 0)),
                  _mod_spec(mod, tiles_per_seq),
                  _resident(wgu.shape), _resident(wdn.shape), _resident(win.shape)],
        out_specs=(pl.BlockSpec((tm, d), lambda i: (i, 0)),
                   pl.BlockSpec((tm, pc), lambda i: (i, 0))),
        compiler_params=pltpu.CompilerParams(dimension_semantics=("arbitrary",),
                                             vmem_limit_bytes=_VMEM_LIMIT),
        name="ffn1_inproj",
    )(x, mod, wgu, wdn, win)


def _out_ffn_kernel(ym_ref, h_ref, mod_ref, wout_ref, wgu_ref, wdn_ref, fg_ref, o_ref, *, tf, final):
    g2, sh3, sc3, g3 = (mod_ref[j] for j in range(5, 9))
    h = h_ref[...] + _affine(jnp.dot(ym_ref[...].astype(_BF16), wout_ref[...],
                                     preferred_element_type=_F32), g2)
    n3 = _affine(_rms(h), 1.0 + sc3, sh3).astype(_BF16)
    h = h + _affine(_swiglu(n3, wgu_ref, wdn_ref, tf), 0.5 * g3)
    if final:
        h = _rms(h) * fg_ref[...]
    o_ref[...] = h


def _out_ffn_call(ym, h, mod, wout, wgu, wdn, fg, *, tm, tiles_per_seq, tf, final):
    n, d = h.shape
    return pl.pallas_call(
        functools.partial(_out_ffn_kernel, tf=tf, final=final),
        out_shape=jax.ShapeDtypeStruct((n, d), _F32),
        grid=(n // tm,),
        in_specs=[pl.BlockSpec((tm, ym.shape[1]), lambda i: (i, 0)),
                  pl.BlockSpec((tm, d), lambda i: (i, 0)),
                  _mod_spec(mod, tiles_per_seq),
                  _resident(wout.shape), _resident(wgu.shape), _resident(wdn.shape),
                  _resident(fg.shape)],
        out_specs=pl.BlockSpec((tm, d), lambda i: (i, 0)),
        compiler_params=pltpu.CompilerParams(dimension_semantics=("arbitrary",),
                                             vmem_limit_bytes=_VMEM_LIMIT),
        name="outproj_ffn2",
    )(ym, h, mod, wout, wgu, wdn, fg)


def _block_diag(y, mask_bf):
    reps = mask_bf.shape[0] // y.shape[0]
    return jnp.concatenate([y.astype(_BF16)] * reps, axis=0) * mask_bf


def _wkv_local(rt, at, bt, kt, vv, mask_bf, strict, incl, diag):
    n = len(rt)
    c = rt[0].shape[0]
    every = range(n)
    bd = lambda xs: [_block_diag(x, mask_bf) for x in xs]
    x = [jnp.concatenate([at[i], rt[i]], axis=0) for i in every]
    bb, kb = bd(bt), bd(kt)
    xb = [_dot_nt(x[i], bb[i]) for i in every]
    xk = [_dot_nt(x[i], kb[i]) for i in every]
    yield
    lab = [xb[i][:c] * strict for i in every]
    mrb = [xb[i][c:] * incl for i in every]
    lak = [xk[i][:c] * strict for i in every]
    mrk = [xk[i][c:] * incl for i in every]
    vb = bd(vv)
    wy = [_dot(jnp.concatenate([lak[i], mrk[i]], axis=0), vb[i]) for i in every]
    w1 = [wy[i][:c] for i in every]
    y0 = [wy[i][c:] for i in every]
    assert c == 4 * INV_BLOCK
    nd = [lab[i] * diag for i in every]
    noff = [lab[i] - nd[i] for i in every]
    tdm = nd
    ndb = bd(nd)
    npow = [_dot(nd[i], ndb[i]) for i in every]
    yield
    levels = INV_BLOCK.bit_length() - 2
    for lvl in range(levels):
        nb = bd(npow)
        if lvl + 1 < levels:
            res = [_dot(jnp.concatenate([tdm[i], npow[i]], axis=0), nb[i]) for i in every]
            tdm = [tdm[i] + npow[i] + res[i][:c] for i in every]
            npow = [res[i][c:] for i in every]
        else:
            tdm = [tdm[i] + npow[i] + _dot(tdm[i], nb[i]) for i in every]
        yield
    noffb = bd(noff)
    m1 = [noff[i] + _dot(tdm[i], noffb[i]) for i in every]
    yield
    m1b = bd(m1)
    m2 = [_dot(m1[i], m1b[i]) for i in every]
    yield
    m2b = bd(m2)
    q = [m1[i] + m2[i] + _dot(m1[i], m2b[i]) for i in every]
    yield
    tdmb = bd(tdm)
    tm = [q[i] + tdm[i] + _dot(q[i], tdmb[i]) for i in every]
    yield
    atb = bd(at)
    ap = [at[i] + _dot(tm[i], atb[i]) for i in every]
    w1b = bd(w1)
    u0 = [w1[i] + _dot(tm[i], w1b[i]) for i in every]
    yield
    return [(ap[i], u0[i], mrb[i], y0[i]) for i in every]


def _wkv_chain(loc, rt, vv, bdec, kdec, pc, st, mask, mask_bf):
    every = range(len(rt))
    c = rt[0].shape[0]
    ar = [_dot_nt(jnp.concatenate([loc[i][0], rt[i]], axis=0), st[i]) for i in every]
    yield
    u = [ar[i][:c] + loc[i][1] for i in every]
    upd = [_dot_tn(jnp.concatenate([u[i], vv[i]], axis=0), jnp.concatenate([bdec[i], kdec[i]], axis=0))
           for i in every]
    yield
    st_new = [st[i] * pc[i] + upd[i] * mask for i in every]
    ub = [_block_diag(u[i], mask_bf) for i in every]
    y = [ar[i][c:] + _dot(loc[i][2], ub[i]) + loc[i][3] for i in every]
    yield
    return y, st_new


def _run_staggered(gens, starts):
    out = [None] * len(gens)
    live = list(gens)
    rnd = 0
    while any(g is not None for g in live):
        for i, g in enumerate(live):
            if g is None or rnd < starts[i]:
                continue
            try:
                next(g)
            except StopIteration as stop:
                out[i] = stop.value
                live[i] = None
        rnd += 1
    return out


_HANDOFF = (("rt", _BF16), ("vv", _BF16), ("bdec", _BF16), ("kdec", _BF16), ("ap", _BF16), ("mrb", _BF16),
            ("u0", _F32), ("y0", _F32), ("gate", _F32), ("bonus", _F32), ("yc", _F32), ("pc", _F32))
SUBLANES = 8
TOKEN_STAGE_LEAD = 2


def _pc_rows(n_chunks):
    return -(-n_chunks // SUBLANES) * SUBLANES


def _token_stage(p, first, prm, masks, *, nb, tr, trp, ct, n_heads, head_dim, want_vn):
    (mu_ref, w0_ref, a0_ref, wwa_ref, wg_ref, kk_ref, ka_ref, rk_ref, lng_ref, lnb_ref, ws2_ref, bsf_ref) = prm
    mask_bf, strict, incl, diag, tri_bf, tril2 = masks
    c = WKV_CHUNK
    gw = HEAD_GROUP * head_dim
    n_groups = n_heads // HEAD_GROUP
    rw = n_heads * head_dim
    rows = nb * trp
    pad = trp - tr

    def segsum(x):
        return jnp.concatenate([_dot_sel_rhs(x[:, g * gw:(g + 1) * gw], mask_bf) for g in range(n_groups)],
                               axis=1)

    r_cols = mu_ref.shape[1]
    pr = p[:, :r_cols]
    trow = _mod_pow2(lax.broadcasted_iota(jnp.int32, (rows, 1), 0), trp)
    prev = jnp.where(trow == 0, first, pltpu.roll(pr, 1, 0))
    xm = pr + (prev - pr) * mu_ref[...]
    r = xm[:, 0:rw]
    k = xm[:, rw:2 * rw]
    v = xm[:, 2 * rw:3 * rw]
    lo_w = wwa_ref.shape[0]
    x_lo = xm[:, 3 * rw:3 * rw + lo_w]
    gd = xm[:, 3 * rw + lo_w:r_cols]
    lane = lax.broadcasted_iota(jnp.int32, (1, lo_w), 1)
    x_lo = jnp.where(lane < lo_w // 2, jnp.tanh(x_lo), x_lo)
    wa = _dot(x_lo, wwa_ref[...])
    gate = _dot(_sigmoid(gd), wg_ref[...])
    kkv = k * kk_ref[...]
    kss = segsum(kkv * kkv)
    yield
    wpre = w0_ref[...] + wa[:, :rw]
    lw = -DECAY_SCALE * _sigmoid(wpre)
    a = _sigmoid(a0_ref[...] + wa[:, rw:])
    kkn = kkv * lax.rsqrt(jnp.maximum(kss, 1e-24))
    k2 = k * (1.0 + (a - 1.0) * ka_ref[...])
    if pad:
        valid = trow < tr
        lw = jnp.where(valid, lw, 0.0)
        kkn = jnp.where(valid, kkn, 0.0)
        k2 = jnp.where(valid, k2, 0.0)
        v = jnp.where(valid, v, 0.0)
    sr = tri_bf.shape[0]
    cum = jnp.concatenate([_dot_sel_lhs(tri_bf, lw[i * sr:(i + 1) * sr]) for i in range(rows // sr)], axis=0)
    bonus = segsum(r * k2 * rk_ref[...]) * v
    yield
    tot = jnp.concatenate([jnp.broadcast_to(cum[j * c + c - 1:j * c + c], (c, rw)) for j in range(rows // c)],
                          axis=0)
    pinc = jnp.exp(cum)
    pinv = jnp.exp(-cum)
    pend = jnp.exp(tot - cum)
    beta = kkn * a
    rt = r * pinc
    at = -kkn * jnp.exp(cum - lw)
    bt = beta * pinv
    kt = k2 * pinv
    bdec = beta * pend
    kdec = k2 * pend

    n_chunks = rows // c
    probs = [(j, g) for j in range(n_chunks) for g in range(n_groups)]
    cut = lambda x: [x[j * c:(j + 1) * c, g * gw:(g + 1) * gw] for j, g in probs]
    local = yield from _wkv_local(cut(rt), cut(at), cut(bt), cut(kt), cut(v), mask_bf, strict, incl, diag)
    local = dict(zip(probs, local))

    def join(idx):
        return jnp.concatenate([jnp.concatenate([local[j, g][idx] for g in range(n_groups)], axis=1)
                                for j in range(n_chunks)], axis=0)

    gwid = lng_ref.shape[1]
    pu = p[:, r_cols:r_cols + gwid]
    pv = p[:, r_cols + gwid:r_cols + 2 * gwid]
    pm = jnp.mean(pv, -1, keepdims=True)
    pd = pv - pm
    pvar = jnp.mean(pd * pd, -1, keepdims=True)
    vn = pd * lax.rsqrt(pvar + LN_EPS) * lng_ref[...] + lnb_ref[...]
    n_pair = ws2_ref.shape[0]
    pw = gwid // n_pair
    first_half = lax.broadcasted_iota(jnp.int32, (1, pw), 1) < pw // 2
    ws_tril = [(ws2_ref[j] * tril2).astype(_BF16) for j in range(n_pair)]
    mixed_rows = []
    for q in range(rows // ct):
        outs = []
        for j in range(n_pair):
            vp = vn[q * ct:(q + 1) * ct, j * pw:(j + 1) * pw]
            rhs = jnp.concatenate([jnp.where(first_half, vp, 0.0), jnp.where(first_half, 0.0, vp)], axis=0)
            outs.append(_dot(ws_tril[j], rhs))
        mixed_rows.append(jnp.concatenate(outs, axis=1) + bsf_ref[...])
    mixed = mixed_rows[0] if len(mixed_rows) == 1 else jnp.concatenate(mixed_rows, axis=0)
    yield
    pc = [pinc[j * c + c - 1:j * c + c] for j in range(n_chunks)]
    pc += [jnp.zeros((_pc_rows(n_chunks) - n_chunks, rw), _F32)] * (n_chunks % SUBLANES != 0)
    vals = dict(rt=rt, vv=v, bdec=bdec, kdec=kdec, ap=join(0), u0=join(1), mrb=join(2), y0=join(3),
                gate=gate, bonus=bonus, yc=pu * mixed, pc=jnp.concatenate(pc, axis=0))
    vals = {name: vals[name].astype(dt) for name, dt in _HANDOFF}
    if want_vn:
        vals["vn"] = vn
    return vals


def _state_stage(vals, st_s, gn_refs, masks, *, nb, trp, parts, n_heads, head_dim):
    gnw_ref, gnb_ref = gn_refs
    mask, mask_bf = masks
    c = WKV_CHUNK
    gw = HEAD_GROUP * head_dim
    n_groups = n_heads // HEAD_GROUP
    rows = nb * trp
    n_chunks = rows // c
    chunks_per_seq = trp // c
    chunks_per_part = n_chunks // parts

    def pc_row(j):
        return (j // chunks_per_part) * _pc_rows(chunks_per_part) + j % chunks_per_part

    def segsum(x):
        return jnp.concatenate([_dot_sel_rhs(x[:, g * gw:(g + 1) * gw], mask_bf) for g in range(n_groups)],
                               axis=1)

    def blk(name, j, g):
        return vals[name][j * c:(j + 1) * c, g * gw:(g + 1) * gw]

    y_blk = {}
    for jj in range(chunks_per_seq):
        ps = [(s * chunks_per_seq + jj, g) for s in range(nb) for g in range(n_groups)]
        cut = lambda name: [blk(name, j, g) for j, g in ps]
        loc = list(zip(cut("ap"), cut("u0"), cut("mrb"), cut("y0")))
        sts = [st_s[j // chunks_per_seq, g] for j, g in ps]
        pcs = [vals["pc"][pc_row(j):pc_row(j) + 1, g * gw:(g + 1) * gw] for j, g in ps]
        ys, st_new = yield from _wkv_chain(loc, cut("rt"), cut("vv"), cut("bdec"), cut("kdec"), pcs, sts,
                                           mask, mask_bf)
        for (j, g), y, st in zip(ps, ys, st_new):
            st_s[j // chunks_per_seq, g] = st
            y_blk[j, g] = y
    y_rows = [jnp.concatenate([y_blk[j, g] for g in range(n_groups)], axis=1) for j in range(n_chunks)]
    yw = y_rows[0] if n_chunks == 1 else jnp.concatenate(y_rows, axis=0)

    inv_n = 1.0 / head_dim
    mean = segsum(yw) * inv_n
    yield
    dev = yw - mean
    var = segsum(dev * dev) * inv_n
    yield
    yn = dev * lax.rsqrt(var + GN_EPS) * gnw_ref[...] + gnb_ref[...]
    y_r = (yn + vals["bonus"]) * vals["gate"]
    return jnp.concatenate([y_r, vals["yc"]], axis=1)


def _mixer_kernel(*refs, nb, tr, trp, n_t, has_state, emit_vn, pipelined, ct, n_heads, head_dim):
    it = iter(refs)
    p_ref = next(it)
    shift0_ref = next(it) if has_state else None
    wkv0_ref = next(it) if has_state else None
    (mu_ref, w0_ref, a0_ref, wwa_ref, wg_ref, kk_ref, ka_ref, rk_ref, gnw_ref, gnb_ref,
     lng_ref, lnb_ref, ws2_ref, bsf_ref, hmask_ref, hmaskb_ref, tmask_ref, segm_ref, tril2_ref) = (
        next(it) for _ in range(19))
    y_ref = next(it)
    shift_ref = next(it)
    wkv_ref = next(it)
    vn_ref = next(it) if emit_vn else None
    st_s = next(it)
    carry_s = next(it)
    hand_s = {name: next(it) for name, _ in _HANDOFF} if pipelined else None

    n_groups = n_heads // HEAD_GROUP
    ti = pl.program_id(1)
    cfg = dict(nb=nb, trp=trp, n_heads=n_heads, head_dim=head_dim)
    prm = (mu_ref, w0_ref, a0_ref, wwa_ref, wg_ref, kk_ref, ka_ref, rk_ref, lng_ref, lnb_ref, ws2_ref, bsf_ref)

    r_cols = mu_ref.shape[1]
    parts = 2 if pipelined else 1
    half = trp // parts

    def token_stages():
        masks = (hmaskb_ref[...], tmask_ref[0], tmask_ref[1], tmask_ref[2], segm_ref[...], tril2_ref[...])
        if pipelined:
            firsts = [carry_s[0]] + [p_ref[0, h * half - 1:h * half, :r_cols] for h in range(1, parts)]
            carry_s[0] = p_ref[0, tr - 1:tr, :r_cols]
            sub = dict(cfg, trp=half)
            return [_token_stage(p_ref[0, h * half:(h + 1) * half], firsts[h], prm, masks, tr=half, ct=ct,
                                 want_vn=False, **sub) for h in range(parts)]
        pad = trp - tr
        rows_p = []
        for s in range(nb):
            rows_p.append(p_ref[s])
            if pad:
                rows_p.append(jnp.zeros((pad, p_ref.shape[2]), _F32))
        p = rows_p[0] if len(rows_p) == 1 else jnp.concatenate(rows_p, axis=0)
        if nb == 1:
            first = carry_s[0]
        else:
            first = jnp.concatenate([jnp.broadcast_to(carry_s[s], (trp, r_cols)) for s in range(nb)], axis=0)
        for s in range(nb):
            carry_s[s] = p_ref[s, tr - 1:tr, :r_cols]
        return [_token_stage(p, first, prm, masks, tr=tr, ct=ct, want_vn=emit_vn, **cfg)]

    def state_stage(vals):
        return _state_stage(vals, st_s, (gnw_ref, gnb_ref), (hmask_ref[...], hmaskb_ref[...]), parts=parts,
                            **cfg)

    def put_y(y_all):
        for s in range(nb):
            y_ref[s] = y_all[s * trp:s * trp + tr]

    def init_state():
        if has_state:
            carry_s[...] = shift0_ref[...]
            zero = jnp.zeros((head_dim, head_dim), _F32)
            for s in range(nb):
                for g in range(n_groups):
                    blocks = []
                    for h in range(HEAD_GROUP):
                        blk = wkv0_ref[s, g * HEAD_GROUP + h]
                        blocks.append(jnp.concatenate([zero] * h + [blk] + [zero] * (HEAD_GROUP - 1 - h),
                                                      axis=1))
                    st_s[s, g] = jnp.concatenate(blocks, axis=0)
        else:
            carry_s[...] = jnp.zeros_like(carry_s)
            st_s[...] = jnp.zeros_like(st_s)

    def put_state():
        shift_ref[...] = carry_s[...]
        for s in range(nb):
            for g in range(n_groups):
                st = st_s[s, g]
                for h in range(HEAD_GROUP):
                    wkv_ref[s, g * HEAD_GROUP + h] = st[h * head_dim:(h + 1) * head_dim,
                                                        h * head_dim:(h + 1) * head_dim]

    if not pipelined:
        pl.when(ti == 0)(init_state)
        vals, = _run_staggered(token_stages(), [0])
        y_all, = _run_staggered([state_stage(vals)], [0])
        put_y(y_all)
        if emit_vn:
            for s in range(nb):
                vn_ref[s] = vals["vn"][s * trp:s * trp + tr]
        pl.when(ti == n_t - 1)(put_state)
        return

    assert nb == 1 and trp == tr
    starts = [h * TOKEN_STAGE_LEAD for h in range(parts)]

    def save(vals_parts, slot):
        for h, vals in enumerate(vals_parts):
            for name, _ in _HANDOFF:
                n = vals[name].shape[0]
                hand_s[name][slot, h * n:(h + 1) * n] = vals[name]

    def load(slot):
        return {name: hand_s[name][slot] for name, _ in _HANDOFF}

    @pl.when(ti == 0)
    def _():
        init_state()
        save(_run_staggered(token_stages(), starts), 0)

    for parity in range(2):
        @pl.when((ti > 0) & (ti < n_t) & ((ti & 1) == parity))
        def _(parity=parity):
            *vals_parts, y_all = _run_staggered(token_stages() + [state_stage(load(1 - parity))], starts + [0])
            save(vals_parts, parity)
            put_y(y_all)

    @pl.when(ti == n_t)
    def _():
        y_all, = _run_staggered([state_stage(load((n_t - 1) % 2))], [0])
        put_y(y_all)
        put_state()


def _mixer_masks(rows, ct, head_dim):
    c, gw = WKV_CHUNK, HEAD_GROUP * head_dim
    blk = np.arange(gw) // head_dim
    hmask = (blk[:, None] == blk[None, :]).astype(np.float32)
    t = np.arange(c)[:, None]
    s = np.arange(HEAD_GROUP * c)[None, :] % c
    strict = s < t
    tmask = np.stack([strict, s <= t, strict & (s // INV_BLOCK == t // INV_BLOCK)]).astype(np.float32)
    r = np.arange(rows)
    same = (r[:, None] // c) == (r[None, :] // c)
    segm = (same & (r[None, :] <= r[:, None])).astype(np.float32)
    tril2 = ((np.arange(2 * ct)[None, :] % ct) <= np.arange(ct)[:, None]).astype(np.float32)
    return [jnp.asarray(hmask), jnp.asarray(hmask, _BF16), jnp.asarray(tmask), jnp.asarray(segm, _BF16),
            jnp.asarray(tril2)]


def _mixer_call(p, shift0, wkv0, prm, *, nb, tr, n_heads, head_dim, emit_vn):
    nseq, t, pcols = p.shape
    has_state = shift0 is not None
    trp = -(-tr // WKV_CHUNK) * WKV_CHUNK
    ct = min(trp, GMLP_CHUNK)
    rw = n_heads * head_dim
    r_cols = prm["mu"].shape[1]
    gwid = prm["lng"].shape[1]
    gw = HEAD_GROUP * head_dim
    n_groups = n_heads // HEAD_GROUP
    assert nseq % nb == 0 and t % tr == 0
    n_t = t // tr
    pipelined = False

    ws = prm["w_s"][:, :ct, :ct]
    hc = ws.shape[0]
    ws2 = ws.reshape(hc // 2, 2, ct, ct).transpose(0, 2, 1, 3).reshape(hc // 2, ct, 2 * ct)
    bsf = jnp.repeat(prm["b_s"][:, :ct].T, gwid // hc, axis=1)

    def full(a):
        nd = a.ndim
        return pl.BlockSpec(a.shape, lambda b, i: (0,) * nd)

    params = [prm["mu"], prm["w0"], prm["a0"], prm["wwa"], prm["wg"], prm["k_k"], prm["k_a"], prm["r_k"],
              prm["gn_w"], prm["gn_b"], prm["lng"], prm["lnb"], ws2, bsf]
    rows = nb * trp
    parts = 2 if pipelined else 1
    part_rows = rows // parts
    assert part_rows % ct == 0
    params += _mixer_masks(SEG_ROWS if part_rows % SEG_ROWS == 0 else part_rows, ct, head_dim)
    if pipelined:
        in_tile = lambda b, i: (b, jnp.minimum(i, n_t - 1), 0)
        out_tile = lambda b, i: (b, jnp.maximum(i - 1, 0), 0)
    else:
        in_tile = out_tile = lambda b, i: (b, i, 0)
    inputs = [p]
    in_specs = [pl.BlockSpec((nb, tr, pcols), in_tile)]
    if has_state:
        inputs += [shift0, wkv0]
        in_specs += [pl.BlockSpec((nb, 1, r_cols), lambda b, i: (b, 0, 0)),
                     pl.BlockSpec((nb, n_heads, head_dim, head_dim), lambda b, i: (b, 0, 0, 0))]
    inputs += params
    in_specs += [full(a) for a in params]

    out_shape = [jax.ShapeDtypeStruct((nseq, t, rw + gwid), _F32),
                 jax.ShapeDtypeStruct((nseq, 1, r_cols), _F32),
                 jax.ShapeDtypeStruct((nseq, n_heads, head_dim, head_dim), _F32)]
    out_specs = [pl.BlockSpec((nb, tr, rw + gwid), out_tile),
                 pl.BlockSpec((nb, 1, r_cols), lambda b, i: (b, 0, 0)),
                 pl.BlockSpec((nb, n_heads, head_dim, head_dim), lambda b, i: (b, 0, 0, 0))]
    if emit_vn:
        assert not pipelined
        out_shape.append(jax.ShapeDtypeStruct((nseq, t, gwid), _F32))
        out_specs.append(pl.BlockSpec((nb, tr, gwid), lambda b, i: (b, i, 0)))

    scratch = [pltpu.VMEM((nb, n_groups, gw, gw), _F32), pltpu.VMEM((nb, 1, r_cols), _F32)]
    if pipelined:
        widths = dict(yc=gwid)
        heights = dict(pc=parts * _pc_rows(part_rows // WKV_CHUNK))
        scratch += [pltpu.VMEM((2, heights.get(name, rows), widths.get(name, rw)), dt) for name, dt in _HANDOFF]

    return pl.pallas_call(
        functools.partial(_mixer_kernel, nb=nb, tr=tr, trp=trp, n_t=n_t, has_state=has_state, emit_vn=emit_vn,
                          pipelined=pipelined, ct=ct, n_heads=n_heads, head_dim=head_dim),
        out_shape=tuple(out_shape),
        grid=(nseq // nb, n_t + (1 if pipelined else 0)),
        in_specs=in_specs,
        out_specs=tuple(out_specs),
        scratch_shapes=scratch,
        compiler_params=pltpu.CompilerParams(dimension_semantics=("arbitrary", "arbitrary"),
                                             vmem_limit_bytes=_VMEM_LIMIT),
        name="mixers_state" if has_state else "mixers_fresh",
    )(*inputs)


def _pick_tile(n, pref):
    t = min(n, pref)
    while n % t:
        t //= 2
    return t


def kernel(x_prompt, x_sample, state_shift, state_wkv, c_prompt, c_sample, w_ada, b_ada, ffn1_gu, ffn1_dn, w_in, mu_shift, w0, w_lora_up, a0, a_lora_up, g_lora_up, k_k, k_a, r_k, gn_w, gn_b, ln_v_g, ln_v_b, w_s, b_s, w_out, ffn2_gu, ffn2_dn, final_g):
    depth = w_ada.shape[0]
    bp, tp, d = x_prompt.shape
    bs, ts, _ = x_sample.shape
    n_heads, head_dim = r_k.shape[1], r_k.shape[2]
    rw = n_heads * head_dim
    w_rank, a_rank = w_lora_up.shape[1], a_lora_up.shape[1]
    assert w_rank == a_rank and n_heads % HEAD_GROUP == 0 and head_dim == WKV_CHUNK
    d_ff = ffn1_dn.shape[1]
    tf = 256 if d_ff % 256 == 0 else 128

    xp = x_prompt.reshape(bp * tp, d)
    xs = x_sample.transpose(1, 0, 2).reshape(ts * bs, d)
    n_c = bp + bs
    n_c_pad = -(-n_c // 16) * 16
    c_all = jnp.concatenate([c_prompt, c_sample, jnp.zeros((n_c_pad - n_c, d), _F32)], axis=0)

    tm_p = _pick_tile(tp, 512)
    tr_p = _pick_tile(tp, 256)
    tm_s = ts * bs
    nb_p = _pick_tile(bp, 2)
    nb_s = _pick_tile(bs, 8)

    shp, wkp, shs, wks, cvs = [], [], [], [], []
    for l in range(depth):
        mod = _mod_call(c_all, w_ada[l], b_ada[l])
        mod_p = mod[:bp].reshape(bp, N_MOD, 1, d)
        mod_s = mod[bp:n_c].reshape(bs, N_MOD, d).transpose(1, 0, 2)[None]

        wgu1, wdn1 = ffn1_gu[l].astype(_BF16), ffn1_dn[l].astype(_BF16)
        wgu2, wdn2 = ffn2_gu[l].astype(_BF16), ffn2_dn[l].astype(_BF16)
        win, wout = w_in[l].astype(_BF16), w_out[l].astype(_BF16)
        zw = jnp.zeros((w_rank, rw), _F32)
        wwa = jnp.concatenate([jnp.concatenate([w_lora_up[l], zw], axis=1),
                               jnp.concatenate([zw, a_lora_up[l]], axis=1)], axis=0).astype(_BF16)
        row = lambda a: a.reshape(1, -1)
        prm = dict(mu=row(mu_shift[l]), w0=row(w0[l]), a0=row(a0[l]), wwa=wwa,
                   wg=g_lora_up[l].astype(_BF16), k_k=row(k_k[l]), k_a=row(k_a[l]), r_k=row(r_k[l]),
                   gn_w=row(gn_w[l]), gn_b=row(gn_b[l]), lng=row(ln_v_g[l]), lnb=row(ln_v_b[l]),
                   w_s=w_s[l], b_s=b_s[l])
        fg = final_g.reshape(1, d)
        final = l == depth - 1

        hp, pp = _ffn_in_call(xp, mod_p, wgu1, wdn1, win, tm=tm_p, tiles_per_seq=tp // tm_p, tf=tf)
        ymp, sh_p, wk_p = _mixer_call(pp.reshape(bp, tp, -1), None, None, prm, nb=nb_p, tr=tr_p,
                                      n_heads=n_heads, head_dim=head_dim, emit_vn=False)
        xp = _out_ffn_call(ymp.reshape(bp * tp, -1), hp, mod_p, wout, wgu2, wdn2, fg,
                           tm=tm_p, tiles_per_seq=tp // tm_p, tf=tf, final=final)

        hs, ps = _ffn_in_call(xs, mod_s, wgu1, wdn1, win, tm=tm_s, tiles_per_seq=1, tf=tf)
        ps_bm = ps.reshape(ts, bs, -1).transpose(1, 0, 2)
        yms, sh_s, wk_s, v_s = _mixer_call(ps_bm, state_shift[l][:, None, :], state_wkv[l], prm, nb=nb_s, tr=ts,
                                           n_heads=n_heads, head_dim=head_dim, emit_vn=True)
        yms_tm = yms.transpose(1, 0, 2).reshape(ts * bs, -1)
        xs = _out_ffn_call(yms_tm, hs, mod_s, wout, wgu2, wdn2, fg,
                           tm=tm_s, tiles_per_seq=1, tf=tf, final=final)

        shp.append(sh_p[:, 0]); wkp.append(wk_p); shs.append(sh_s[:, 0]); wks.append(wk_s); cvs.append(v_s)

    y_prompt = xp.reshape(bp, tp, d)
    y_sample = xs.reshape(ts, bs, d).transpose(1, 0, 2)
    return (y_prompt, y_sample, jnp.stack(shp), jnp.stack(wkp), jnp.stack(shs), jnp.stack(wks), jnp.stack(cvs))
```

```python
import functools
import math

import jax
import jax.numpy as jnp
import numpy as np
from jax import lax
from jax.experimental import pallas as pl
from jax.experimental.pallas import tpu as pltpu

_F32 = jnp.float32
_BF16 = jnp.bfloat16

RMS_EPS = 1e-6
LN_EPS = 1e-5
GN_EPS = 64e-5
N_MOD = 9
WKV_CHUNK = 64
HEAD_GROUP = 4
INV_BLOCK = 16
GMLP_CHUNK = 128
SEG_ROWS = 256
SUBLANES = 8
DECAY_SCALE = math.exp(-0.5)

_V7X_VMEM_BYTES = 64 * 1024 * 1024
_VMEM_LIMIT = _V7X_VMEM_BYTES - 8 * 1024 * 1024


def _dot(a, b):
    return jnp.dot(a.astype(_BF16), b.astype(_BF16), preferred_element_type=_F32)


def _dot_nt(a, b):
    return lax.dot_general(a.astype(_BF16), b.astype(_BF16), (((1,), (1,)), ((), ())),
                           preferred_element_type=_F32)


def _dot_tn(a, b):
    return lax.dot_general(a.astype(_BF16), b.astype(_BF16), (((0,), (0,)), ((), ())),
                           preferred_element_type=_F32)


def _split_hi_lo(x):
    hi = x.astype(_BF16)
    lo = (x - hi.astype(_F32)).astype(_BF16)
    return hi, lo


def _dot_sel_rhs(x, sel):
    hi, lo = _split_hi_lo(x)
    return (jnp.dot(hi, sel, preferred_element_type=_F32)
            + jnp.dot(lo, sel, preferred_element_type=_F32))


def _dot_sel_lhs(sel, x):
    hi, lo = _split_hi_lo(x)
    return (jnp.dot(sel, hi, preferred_element_type=_F32)
            + jnp.dot(sel, lo, preferred_element_type=_F32))


def _div_pow2(x, n):
    assert n & (n - 1) == 0
    return lax.shift_right_logical(x, jnp.int32(n.bit_length() - 1))


def _mod_pow2(x, n):
    assert n & (n - 1) == 0
    return lax.bitwise_and(x, jnp.int32(n - 1))


def _rms(x):
    return x * lax.rsqrt(jnp.mean(x * x, -1, keepdims=True) + RMS_EPS)


def _sigmoid(x):
    return 1.0 / (1.0 + jnp.exp(-x))


def _affine(x, scale, shift=None):
    mb = scale.shape[0]
    if mb == 1:
        y = x * scale
        return y if shift is None else y + shift
    tm, d = x.shape
    y = x.reshape(tm // mb, mb, d) * scale[None]
    if shift is not None:
        y = y + shift[None]
    return y.reshape(tm, d)


def _swiglu(n_bf, wgu_ref, wdn_ref, tf):
    d_ff = wdn_ref.shape[0]
    acc = None
    for j in range(d_ff // tf):
        g = jnp.dot(n_bf, wgu_ref[:, j * tf:(j + 1) * tf], preferred_element_type=_F32)
        u = jnp.dot(n_bf, wgu_ref[:, d_ff + j * tf:d_ff + (j + 1) * tf], preferred_element_type=_F32)
        hm = (g * _sigmoid(g) * u).astype(_BF16)
        part = jnp.dot(hm, wdn_ref[j * tf:(j + 1) * tf, :], preferred_element_type=_F32)
        acc = part if acc is None else acc + part
    return acc


def _mod_kernel(c_ref, w_ref, b_ref, o_ref):
    c = c_ref[...]
    s = c * _sigmoid(c)
    o_ref[...] = _dot(s, w_ref[...]) + b_ref[...]


def _mod_call(c_all, w_ada, b_ada):
    n, d = c_all.shape
    return pl.pallas_call(
        _mod_kernel,
        out_shape=jax.ShapeDtypeStruct((n, N_MOD * d), _F32),
        grid=(N_MOD,),
        in_specs=[pl.BlockSpec((n, d), lambda j: (0, 0)),
                  pl.BlockSpec((d, d), lambda j: (0, j)),
                  pl.BlockSpec((1, d), lambda j: (0, j))],
        out_specs=pl.BlockSpec((n, d), lambda j: (0, j)),
        compiler_params=pltpu.CompilerParams(dimension_semantics=("arbitrary",)),
        name="adaln_mod",
    )(c_all, w_ada, b_ada.reshape(1, N_MOD * d))


def _ffn_in_kernel(x_ref, mod_ref, wgu_ref, wdn_ref, win_ref, h_ref, p_ref, *, tf):
    x = x_ref[...]
    sh1, sc1, g1, sh2, sc2 = (mod_ref[j] for j in range(5))
    n1 = _affine(_rms(x), 1.0 + sc1, sh1).astype(_BF16)
    h = x + _affine(_swiglu(n1, wgu_ref, wdn_ref, tf), 0.5 * g1)
    h_ref[...] = h
    n2 = _affine(_rms(h), 1.0 + sc2, sh2).astype(_BF16)
    p_ref[...] = jnp.dot(n2, win_ref[...], preferred_element_type=_F32)


def _resident(shape):
    nd = len(shape)
    return pl.BlockSpec(shape, lambda *_: (0,) * nd, pipeline_mode=pl.Buffered(1))


def _mod_spec(mod, tiles_per_seq):
    _, n_mod, mb, d = mod.shape
    return pl.BlockSpec((None, n_mod, mb, d), lambda i: (i // tiles_per_seq, 0, 0, 0))


def _ffn_in_call(x, mod, wgu, wdn, win, *, tm, tiles_per_seq, tf):
    n, d = x.shape
    pc = win.shape[1]
    return pl.pallas_call(
        functools.partial(_ffn_in_kernel, tf=tf),
        out_shape=(jax.ShapeDtypeStruct((n, d), _F32), jax.ShapeDtypeStruct((n, pc), _F32)),
        grid=(n // tm,),
        in_specs=[pl.BlockSpec((tm, d), lambda i: (i, 0)),
                  _mod_spec(mod, tiles_per_seq),
                  _resident(wgu.shape), _resident(wdn.shape), _resident(win.shape)],
        out_specs=(pl.BlockSpec((tm, d), lambda i: (i, 0)),
                   pl.BlockSpec((tm, pc), lambda i: (i, 0))),
        compiler_params=pltpu.CompilerParams(dimension_semantics=("arbitrary",),
                                             vmem_limit_bytes=_VMEM_LIMIT),
        name="ffn1_inproj",
    )(x, mod, wgu, wdn, win)


def _out_ffn_kernel(ym_ref, h_ref, mod_ref, wout_ref, wgu_ref, wdn_ref, fg_ref, o_ref, *, tf, final):
    g2, sh3, sc3, g3 = (mod_ref[j] for j in range(5, 9))
    h = h_ref[...] + _affine(jnp.dot(ym_ref[...].astype(_BF16), wout_ref[...],
                                     preferred_element_type=_F32), g2)
    n3 = _affine(_rms(h), 1.0 + sc3, sh3).astype(_BF16)
    h = h + _affine(_swiglu(n3, wgu_ref, wdn_ref, tf), 0.5 * g3)
    if final:
        h = _rms(h) * fg_ref[...]
    o_ref[...] = h


def _out_ffn_call(ym, h, mod, wout, wgu, wdn, fg, *, tm, tiles_per_seq, tf, final):
    n, d = h.shape
    return pl.pallas_call(
        functools.partial(_out_ffn_kernel, tf=tf, final=final),
        out_shape=jax.ShapeDtypeStruct((n, d), _F32),
        grid=(n // tm,),
        in_specs=[pl.BlockSpec((tm, ym.shape[1]), lambda i: (i, 0)),
                  pl.BlockSpec((tm, d), lambda i: (i, 0)),
                  _mod_spec(mod, tiles_per_seq),
                  _resident(wout.shape), _resident(wgu.shape), _resident(wdn.shape),
                  _resident(fg.shape)],
        out_specs=pl.BlockSpec((tm, d), lambda i: (i, 0)),
        compiler_params=pltpu.CompilerParams(dimension_semantics=("arbitrary",),
                                             vmem_limit_bytes=_VMEM_LIMIT),
        name="outproj_ffn2",
    )(ym, h, mod, wout, wgu, wdn, fg)


def _block_diag(y, mask_bf):
    reps = mask_bf.shape[0] // y.shape[0]
    return jnp.concatenate([y.astype(_BF16)] * reps, axis=0) * mask_bf


def _nilpotent_inverse(n, bd, order):
    every = range(len(n))
    c = n[0].shape[0]
    tm = n
    levels = order.bit_length() - 2
    if levels <= 0:
        return tm
    nb = bd(n)
    npow = [_dot(n[i], nb[i]) for i in every]
    for lvl in range(levels):
        nb = bd(npow)
        if lvl + 1 < levels:
            res = [_dot(jnp.concatenate([tm[i], npow[i]], axis=0), nb[i]) for i in every]
            tm = [tm[i] + npow[i] + res[i][:c] for i in every]
            npow = [res[i][c:] for i in every]
        else:
            tm = [tm[i] + npow[i] + _dot(tm[i], nb[i]) for i in every]
    return tm


def _wkv_local(rt, at, bt, kt, vv, mask_bf, strict, incl, diag):
    n = len(rt)
    c = rt[0].shape[0]
    every = range(n)
    bd = lambda xs: [_block_diag(x, mask_bf) for x in xs]
    x = [jnp.concatenate([at[i], rt[i]], axis=0) for i in every]
    bb, kb = bd(bt), bd(kt)
    xb = [_dot_nt(x[i], bb[i]) for i in every]
    xk = [_dot_nt(x[i], kb[i]) for i in every]
    lab = [xb[i][:c] * strict for i in every]
    mrb = [xb[i][c:] * incl for i in every]
    lak = [xk[i][:c] * strict for i in every]
    mrk = [xk[i][c:] * incl for i in every]
    vb = bd(vv)
    wy = [_dot(jnp.concatenate([lak[i], mrk[i]], axis=0), vb[i]) for i in every]
    w1 = [wy[i][:c] for i in every]
    y0 = [wy[i][c:] for i in every]
    if isinstance(diag, int):
        tm = _nilpotent_inverse(lab, bd, diag)
    else:
        assert c == 4 * INV_BLOCK
        nd = [lab[i] * diag for i in every]
        noff = [lab[i] - nd[i] for i in every]
        tdm = _nilpotent_inverse(nd, bd, INV_BLOCK)
        noffb = bd(noff)
        m1 = [noff[i] + _dot(tdm[i], noffb[i]) for i in every]
        m1b = bd(m1)
        m2 = [_dot(m1[i], m1b[i]) for i in every]
        m2b = bd(m2)
        q = [m1[i] + m2[i] + _dot(m1[i], m2b[i]) for i in every]
        tdmb = bd(tdm)
        tm = [q[i] + tdm[i] + _dot(q[i], tdmb[i]) for i in every]
    atb = bd(at)
    ap = [at[i] + _dot(tm[i], atb[i]) for i in every]
    w1b = bd(w1)
    u0 = [w1[i] + _dot(tm[i], w1b[i]) for i in every]
    return [(ap[i], u0[i], mrb[i], y0[i]) for i in every]


def _wkv_state_step(ap, rt, u0, vv, bdec, kdec, pc, st, mask):
    every = range(len(rt))
    c = rt[0].shape[0]
    ar = [_dot_nt(jnp.concatenate([ap[i], rt[i]], axis=0), st[i]) for i in every]
    u = [ar[i][:c] + u0[i] for i in every]
    upd = [_dot_tn(jnp.concatenate([u[i], vv[i]], axis=0), jnp.concatenate([bdec[i], kdec[i]], axis=0))
           for i in every]
    st_new = [st[i] * pc[i] + upd[i] * mask for i in every]
    return u, [ar[i][c:] for i in every], st_new


def _token_features(p, first, trow, prm, segsum, rw):
    mu_ref, w0_ref, a0_ref, wwa_ref, wg_ref, kk_ref, ka_ref = prm
    r_cols = mu_ref.shape[1]
    pr = p[:, :r_cols]
    prev = jnp.where(trow == 0, first, pltpu.roll(pr, 1, 0))
    xm = pr + (prev - pr) * mu_ref[...]
    r = xm[:, 0:rw]
    k = xm[:, rw:2 * rw]
    v = xm[:, 2 * rw:3 * rw]
    lo_w = wwa_ref.shape[0]
    x_lo = xm[:, 3 * rw:3 * rw + lo_w]
    gd = xm[:, 3 * rw + lo_w:r_cols]
    lane = lax.broadcasted_iota(jnp.int32, (1, lo_w), 1)
    x_lo = jnp.where(lane < lo_w // 2, jnp.tanh(x_lo), x_lo)
    wa = _dot(x_lo, wwa_ref[...])
    wpre = w0_ref[...] + wa[:, :rw]
    lw = -DECAY_SCALE * _sigmoid(wpre)
    a = _sigmoid(a0_ref[...] + wa[:, rw:])
    gate = _dot(_sigmoid(gd), wg_ref[...])
    kkv = k * kk_ref[...]
    kkn = kkv * lax.rsqrt(jnp.maximum(segsum(kkv * kkv), 1e-24))
    k2 = k * (1.0 + (a - 1.0) * ka_ref[...])
    return dict(r=r, v=v, lw=lw, a=a, gate=gate, kkn=kkn, k2=k2)


def _decay_factors(f, cum, tot):
    pinc = jnp.exp(cum)
    pinv = jnp.exp(-cum)
    pend = jnp.exp(tot - cum)
    beta = f["kkn"] * f["a"]
    return dict(rt=f["r"] * pinc, at=-f["kkn"] * jnp.exp(cum - f["lw"]), bt=beta * pinv, kt=f["k2"] * pinv,
                bdec=beta * pend, kdec=f["k2"] * pend, pinc=pinc)


def _rwkv_output(yw, f, rk_ref, gnw_ref, gnb_ref, segsum, head_dim):
    inv_n = 1.0 / head_dim
    mean = segsum(yw) * inv_n
    dev = yw - mean
    var = segsum(dev * dev) * inv_n
    yn = dev * lax.rsqrt(var + GN_EPS) * gnw_ref[...] + gnb_ref[...]
    bonus = segsum(f["r"] * f["k2"] * rk_ref[...]) * f["v"]
    return (yn + bonus) * f["gate"]


def _gmlp_branch(p, r_cols, lng_ref, lnb_ref, ws2_ref, tril2, bsf_ref):
    gwid = lng_ref.shape[1]
    ct = ws2_ref.shape[1]
    rows = p.shape[0]
    pu = p[:, r_cols:r_cols + gwid]
    pv = p[:, r_cols + gwid:r_cols + 2 * gwid]
    pm = jnp.mean(pv, -1, keepdims=True)
    pd = pv - pm
    pvar = jnp.mean(pd * pd, -1, keepdims=True)
    vn = pd * lax.rsqrt(pvar + LN_EPS) * lng_ref[...] + lnb_ref[...]
    n_pair = ws2_ref.shape[0]
    pw = gwid // n_pair
    first_half = lax.broadcasted_iota(jnp.int32, (1, pw), 1) < pw // 2
    ws_tril = [(ws2_ref[j] * tril2).astype(_BF16) for j in range(n_pair)]
    mixed_rows = []
    for q in range(rows // ct):
        outs = []
        for j in range(n_pair):
            vp = vn[q * ct:(q + 1) * ct, j * pw:(j + 1) * pw]
            rhs = jnp.concatenate([jnp.where(first_half, vp, 0.0), jnp.where(first_half, 0.0, vp)], axis=0)
            outs.append(_dot(ws_tril[j], rhs))
        mixed_rows.append(jnp.concatenate(outs, axis=1) + bsf_ref[...])
    mixed = mixed_rows[0] if len(mixed_rows) == 1 else jnp.concatenate(mixed_rows, axis=0)
    return pu * mixed, vn


def _state_from_heads(wkv_ref, s, g, head_dim):
    zero = jnp.zeros((head_dim, head_dim), _F32)
    blocks = []
    for h in range(HEAD_GROUP):
        blk = wkv_ref[s, g * HEAD_GROUP + h]
        blocks.append(jnp.concatenate([zero] * h + [blk] + [zero] * (HEAD_GROUP - 1 - h), axis=1))
    return jnp.concatenate(blocks, axis=0)


def _state_to_heads(wkv_ref, s, g, st, head_dim):
    for h in range(HEAD_GROUP):
        wkv_ref[s, g * HEAD_GROUP + h] = st[h * head_dim:(h + 1) * head_dim, h * head_dim:(h + 1) * head_dim]


def _mixer_kernel(p_ref, mu_ref, w0_ref, a0_ref, wwa_ref, wg_ref, kk_ref, ka_ref, rk_ref, gnw_ref, gnb_ref,
                  lng_ref, lnb_ref, ws2_ref, bsf_ref, hmask_ref, hmaskb_ref, tmask_ref, segm_ref, tril2_ref,
                  y_ref, shift_ref, wkv_ref, st_s, carry_s, *, n_heads, head_dim):
    nb, tr, _ = p_ref.shape
    c = WKV_CHUNK
    gw = HEAD_GROUP * head_dim
    n_groups = n_heads // HEAD_GROUP
    rw = n_heads * head_dim
    rows = nb * tr
    ti = pl.program_id(1)
    n_t = pl.num_programs(1)
    r_cols = mu_ref.shape[1]

    @pl.when(ti == 0)
    def _():
        carry_s[...] = jnp.zeros_like(carry_s)
        st_s[...] = jnp.zeros_like(st_s)

    mask = hmask_ref[...]
    mask_bf = hmaskb_ref[...]
    strict, incl, diag = tmask_ref[0], tmask_ref[1], tmask_ref[2]
    tri_bf = segm_ref[...]

    def segsum(x):
        return jnp.concatenate([_dot_sel_rhs(x[:, g * gw:(g + 1) * gw], mask_bf) for g in range(n_groups)],
                               axis=1)

    p = p_ref[0] if nb == 1 else jnp.concatenate([p_ref[s] for s in range(nb)], axis=0)
    trow = _mod_pow2(lax.broadcasted_iota(jnp.int32, (rows, 1), 0), tr)
    if nb == 1:
        first = carry_s[0]
    else:
        first = jnp.concatenate([jnp.broadcast_to(carry_s[s], (tr, r_cols)) for s in range(nb)], axis=0)
    for s in range(nb):
        carry_s[s] = p_ref[s, tr - 1:tr, :r_cols]
    f = _token_features(p, first, trow, (mu_ref, w0_ref, a0_ref, wwa_ref, wg_ref, kk_ref, ka_ref), segsum, rw)
    sr = tri_bf.shape[0]
    cum = jnp.concatenate([_dot_sel_lhs(tri_bf, f["lw"][i * sr:(i + 1) * sr]) for i in range(rows // sr)],
                          axis=0)
    tot = jnp.concatenate([jnp.broadcast_to(cum[j * c + c - 1:j * c + c], (c, rw)) for j in range(rows // c)],
                          axis=0)
    d = _decay_factors(f, cum, tot)

    n_chunks = rows // c
    chunks_per_seq = tr // c

    def blk(x, j, g):
        return x[j * c:(j + 1) * c, g * gw:(g + 1) * gw]

    probs = [(j, g) for j in range(n_chunks) for g in range(n_groups)]
    cut = lambda x, ps: [blk(x, j, g) for j, g in ps]
    local = dict(zip(probs, _wkv_local(cut(d["rt"], probs), cut(d["at"], probs), cut(d["bt"], probs),
                                       cut(d["kt"], probs), cut(f["v"], probs), mask_bf, strict, incl, diag)))
    y_blk = {}
    for jj in range(chunks_per_seq):
        ps = [(s * chunks_per_seq + jj, g) for s in range(nb) for g in range(n_groups)]
        loc = [local[pr_] for pr_ in ps]
        sts = [st_s[j // chunks_per_seq, g] for j, g in ps]
        pcs = [blk(d["pinc"], j, g)[c - 1:c] for j, g in ps]
        u, ys, st_new = _wkv_state_step([l[0] for l in loc], cut(d["rt"], ps), [l[1] for l in loc],
                                        cut(f["v"], ps), cut(d["bdec"], ps), cut(d["kdec"], ps), pcs, sts, mask)
        for i, (j, g) in enumerate(ps):
            st_s[j // chunks_per_seq, g] = st_new[i]
            y_blk[j, g] = ys[i] + _dot(loc[i][2], _block_diag(u[i], mask_bf)) + loc[i][3]
    y_rows = [jnp.concatenate([y_blk[j, g] for g in range(n_groups)], axis=1) for j in range(n_chunks)]
    yw = y_rows[0] if n_chunks == 1 else jnp.concatenate(y_rows, axis=0)

    y_r = _rwkv_output(yw, f, rk_ref, gnw_ref, gnb_ref, segsum, head_dim)
    y_c, _ = _gmlp_branch(p, r_cols, lng_ref, lnb_ref, ws2_ref, tril2_ref[...], bsf_ref)
    y_all = jnp.concatenate([y_r, y_c], axis=1)
    for s in range(nb):
        y_ref[s] = y_all[s * tr:(s + 1) * tr]

    @pl.when(ti == n_t - 1)
    def _():
        shift_ref[...] = carry_s[...]
        for s in range(nb):
            for g in range(n_groups):
                _state_to_heads(wkv_ref, s, g, st_s[s, g], head_dim)


def _mixer_short_kernel(p_ref, shift0_ref, wkv0_ref, mu_ref, w0_ref, a0_ref, wwa_ref, wg_ref, kk_ref, ka_ref,
                        rk_ref, gnw_ref, gnb_ref, lng_ref, lnb_ref, ws2_ref, bsf_ref, hmask_ref, hmaskb_ref,
                        tmask_ref, segm_ref, tril2_ref, y_ref, shift_ref, wkv_ref, vn_ref, *, n_heads, head_dim):
    nb, ts, _ = p_ref.shape
    c = WKV_CHUNK
    assert nb * ts == c
    gw = HEAD_GROUP * head_dim
    n_groups = n_heads // HEAD_GROUP
    rw = n_heads * head_dim
    r_cols = mu_ref.shape[1]

    mask = hmask_ref[...]
    mask_bf = hmaskb_ref[...]
    strict, incl = tmask_ref[0], tmask_ref[1]

    def segsum(x):
        return jnp.concatenate([_dot_sel_rhs(x[:, g * gw:(g + 1) * gw], mask_bf) for g in range(n_groups)],
                               axis=1)

    p = p_ref[...].reshape(c, p_ref.shape[2])
    trow = _mod_pow2(lax.broadcasted_iota(jnp.int32, (c, 1), 0), ts)
    first = jnp.concatenate([jnp.broadcast_to(shift0_ref[s], (ts, r_cols)) for s in range(nb)], axis=0)
    for s in range(nb):
        shift_ref[s] = p_ref[s, ts - 1:ts, :r_cols]
    f = _token_features(p, first, trow, (mu_ref, w0_ref, a0_ref, wwa_ref, wg_ref, kk_ref, ka_ref), segsum, rw)
    cum = _dot_sel_lhs(segm_ref[...], f["lw"])
    tot = jnp.concatenate([jnp.broadcast_to(cum[s * ts + ts - 1:(s + 1) * ts], (ts, rw)) for s in range(nb)],
                          axis=0)
    d = _decay_factors(f, cum, tot)

    groups = range(n_groups)
    gcut = lambda x: [x[:, g * gw:(g + 1) * gw] for g in groups]
    local = _wkv_local(gcut(d["rt"]), gcut(d["at"]), gcut(d["bt"]), gcut(d["kt"]), gcut(f["v"]), mask_bf,
                       strict, incl, ts)

    ps = [(s, g) for s in range(nb) for g in groups]
    scut = lambda x: [x[s * ts:(s + 1) * ts, g * gw:(g + 1) * gw] for s, g in ps]
    lcut = lambda idx: [local[g][idx][s * ts:(s + 1) * ts] for s, g in ps]
    sts = [_state_from_heads(wkv0_ref, s, g, head_dim) for s, g in ps]
    pcs = [d["pinc"][s * ts + ts - 1:(s + 1) * ts, g * gw:(g + 1) * gw] for s, g in ps]
    u, ys, st_new = _wkv_state_step(lcut(0), scut(d["rt"]), lcut(1), scut(f["v"]), scut(d["bdec"]),
                                    scut(d["kdec"]), pcs, sts, mask)
    for i, (s, g) in enumerate(ps):
        _state_to_heads(wkv_ref, s, g, st_new[i], head_dim)
    rows_of = lambda xs, g: jnp.concatenate([xs[i] for i, (s, gg) in enumerate(ps) if gg == g], axis=0)
    yw = jnp.concatenate([rows_of(ys, g) + _dot(local[g][2], _block_diag(rows_of(u, g), mask_bf)) + local[g][3]
                          for g in groups], axis=1)

    y_r = _rwkv_output(yw, f, rk_ref, gnw_ref, gnb_ref, segsum, head_dim)
    y_c, vn = _gmlp_branch(p, r_cols, lng_ref, lnb_ref, ws2_ref, tril2_ref[...], bsf_ref)
    y_ref[...] = jnp.concatenate([y_r, y_c], axis=1).reshape(nb, ts, -1)
    vn_ref[...] = vn.reshape(nb, ts, -1)


def _mixer_masks(rows, seq, ct, head_dim):
    c, gw = WKV_CHUNK, HEAD_GROUP * head_dim
    blk = np.arange(gw) // head_dim
    hmask = (blk[:, None] == blk[None, :]).astype(np.float32)
    t = np.arange(c)[:, None]
    s = np.arange(HEAD_GROUP * c)[None, :] % c
    strict = (s < t) & (s // seq == t // seq)
    incl = (s <= t) & (s // seq == t // seq)
    tmask = np.stack([strict, incl, strict & (s // INV_BLOCK == t // INV_BLOCK)]).astype(np.float32)
    r = np.arange(rows)
    seg = min(c, seq)
    same = (r[:, None] // seg) == (r[None, :] // seg)
    segm = (same & (r[None, :] <= r[:, None])).astype(np.float32)
    ctm = np.arange(ct)
    tril1 = ((ctm[None, :] <= ctm[:, None]) & (ctm[None, :] // seq == ctm[:, None] // seq))
    tril2 = np.concatenate([tril1, tril1], axis=1).astype(np.float32)
    return [jnp.asarray(hmask), jnp.asarray(hmask, _BF16), jnp.asarray(tmask), jnp.asarray(segm, _BF16),
            jnp.asarray(tril2)]


def _gmlp_weights(w_s, b_s, ct, seq, gwid):
    hc = w_s.shape[0]
    n_seq = ct // min(seq, ct)
    ws = w_s[:, :ct // n_seq, :ct // n_seq]
    bs = b_s[:, :ct // n_seq]
    if n_seq > 1:
        ws = jnp.einsum("st,hij->hsitj", jnp.eye(n_seq, dtype=ws.dtype), ws).reshape(hc, ct, ct)
        bs = jnp.tile(bs, (1, n_seq))
    ws2 = ws.reshape(hc // 2, 2, ct, ct).transpose(0, 2, 1, 3).reshape(hc // 2, ct, 2 * ct)
    return ws2, jnp.repeat(bs.T, gwid // hc, axis=1)


def _mixer_params(prm, rows, seq, ct, head_dim):
    gwid = prm["lng"].shape[1]
    ws2, bsf = _gmlp_weights(prm["w_s"], prm["b_s"], ct, seq, gwid)
    params = [prm["mu"], prm["w0"], prm["a0"], prm["wwa"], prm["wg"], prm["k_k"], prm["k_a"], prm["r_k"],
              prm["gn_w"], prm["gn_b"], prm["lng"], prm["lnb"], ws2, bsf]
    return params + _mixer_masks(rows, seq, ct, head_dim)


def _full_spec(a):
    nd = a.ndim
    return pl.BlockSpec(a.shape, lambda b, i: (0,) * nd)


def _mixer_call(p, prm, *, nb, tr, n_heads, head_dim):
    nseq, t, pcols = p.shape
    rw = n_heads * head_dim
    r_cols = prm["mu"].shape[1]
    gwid = prm["lng"].shape[1]
    gw = HEAD_GROUP * head_dim
    n_groups = n_heads // HEAD_GROUP
    rows = nb * tr
    assert nseq % nb == 0 and t % tr == 0 and tr % GMLP_CHUNK == 0 and rows % SEG_ROWS == 0
    params = _mixer_params(prm, SEG_ROWS, t, GMLP_CHUNK, head_dim)
    return pl.pallas_call(
        functools.partial(_mixer_kernel, n_heads=n_heads, head_dim=head_dim),
        out_shape=(jax.ShapeDtypeStruct((nseq, t, rw + gwid), _F32),
                   jax.ShapeDtypeStruct((nseq, 1, r_cols), _F32),
                   jax.ShapeDtypeStruct((nseq, n_heads, head_dim, head_dim), _F32)),
        grid=(nseq // nb, t // tr),
        in_specs=[pl.BlockSpec((nb, tr, pcols), lambda b, i: (b, i, 0))] + [_full_spec(a) for a in params],
        out_specs=(pl.BlockSpec((nb, tr, rw + gwid), lambda b, i: (b, i, 0)),
                   pl.BlockSpec((nb, 1, r_cols), lambda b, i: (b, 0, 0)),
                   pl.BlockSpec((nb, n_heads, head_dim, head_dim), lambda b, i: (b, 0, 0, 0))),
        scratch_shapes=[pltpu.VMEM((nb, n_groups, gw, gw), _F32), pltpu.VMEM((nb, 1, r_cols), _F32)],
        compiler_params=pltpu.CompilerParams(dimension_semantics=("arbitrary", "arbitrary"),
                                             vmem_limit_bytes=_VMEM_LIMIT),
        name="mixers_fresh",
    )(p, *params)


def _mixer_short_call(p, shift0, wkv0, prm, *, n_heads, head_dim):
    nseq, ts, pcols = p.shape
    nb = WKV_CHUNK // ts
    assert nb * ts == WKV_CHUNK and ts % SUBLANES == 0 and nseq % nb == 0
    rw = n_heads * head_dim
    r_cols = prm["mu"].shape[1]
    gwid = prm["lng"].shape[1]
    params = _mixer_params(prm, WKV_CHUNK, ts, WKV_CHUNK, head_dim)
    seqs = lambda *tail: pl.BlockSpec((nb,) + tail, lambda b, i: (b,) + (0,) * len(tail))
    return pl.pallas_call(
        functools.partial(_mixer_short_kernel, n_heads=n_heads, head_dim=head_dim),
        out_shape=(jax.ShapeDtypeStruct((nseq, ts, rw + gwid), _F32),
                   jax.ShapeDtypeStruct((nseq, 1, r_cols), _F32),
                   jax.ShapeDtypeStruct((nseq, n_heads, head_dim, head_dim), _F32),
                   jax.ShapeDtypeStruct((nseq, ts, gwid), _F32)),
        grid=(nseq // nb, 1),
        in_specs=[seqs(ts, pcols), seqs(1, r_cols), seqs(n_heads, head_dim, head_dim)]
                 + [_full_spec(a) for a in params],
        out_specs=(seqs(ts, rw + gwid), seqs(1, r_cols), seqs(n_heads, head_dim, head_dim), seqs(ts, gwid)),
        compiler_params=pltpu.CompilerParams(dimension_semantics=("arbitrary", "arbitrary"),
                                             vmem_limit_bytes=_VMEM_LIMIT),
        name="mixers_state",
    )(p, shift0, wkv0, *params)


def _pick_tile(n, pref):
    t = min(n, pref)
    while n % t:
        t //= 2
    return t


def kernel(x_prompt, x_sample, state_shift, state_wkv, c_prompt, c_sample, w_ada, b_ada, ffn1_gu, ffn1_dn, w_in, mu_shift, w0, w_lora_up, a0, a_lora_up, g_lora_up, k_k, k_a, r_k, gn_w, gn_b, ln_v_g, ln_v_b, w_s, b_s, w_out, ffn2_gu, ffn2_dn, final_g):
    depth = w_ada.shape[0]
    bp, tp, d = x_prompt.shape
    bs, ts, _ = x_sample.shape
    n_heads, head_dim = r_k.shape[1], r_k.shape[2]
    rw = n_heads * head_dim
    w_rank, a_rank = w_lora_up.shape[1], a_lora_up.shape[1]
    assert w_rank == a_rank and n_heads % HEAD_GROUP == 0 and head_dim == WKV_CHUNK
    d_ff = ffn1_dn.shape[1]
    tf = 256 if d_ff % 256 == 0 else 128

    xp = x_prompt.reshape(bp * tp, d)
    xs = x_sample.transpose(1, 0, 2).reshape(ts * bs, d)
    n_c = bp + bs
    n_c_pad = -(-n_c // 16) * 16
    c_all = jnp.concatenate([c_prompt, c_sample, jnp.zeros((n_c_pad - n_c, d), _F32)], axis=0)

    tm_p = _pick_tile(tp, 512)
    tr_p = _pick_tile(tp, 256)
    tm_s = ts * bs
    nb_p = _pick_tile(bp, 2)

    shp, wkp, shs, wks, cvs = [], [], [], [], []
    for l in range(depth):
        mod = _mod_call(c_all, w_ada[l], b_ada[l])
        mod_p = mod[:bp].reshape(bp, N_MOD, 1, d)
        mod_s = mod[bp:n_c].reshape(bs, N_MOD, d).transpose(1, 0, 2)[None]

        wgu1, wdn1 = ffn1_gu[l].astype(_BF16), ffn1_dn[l].astype(_BF16)
        wgu2, wdn2 = ffn2_gu[l].astype(_BF16), ffn2_dn[l].astype(_BF16)
        win, wout = w_in[l].astype(_BF16), w_out[l].astype(_BF16)
        zw = jnp.zeros((w_rank, rw), _F32)
        wwa = jnp.concatenate([jnp.concatenate([w_lora_up[l], zw], axis=1),
                               jnp.concatenate([zw, a_lora_up[l]], axis=1)], axis=0).astype(_BF16)
        row = lambda a: a.reshape(1, -1)
        prm = dict(mu=row(mu_shift[l]), w0=row(w0[l]), a0=row(a0[l]), wwa=wwa,
                   wg=g_lora_up[l].astype(_BF16), k_k=row(k_k[l]), k_a=row(k_a[l]), r_k=row(r_k[l]),
                   gn_w=row(gn_w[l]), gn_b=row(gn_b[l]), lng=row(ln_v_g[l]), lnb=row(ln_v_b[l]),
                   w_s=w_s[l], b_s=b_s[l])
        fg = final_g.reshape(1, d)
        final = l == depth - 1

        hp, pp = _ffn_in_call(xp, mod_p, wgu1, wdn1, win, tm=tm_p, tiles_per_seq=tp // tm_p, tf=tf)
        ymp, sh_p, wk_p = _mixer_call(pp.reshape(bp, tp, -1), prm, nb=nb_p, tr=tr_p,
                                      n_heads=n_heads, head_dim=head_dim)
        xp = _out_ffn_call(ymp.reshape(bp * tp, -1), hp, mod_p, wout, wgu2, wdn2, fg,
                           tm=tm_p, tiles_per_seq=tp // tm_p, tf=tf, final=final)

        hs, ps = _ffn_in_call(xs, mod_s, wgu1, wdn1, win, tm=tm_s, tiles_per_seq=1, tf=tf)
        ps_bm = ps.reshape(ts, bs, -1).transpose(1, 0, 2)
        yms, sh_s, wk_s, v_s = _mixer_short_call(ps_bm, state_shift[l][:, None, :], state_wkv[l], prm,
                                                 n_heads=n_heads, head_dim=head_dim)
        yms_tm = yms.transpose(1, 0, 2).reshape(ts * bs, -1)
        xs = _out_ffn_call(yms_tm, hs, mod_s, wout, wgu2, wdn2, fg,
                           tm=tm_s, tiles_per_seq=1, tf=tf, final=final)

        shp.append(sh_p[:, 0]); wkp.append(wk_p); shs.append(sh_s[:, 0]); wks.append(wk_s); cvs.append(v_s)

    y_prompt = xp.reshape(bp, tp, d)
    y_sample = xs.reshape(ts, bs, d).transpose(1, 0, 2)
    return (y_prompt, y_sample, jnp.stack(shp), jnp.stack(wkp), jnp.stack(shs), jnp.stack(wks), jnp.stack(cvs))
```

```python
import functools
import math

import jax
import jax.numpy as jnp
import numpy as np
from jax import lax
from jax.experimental import pallas as pl
from jax.experimental.pallas import tpu as pltpu

_F32 = jnp.float32
_BF16 = jnp.bfloat16

RMS_EPS = 1e-6
LN_EPS = 1e-5
GN_EPS = 64e-5
N_MOD = 9
WKV_CHUNK = 64
HEAD_GROUP = 4
INV_BLOCK = 16
GMLP_CHUNK = 128
SEG_ROWS = 256
SUBLANES = 8
DECAY_SCALE = math.exp(-0.5)

_V7X_VMEM_BYTES = 64 * 1024 * 1024
_VMEM_LIMIT = _V7X_VMEM_BYTES - 8 * 1024 * 1024


def _dot(a, b):
    return jnp.dot(a.astype(_BF16), b.astype(_BF16), preferred_element_type=_F32)


def _dot_nt(a, b):
    return lax.dot_general(a.astype(_BF16), b.astype(_BF16), (((1,), (1,)), ((), ())),
                           preferred_element_type=_F32)


def _dot_tn(a, b):
    return lax.dot_general(a.astype(_BF16), b.astype(_BF16), (((0,), (0,)), ((), ())),
                           preferred_element_type=_F32)


def _split_hi_lo(x):
    hi = x.astype(_BF16)
    lo = (x - hi.astype(_F32)).astype(_BF16)
    return hi, lo


def _dot_sel_rhs(x, sel):
    hi, lo = _split_hi_lo(x)
    return (jnp.dot(hi, sel, preferred_element_type=_F32)
            + jnp.dot(lo, sel, preferred_element_type=_F32))


def _dot_sel_lhs(sel, x):
    hi, lo = _split_hi_lo(x)
    return (jnp.dot(sel, hi, preferred_element_type=_F32)
            + jnp.dot(sel, lo, preferred_element_type=_F32))


def _div_pow2(x, n):
    assert n & (n - 1) == 0
    return lax.shift_right_logical(x, jnp.int32(n.bit_length() - 1))


def _mod_pow2(x, n):
    assert n & (n - 1) == 0
    return lax.bitwise_and(x, jnp.int32(n - 1))


def _rms(x):
    return x * lax.rsqrt(jnp.mean(x * x, -1, keepdims=True) + RMS_EPS)


def _sigmoid(x):
    return 1.0 / (1.0 + jnp.exp(-x))


def _affine(x, scale, shift=None):
    mb = scale.shape[0]
    if mb == 1:
        y = x * scale
        return y if shift is None else y + shift
    tm, d = x.shape
    y = x.reshape(mb, tm // mb, d) * scale[:, None, :]
    if shift is not None:
        y = y + shift[:, None, :]
    return y.reshape(tm, d)


def _swiglu(n_bf, wgu_ref, wdn_ref, tf):
    d_ff = wdn_ref.shape[0]
    acc = None
    for j in range(d_ff // tf):
        g = jnp.dot(n_bf, wgu_ref[:, j * tf:(j + 1) * tf], preferred_element_type=_F32)
        u = jnp.dot(n_bf, wgu_ref[:, d_ff + j * tf:d_ff + (j + 1) * tf], preferred_element_type=_F32)
        hm = (g * _sigmoid(g) * u).astype(_BF16)
        part = jnp.dot(hm, wdn_ref[j * tf:(j + 1) * tf, :], preferred_element_type=_F32)
        acc = part if acc is None else acc + part
    return acc


def _mod_kernel(c_ref, w_ref, b_ref, o_ref):
    c = c_ref[...]
    s = c * _sigmoid(c)
    o_ref[...] = _dot(s, w_ref[...]) + b_ref[...]


def _mod_call(c_all, w_ada, b_ada):
    n, d = c_all.shape
    return pl.pallas_call(
        _mod_kernel,
        out_shape=jax.ShapeDtypeStruct((n, N_MOD * d), _F32),
        grid=(N_MOD,),
        in_specs=[pl.BlockSpec((n, d), lambda j: (0, 0)),
                  pl.BlockSpec((d, d), lambda j: (0, j)),
                  pl.BlockSpec((1, d), lambda j: (0, j))],
        out_specs=pl.BlockSpec((n, d), lambda j: (0, j)),
        compiler_params=pltpu.CompilerParams(dimension_semantics=("arbitrary",)),
        name="adaln_mod",
    )(c_all, w_ada, b_ada.reshape(1, N_MOD * d))


def _ffn_in_kernel(x_ref, mod_ref, wgu_ref, wdn_ref, win_ref, h_ref, p_ref, *, tf):
    x = x_ref[...]
    sh1, sc1, g1, sh2, sc2 = (mod_ref[j] for j in range(5))
    n1 = _affine(_rms(x), 1.0 + sc1, sh1).astype(_BF16)
    h = x + _affine(_swiglu(n1, wgu_ref, wdn_ref, tf), 0.5 * g1)
    h_ref[...] = h
    n2 = _affine(_rms(h), 1.0 + sc2, sh2).astype(_BF16)
    p_ref[...] = jnp.dot(n2, win_ref[...], preferred_element_type=_F32)


def _resident(shape):
    nd = len(shape)
    return pl.BlockSpec(shape, lambda *_: (0,) * nd, pipeline_mode=pl.Buffered(1))


def _mod_spec(mod, tiles_per_seq):
    _, n_mod, mb, d = mod.shape
    return pl.BlockSpec((None, n_mod, mb, d), lambda i: (i // tiles_per_seq, 0, 0, 0))


def _ffn_in_call(x, mod, wgu, wdn, win, *, tm, tiles_per_seq, tf):
    n, d = x.shape
    pc = win.shape[1]
    return pl.pallas_call(
        functools.partial(_ffn_in_kernel, tf=tf),
        out_shape=(jax.ShapeDtypeStruct((n, d), _F32), jax.ShapeDtypeStruct((n, pc), _F32)),
        grid=(n // tm,),
        in_specs=[pl.BlockSpec((tm, d), lambda i: (i, 0)),
                  _mod_spec(mod, tiles_per_seq),
                  _resident(wgu.shape), _resident(wdn.shape), _resident(win.shape)],
        out_specs=(pl.BlockSpec((tm, d), lambda i: (i, 0)),
                   pl.BlockSpec((tm, pc), lambda i: (i, 0))),
        compiler_params=pltpu.CompilerParams(dimension_semantics=("arbitrary",),
                                             vmem_limit_bytes=_VMEM_LIMIT),
        name="ffn1_inproj",
    )(x, mod, wgu, wdn, win)


def _out_ffn_kernel(ym_ref, h_ref, mod_ref, wout_ref, wgu_ref, wdn_ref, fg_ref, o_ref, *, tf, final):
    g2, sh3, sc3, g3 = (mod_ref[j] for j in range(5, 9))
    h = h_ref[...] + _affine(jnp.dot(ym_ref[...].astype(_BF16), wout_ref[...],
                                     preferred_element_type=_F32), g2)
    n3 = _affine(_rms(h), 1.0 + sc3, sh3).astype(_BF16)
    h = h + _affine(_swiglu(n3, wgu_ref, wdn_ref, tf), 0.5 * g3)
    if final:
        h = _rms(h) * fg_ref[...]
    o_ref[...] = h


def _out_ffn_call(ym, h, mod, wout, wgu, wdn, fg, *, tm, tiles_per_seq, tf, final):
    n, d = h.shape
    return pl.pallas_call(
        functools.partial(_out_ffn_kernel, tf=tf, final=final),
        out_shape=jax.ShapeDtypeStruct((n, d), _F32),
        grid=(n // tm,),
        in_specs=[pl.BlockSpec((tm, ym.shape[1]), lambda i: (i, 0)),
                  pl.BlockSpec((tm, d), lambda i: (i, 0)),
                  _mod_spec(mod, tiles_per_seq),
                  _resident(wout.shape), _resident(wgu.shape), _resident(wdn.shape),
                  _resident(fg.shape)],
        out_specs=pl.BlockSpec((tm, d), lambda i: (i, 0)),
        compiler_params=pltpu.CompilerParams(dimension_semantics=("arbitrary",),
                                             vmem_limit_bytes=_VMEM_LIMIT),
        name="outproj_ffn2",
    )(ym, h, mod, wout, wgu, wdn, fg)


def _block_diag(y, mask_bf):
    reps = mask_bf.shape[0] // y.shape[0]
    return jnp.concatenate([y.astype(_BF16)] * reps, axis=0) * mask_bf


def _nilpotent_inverse(n, bd, order):
    every = range(len(n))
    c = n[0].shape[0]
    tm = n
    levels = order.bit_length() - 2
    if levels <= 0:
        return tm
    nb = bd(n)
    npow = [_dot(n[i], nb[i]) for i in every]
    for lvl in range(levels):
        nb = bd(npow)
        if lvl + 1 < levels:
            res = [_dot(jnp.concatenate([tm[i], npow[i]], axis=0), nb[i]) for i in every]
            tm = [tm[i] + npow[i] + res[i][:c] for i in every]
            npow = [res[i][c:] for i in every]
        else:
            tm = [tm[i] + npow[i] + _dot(tm[i], nb[i]) for i in every]
    return tm


def _wkv_local(rt, at, bt, kt, vv, mask_bf, strict, incl, diag):
    n = len(rt)
    c = rt[0].shape[0]
    every = range(n)
    bd = lambda xs: [_block_diag(x, mask_bf) for x in xs]
    x = [jnp.concatenate([at[i], rt[i]], axis=0) for i in every]
    bb, kb = bd(bt), bd(kt)
    xb = [_dot_nt(x[i], bb[i]) for i in every]
    xk = [_dot_nt(x[i], kb[i]) for i in every]
    lab = [xb[i][:c] * strict for i in every]
    mrb = [xb[i][c:] * incl for i in every]
    lak = [xk[i][:c] * strict for i in every]
    mrk = [xk[i][c:] * incl for i in every]
    vb = bd(vv)
    wy = [_dot(jnp.concatenate([lak[i], mrk[i]], axis=0), vb[i]) for i in every]
    w1 = [wy[i][:c] for i in every]
    y0 = [wy[i][c:] for i in every]
    if isinstance(diag, int):
        tm = _nilpotent_inverse(lab, bd, diag)
    else:
        assert c == 4 * INV_BLOCK
        nd = [lab[i] * diag for i in every]
        noff = [lab[i] - nd[i] for i in every]
        tdm = _nilpotent_inverse(nd, bd, INV_BLOCK)
        noffb = bd(noff)
        m1 = [noff[i] + _dot(tdm[i], noffb[i]) for i in every]
        m1b = bd(m1)
        m2 = [_dot(m1[i], m1b[i]) for i in every]
        m2b = bd(m2)
        q = [m1[i] + m2[i] + _dot(m1[i], m2b[i]) for i in every]
        tdmb = bd(tdm)
        tm = [q[i] + tdm[i] + _dot(q[i], tdmb[i]) for i in every]
    atb = bd(at)
    ap = [at[i] + _dot(tm[i], atb[i]) for i in every]
    w1b = bd(w1)
    u0 = [w1[i] + _dot(tm[i], w1b[i]) for i in every]
    return [(ap[i], u0[i], mrb[i], y0[i]) for i in every]


def _wkv_state_step(ap, rt, u0, vv, bdec, kdec, pc, st, mask):
    every = range(len(rt))
    c = rt[0].shape[0]
    ar = [_dot_nt(jnp.concatenate([ap[i], rt[i]], axis=0), st[i]) for i in every]
    u = [ar[i][:c] + u0[i] for i in every]
    upd = [_dot_tn(jnp.concatenate([u[i], vv[i]], axis=0), jnp.concatenate([bdec[i], kdec[i]], axis=0))
           for i in every]
    st_new = [st[i] * pc[i] + upd[i] * mask for i in every]
    return u, [ar[i][c:] for i in every], st_new


def _token_features(p, first, trow, prm, segsum, rw):
    mu_ref, w0_ref, a0_ref, wwa_ref, wg_ref, kk_ref, ka_ref = prm
    r_cols = mu_ref.shape[1]
    pr = p[:, :r_cols]
    prev = jnp.where(trow == 0, first, pltpu.roll(pr, 1, 0))
    xm = pr + (prev - pr) * mu_ref[...]
    r = xm[:, 0:rw]
    k = xm[:, rw:2 * rw]
    v = xm[:, 2 * rw:3 * rw]
    lo_w = wwa_ref.shape[0]
    x_lo = xm[:, 3 * rw:3 * rw + lo_w]
    gd = xm[:, 3 * rw + lo_w:r_cols]
    lane = lax.broadcasted_iota(jnp.int32, (1, lo_w), 1)
    x_lo = jnp.where(lane < lo_w // 2, jnp.tanh(x_lo), x_lo)
    wa = _dot(x_lo, wwa_ref[...])
    wpre = w0_ref[...] + wa[:, :rw]
    lw = -DECAY_SCALE * _sigmoid(wpre)
    a = _sigmoid(a0_ref[...] + wa[:, rw:])
    gate = _dot(_sigmoid(gd), wg_ref[...])
    kkv = k * kk_ref[...]
    kkn = kkv * lax.rsqrt(jnp.maximum(segsum(kkv * kkv), 1e-24))
    k2 = k * (1.0 + (a - 1.0) * ka_ref[...])
    return dict(r=r, v=v, lw=lw, a=a, gate=gate, kkn=kkn, k2=k2)


def _decay_factors(f, cum, tot):
    pinc = jnp.exp(cum)
    pinv = jnp.exp(-cum)
    pend = jnp.exp(tot - cum)
    beta = f["kkn"] * f["a"]
    return dict(rt=f["r"] * pinc, at=-f["kkn"] * jnp.exp(cum - f["lw"]), bt=beta * pinv, kt=f["k2"] * pinv,
                bdec=beta * pend, kdec=f["k2"] * pend, pinc=pinc)


def _rwkv_output(yw, f, rk_ref, gnw_ref, gnb_ref, segsum, head_dim):
    inv_n = 1.0 / head_dim
    mean = segsum(yw) * inv_n
    dev = yw - mean
    var = segsum(dev * dev) * inv_n
    yn = dev * lax.rsqrt(var + GN_EPS) * gnw_ref[...] + gnb_ref[...]
    bonus = segsum(f["r"] * f["k2"] * rk_ref[...]) * f["v"]
    return (yn + bonus) * f["gate"]


def _gmlp_branch(p, r_cols, lng_ref, lnb_ref, ws2_ref, tril2, bsf_ref):
    gwid = lng_ref.shape[1]
    ct = ws2_ref.shape[1]
    rows = p.shape[0]
    pu = p[:, r_cols:r_cols + gwid]
    pv = p[:, r_cols + gwid:r_cols + 2 * gwid]
    pm = jnp.mean(pv, -1, keepdims=True)
    pd = pv - pm
    pvar = jnp.mean(pd * pd, -1, keepdims=True)
    vn = pd * lax.rsqrt(pvar + LN_EPS) * lng_ref[...] + lnb_ref[...]
    n_pair = ws2_ref.shape[0]
    pw = gwid // n_pair
    first_half = lax.broadcasted_iota(jnp.int32, (1, pw), 1) < pw // 2
    ws_tril = [(ws2_ref[j] * tril2).astype(_BF16) for j in range(n_pair)]
    mixed_rows = []
    for q in range(rows // ct):
        outs = []
        for j in range(n_pair):
            vp = vn[q * ct:(q + 1) * ct, j * pw:(j + 1) * pw]
            rhs = jnp.concatenate([jnp.where(first_half, vp, 0.0), jnp.where(first_half, 0.0, vp)], axis=0)
            outs.append(_dot(ws_tril[j], rhs))
        mixed_rows.append(jnp.concatenate(outs, axis=1) + bsf_ref[...])
    mixed = mixed_rows[0] if len(mixed_rows) == 1 else jnp.concatenate(mixed_rows, axis=0)
    return pu * mixed, vn


def _state_from_heads(wkv_ref, s, g, head_dim):
    zero = jnp.zeros((head_dim, head_dim), _F32)
    blocks = []
    for h in range(HEAD_GROUP):
        blk = wkv_ref[s, g * HEAD_GROUP + h]
        blocks.append(jnp.concatenate([zero] * h + [blk] + [zero] * (HEAD_GROUP - 1 - h), axis=1))
    return jnp.concatenate(blocks, axis=0)


def _state_to_heads(wkv_ref, s, g, st, head_dim):
    for h in range(HEAD_GROUP):
        wkv_ref[s, g * HEAD_GROUP + h] = st[h * head_dim:(h + 1) * head_dim, h * head_dim:(h + 1) * head_dim]


def _mixer_kernel(p_ref, mu_ref, w0_ref, a0_ref, wwa_ref, wg_ref, kk_ref, ka_ref, rk_ref, gnw_ref, gnb_ref,
                  lng_ref, lnb_ref, ws2_ref, bsf_ref, hmask_ref, hmaskb_ref, tmask_ref, segm_ref, tril2_ref,
                  y_ref, shift_ref, wkv_ref, st_s, carry_s, *, n_heads, head_dim):
    nb, tr, _ = p_ref.shape
    c = WKV_CHUNK
    gw = HEAD_GROUP * head_dim
    n_groups = n_heads // HEAD_GROUP
    rw = n_heads * head_dim
    rows = nb * tr
    ti = pl.program_id(1)
    n_t = pl.num_programs(1)
    r_cols = mu_ref.shape[1]

    @pl.when(ti == 0)
    def _():
        carry_s[...] = jnp.zeros_like(carry_s)
        st_s[...] = jnp.zeros_like(st_s)

    mask = hmask_ref[...]
    mask_bf = hmaskb_ref[...]
    strict, incl, diag = tmask_ref[0], tmask_ref[1], tmask_ref[2]
    tri_bf = segm_ref[...]

    def segsum(x):
        return jnp.concatenate([_dot_sel_rhs(x[:, g * gw:(g + 1) * gw], mask_bf) for g in range(n_groups)],
                               axis=1)

    p = p_ref[0] if nb == 1 else jnp.concatenate([p_ref[s] for s in range(nb)], axis=0)
    trow = _mod_pow2(lax.broadcasted_iota(jnp.int32, (rows, 1), 0), tr)
    if nb == 1:
        first = carry_s[0]
    else:
        first = jnp.concatenate([jnp.broadcast_to(carry_s[s], (tr, r_cols)) for s in range(nb)], axis=0)
    for s in range(nb):
        carry_s[s] = p_ref[s, tr - 1:tr, :r_cols]
    f = _token_features(p, first, trow, (mu_ref, w0_ref, a0_ref, wwa_ref, wg_ref, kk_ref, ka_ref), segsum, rw)
    sr = tri_bf.shape[0]
    cum = jnp.concatenate([_dot_sel_lhs(tri_bf, f["lw"][i * sr:(i + 1) * sr]) for i in range(rows // sr)],
                          axis=0)
    tot = jnp.concatenate([jnp.broadcast_to(cum[j * c + c - 1:j * c + c], (c, rw)) for j in range(rows // c)],
                          axis=0)
    d = _decay_factors(f, cum, tot)

    n_chunks = rows // c
    chunks_per_seq = tr // c

    def blk(x, j, g):
        return x[j * c:(j + 1) * c, g * gw:(g + 1) * gw]

    probs = [(j, g) for j in range(n_chunks) for g in range(n_groups)]
    cut = lambda x, ps: [blk(x, j, g) for j, g in ps]
    local = dict(zip(probs, _wkv_local(cut(d["rt"], probs), cut(d["at"], probs), cut(d["bt"], probs),
                                       cut(d["kt"], probs), cut(f["v"], probs), mask_bf, strict, incl, diag)))
    y_blk = {}
    for jj in range(chunks_per_seq):
        ps = [(s * chunks_per_seq + jj, g) for s in range(nb) for g in range(n_groups)]
        loc = [local[pr_] for pr_ in ps]
        sts = [st_s[j // chunks_per_seq, g] for j, g in ps]
        pcs = [blk(d["pinc"], j, g)[c - 1:c] for j, g in ps]
        u, ys, st_new = _wkv_state_step([l[0] for l in loc], cut(d["rt"], ps), [l[1] for l in loc],
                                        cut(f["v"], ps), cut(d["bdec"], ps), cut(d["kdec"], ps), pcs, sts, mask)
        for i, (j, g) in enumerate(ps):
            st_s[j // chunks_per_seq, g] = st_new[i]
            y_blk[j, g] = ys[i] + _dot(loc[i][2], _block_diag(u[i], mask_bf)) + loc[i][3]
    y_rows = [jnp.concatenate([y_blk[j, g] for g in range(n_groups)], axis=1) for j in range(n_chunks)]
    yw = y_rows[0] if n_chunks == 1 else jnp.concatenate(y_rows, axis=0)

    y_r = _rwkv_output(yw, f, rk_ref, gnw_ref, gnb_ref, segsum, head_dim)
    y_c, _ = _gmlp_branch(p, r_cols, lng_ref, lnb_ref, ws2_ref, tril2_ref[...], bsf_ref)
    y_all = jnp.concatenate([y_r, y_c], axis=1)
    for s in range(nb):
        y_ref[s] = y_all[s * tr:(s + 1) * tr]

    @pl.when(ti == n_t - 1)
    def _():
        shift_ref[...] = carry_s[...]
        for s in range(nb):
            for g in range(n_groups):
                _state_to_heads(wkv_ref, s, g, st_s[s, g], head_dim)


def _mixer_short_kernel(p_ref, shift0_ref, wkv0_ref, mu_ref, w0_ref, a0_ref, wwa_ref, wg_ref, kk_ref, ka_ref,
                        rk_ref, gnw_ref, gnb_ref, lng_ref, lnb_ref, ws2_ref, bsf_ref, hmask_ref, hmaskb_ref,
                        tmask_ref, segm_ref, tril2_ref, y_ref, shift_ref, wkv_ref, vn_ref, *, n_heads, head_dim):
    nb, ts, _ = p_ref.shape
    c = WKV_CHUNK
    assert nb * ts == c
    gw = HEAD_GROUP * head_dim
    n_groups = n_heads // HEAD_GROUP
    rw = n_heads * head_dim
    r_cols = mu_ref.shape[1]

    mask = hmask_ref[...]
    mask_bf = hmaskb_ref[...]
    strict, incl = tmask_ref[0], tmask_ref[1]

    def segsum(x):
        return jnp.concatenate([_dot_sel_rhs(x[:, g * gw:(g + 1) * gw], mask_bf) for g in range(n_groups)],
                               axis=1)

    p = p_ref[...].reshape(c, p_ref.shape[2])
    trow = _mod_pow2(lax.broadcasted_iota(jnp.int32, (c, 1), 0), ts)
    first = jnp.concatenate([jnp.broadcast_to(shift0_ref[s], (ts, r_cols)) for s in range(nb)], axis=0)
    for s in range(nb):
        shift_ref[s] = p_ref[s, ts - 1:ts, :r_cols]
    f = _token_features(p, first, trow, (mu_ref, w0_ref, a0_ref, wwa_ref, wg_ref, kk_ref, ka_ref), segsum, rw)
    cum = _dot_sel_lhs(segm_ref[...], f["lw"])
    tot = jnp.concatenate([jnp.broadcast_to(cum[s * ts + ts - 1:(s + 1) * ts], (ts, rw)) for s in range(nb)],
                          axis=0)
    d = _decay_factors(f, cum, tot)

    groups = range(n_groups)
    gcut = lambda x: [x[:, g * gw:(g + 1) * gw] for g in groups]
    local = _wkv_local(gcut(d["rt"]), gcut(d["at"]), gcut(d["bt"]), gcut(d["kt"]), gcut(f["v"]), mask_bf,
                       strict, incl, ts)

    ps = [(s, g) for s in range(nb) for g in groups]
    scut = lambda x: [x[s * ts:(s + 1) * ts, g * gw:(g + 1) * gw] for s, g in ps]
    lcut = lambda idx: [local[g][idx][s * ts:(s + 1) * ts] for s, g in ps]
    sts = [_state_from_heads(wkv0_ref, s, g, head_dim) for s, g in ps]
    pcs = [d["pinc"][s * ts + ts - 1:(s + 1) * ts, g * gw:(g + 1) * gw] for s, g in ps]
    u, ys, st_new = _wkv_state_step(lcut(0), scut(d["rt"]), lcut(1), scut(f["v"]), scut(d["bdec"]),
                                    scut(d["kdec"]), pcs, sts, mask)
    for i, (s, g) in enumerate(ps):
        _state_to_heads(wkv_ref, s, g, st_new[i], head_dim)
    rows_of = lambda xs, g: jnp.concatenate([xs[i] for i, (s, gg) in enumerate(ps) if gg == g], axis=0)
    yw = jnp.concatenate([rows_of(ys, g) + _dot(local[g][2], _block_diag(rows_of(u, g), mask_bf)) + local[g][3]
                          for g in groups], axis=1)

    y_r = _rwkv_output(yw, f, rk_ref, gnw_ref, gnb_ref, segsum, head_dim)
    y_c, vn = _gmlp_branch(p, r_cols, lng_ref, lnb_ref, ws2_ref, tril2_ref[...], bsf_ref)
    y_ref[...] = jnp.concatenate([y_r, y_c], axis=1).reshape(nb, ts, -1)
    vn_ref[...] = vn.reshape(nb, ts, -1)


def _mixer_masks(rows, seq, ct, head_dim):
    c, gw = WKV_CHUNK, HEAD_GROUP * head_dim
    blk = np.arange(gw) // head_dim
    hmask = (blk[:, None] == blk[None, :]).astype(np.float32)
    t = np.arange(c)[:, None]
    s = np.arange(HEAD_GROUP * c)[None, :] % c
    strict = (s < t) & (s // seq == t // seq)
    incl = (s <= t) & (s // seq == t // seq)
    tmask = np.stack([strict, incl, strict & (s // INV_BLOCK == t // INV_BLOCK)]).astype(np.float32)
    r = np.arange(rows)
    seg = min(c, seq)
    same = (r[:, None] // seg) == (r[None, :] // seg)
    segm = (same & (r[None, :] <= r[:, None])).astype(np.float32)
    ctm = np.arange(ct)
    tril1 = ((ctm[None, :] <= ctm[:, None]) & (ctm[None, :] // seq == ctm[:, None] // seq))
    tril2 = np.concatenate([tril1, tril1], axis=1).astype(np.float32)
    return [jnp.asarray(hmask), jnp.asarray(hmask, _BF16), jnp.asarray(tmask), jnp.asarray(segm, _BF16),
            jnp.asarray(tril2)]


def _gmlp_weights(w_s, b_s, ct, seq, gwid):
    hc = w_s.shape[0]
    n_seq = ct // min(seq, ct)
    ws = w_s[:, :ct // n_seq, :ct // n_seq]
    bs = b_s[:, :ct // n_seq]
    if n_seq > 1:
        ws = jnp.einsum("st,hij->hsitj", jnp.eye(n_seq, dtype=ws.dtype), ws).reshape(hc, ct, ct)
        bs = jnp.tile(bs, (1, n_seq))
    ws2 = ws.reshape(hc // 2, 2, ct, ct).transpose(0, 2, 1, 3).reshape(hc // 2, ct, 2 * ct)
    return ws2, jnp.repeat(bs.T, gwid // hc, axis=1)


def _mixer_params(prm, rows, seq, ct, head_dim):
    gwid = prm["lng"].shape[1]
    ws2, bsf = _gmlp_weights(prm["w_s"], prm["b_s"], ct, seq, gwid)
    params = [prm["mu"], prm["w0"], prm["a0"], prm["wwa"], prm["wg"], prm["k_k"], prm["k_a"], prm["r_k"],
              prm["gn_w"], prm["gn_b"], prm["lng"], prm["lnb"], ws2, bsf]
    return params + _mixer_masks(rows, seq, ct, head_dim)


def _full_spec(a):
    nd = a.ndim
    return pl.BlockSpec(a.shape, lambda b, i: (0,) * nd)


def _mixer_call(p, prm, *, nb, tr, n_heads, head_dim):
    nseq, t, pcols = p.shape
    rw = n_heads * head_dim
    r_cols = prm["mu"].shape[1]
    gwid = prm["lng"].shape[1]
    gw = HEAD_GROUP * head_dim
    n_groups = n_heads // HEAD_GROUP
    rows = nb * tr
    assert nseq % nb == 0 and t % tr == 0 and tr % GMLP_CHUNK == 0 and rows % SEG_ROWS == 0
    params = _mixer_params(prm, SEG_ROWS, t, GMLP_CHUNK, head_dim)
    return pl.pallas_call(
        functools.partial(_mixer_kernel, n_heads=n_heads, head_dim=head_dim),
        out_shape=(jax.ShapeDtypeStruct((nseq, t, rw + gwid), _F32),
                   jax.ShapeDtypeStruct((nseq, 1, r_cols), _F32),
                   jax.ShapeDtypeStruct((nseq, n_heads, head_dim, head_dim), _F32)),
        grid=(nseq // nb, t // tr),
        in_specs=[pl.BlockSpec((nb, tr, pcols), lambda b, i: (b, i, 0))] + [_full_spec(a) for a in params],
        out_specs=(pl.BlockSpec((nb, tr, rw + gwid), lambda b, i: (b, i, 0)),
                   pl.BlockSpec((nb, 1, r_cols), lambda b, i: (b, 0, 0)),
                   pl.BlockSpec((nb, n_heads, head_dim, head_dim), lambda b, i: (b, 0, 0, 0))),
        scratch_shapes=[pltpu.VMEM((nb, n_groups, gw, gw), _F32), pltpu.VMEM((nb, 1, r_cols), _F32)],
        compiler_params=pltpu.CompilerParams(dimension_semantics=("arbitrary", "arbitrary"),
                                             vmem_limit_bytes=_VMEM_LIMIT),
        name="mixers_fresh",
    )(p, *params)


def _mixer_short_call(p, shift0, wkv0, prm, *, n_heads, head_dim):
    nseq, ts, pcols = p.shape
    nb = WKV_CHUNK // ts
    assert nb * ts == WKV_CHUNK and ts % SUBLANES == 0 and nseq % nb == 0
    rw = n_heads * head_dim
    r_cols = prm["mu"].shape[1]
    gwid = prm["lng"].shape[1]
    params = _mixer_params(prm, WKV_CHUNK, ts, WKV_CHUNK, head_dim)
    seqs = lambda *tail: pl.BlockSpec((nb,) + tail, lambda b, i: (b,) + (0,) * len(tail))
    return pl.pallas_call(
        functools.partial(_mixer_short_kernel, n_heads=n_heads, head_dim=head_dim),
        out_shape=(jax.ShapeDtypeStruct((nseq, ts, rw + gwid), _F32),
                   jax.ShapeDtypeStruct((nseq, 1, r_cols), _F32),
                   jax.ShapeDtypeStruct((nseq, n_heads, head_dim, head_dim), _F32),
                   jax.ShapeDtypeStruct((nseq, ts, gwid), _F32)),
        grid=(nseq // nb, 1),
        in_specs=[seqs(ts, pcols), seqs(1, r_cols), seqs(n_heads, head_dim, head_dim)]
                 + [_full_spec(a) for a in params],
        out_specs=(seqs(ts, rw + gwid), seqs(1, r_cols), seqs(n_heads, head_dim, head_dim), seqs(ts, gwid)),
        compiler_params=pltpu.CompilerParams(dimension_semantics=("arbitrary", "arbitrary"),
                                             vmem_limit_bytes=_VMEM_LIMIT),
        name="mixers_state",
    )(p, shift0, wkv0, *params)


def _pick_tile(n, pref):
    t = min(n, pref)
    while n % t:
        t //= 2
    return t


def kernel(x_prompt, x_sample, state_shift, state_wkv, c_prompt, c_sample, w_ada, b_ada, ffn1_gu, ffn1_dn, w_in, mu_shift, w0, w_lora_up, a0, a_lora_up, g_lora_up, k_k, k_a, r_k, gn_w, gn_b, ln_v_g, ln_v_b, w_s, b_s, w_out, ffn2_gu, ffn2_dn, final_g):
    depth = w_ada.shape[0]
    bp, tp, d = x_prompt.shape
    bs, ts, _ = x_sample.shape
    n_heads, head_dim = r_k.shape[1], r_k.shape[2]
    rw = n_heads * head_dim
    w_rank, a_rank = w_lora_up.shape[1], a_lora_up.shape[1]
    assert w_rank == a_rank and n_heads % HEAD_GROUP == 0 and head_dim == WKV_CHUNK
    d_ff = ffn1_dn.shape[1]
    tf = 256 if d_ff % 256 == 0 else 128

    xp = x_prompt.reshape(bp * tp, d)
    xs = x_sample.reshape(bs * ts, d)
    n_c = bp + bs
    n_c_pad = -(-n_c // 16) * 16
    c_all = jnp.concatenate([c_prompt, c_sample, jnp.zeros((n_c_pad - n_c, d), _F32)], axis=0)

    tm_p = _pick_tile(tp, 512)
    tr_p = _pick_tile(tp, 256)
    tm_s = ts * bs
    nb_p = _pick_tile(bp, 2)

    shp, wkp, shs, wks, cvs = [], [], [], [], []
    for l in range(depth):
        mod = _mod_call(c_all, w_ada[l], b_ada[l])
        mod_p = mod[:bp].reshape(bp, N_MOD, 1, d)
        mod_s = mod[bp:n_c].reshape(bs, N_MOD, d).transpose(1, 0, 2)[None]

        wgu1, wdn1 = ffn1_gu[l].astype(_BF16), ffn1_dn[l].astype(_BF16)
        wgu2, wdn2 = ffn2_gu[l].astype(_BF16), ffn2_dn[l].astype(_BF16)
        win, wout = w_in[l].astype(_BF16), w_out[l].astype(_BF16)
        zw = jnp.zeros((w_rank, rw), _F32)
        wwa = jnp.concatenate([jnp.concatenate([w_lora_up[l], zw], axis=1),
                               jnp.concatenate([zw, a_lora_up[l]], axis=1)], axis=0).astype(_BF16)
        row = lambda a: a.reshape(1, -1)
        prm = dict(mu=row(mu_shift[l]), w0=row(w0[l]), a0=row(a0[l]), wwa=wwa,
                   wg=g_lora_up[l].astype(_BF16), k_k=row(k_k[l]), k_a=row(k_a[l]), r_k=row(r_k[l]),
                   gn_w=row(gn_w[l]), gn_b=row(gn_b[l]), lng=row(ln_v_g[l]), lnb=row(ln_v_b[l]),
                   w_s=w_s[l], b_s=b_s[l])
        fg = final_g.reshape(1, d)
        final = l == depth - 1

        hp, pp = _ffn_in_call(xp, mod_p, wgu1, wdn1, win, tm=tm_p, tiles_per_seq=tp // tm_p, tf=tf)
        ymp, sh_p, wk_p = _mixer_call(pp.reshape(bp, tp, -1), prm, nb=nb_p, tr=tr_p,
                                      n_heads=n_heads, head_dim=head_dim)
        xp = _out_ffn_call(ymp.reshape(bp * tp, -1), hp, mod_p, wout, wgu2, wdn2, fg,
                           tm=tm_p, tiles_per_seq=tp // tm_p, tf=tf, final=final)

        hs, ps = _ffn_in_call(xs, mod_s, wgu1, wdn1, win, tm=tm_s, tiles_per_seq=1, tf=tf)
        yms, sh_s, wk_s, v_s = _mixer_short_call(ps.reshape(bs, ts, -1), state_shift[l][:, None, :], state_wkv[l], prm,
                                                 n_heads=n_heads, head_dim=head_dim)
        xs = _out_ffn_call(yms.reshape(bs * ts, -1), hs, mod_s, wout, wgu2, wdn2, fg,
                           tm=tm_s, tiles_per_seq=1, tf=tf, final=final)

        shp.append(sh_p[:, 0]); wkp.append(wk_p); shs.append(sh_s[:, 0]); wks.append(wk_s); cvs.append(v_s)

    y_prompt = xp.reshape(bp, tp, d)
    y_sample = xs.reshape(bs, ts, d)
    return (y_prompt, y_sample, jnp.stack(shp), jnp.stack(wkp), jnp.stack(shs), jnp.stack(wks), jnp.stack(cvs))
```

```python
import functools
import math

import jax
import jax.numpy as jnp
import numpy as np
from jax import lax
from jax.experimental import pallas as pl
from jax.experimental.pallas import tpu as pltpu

_F32 = jnp.float32
_BF16 = jnp.bfloat16

RMS_EPS = 1e-6
LN_EPS = 1e-5
GN_EPS = 64e-5
N_MOD = 9
WKV_CHUNK = 64
HEAD_GROUP = 4
INV_BLOCK = 16
GMLP_CHUNK = 128
SEG_ROWS = 256
SUBLANES = 8
MOD_PER_STEP = 3
DECAY_SCALE = math.exp(-0.5)

_V7X_VMEM_BYTES = 64 * 1024 * 1024
_VMEM_LIMIT = _V7X_VMEM_BYTES - 8 * 1024 * 1024


def _dot(a, b):
    return jnp.dot(a.astype(_BF16), b.astype(_BF16), preferred_element_type=_F32)


def _dot_nt(a, b):
    return lax.dot_general(a.astype(_BF16), b.astype(_BF16), (((1,), (1,)), ((), ())),
                           preferred_element_type=_F32)


def _dot_tn(a, b):
    return lax.dot_general(a.astype(_BF16), b.astype(_BF16), (((0,), (0,)), ((), ())),
                           preferred_element_type=_F32)


def _split_hi_lo(x):
    hi = x.astype(_BF16)
    lo = (x - hi.astype(_F32)).astype(_BF16)
    return hi, lo


def _dot_sel_lhs(sel, x):
    hi, lo = _split_hi_lo(x)
    return (jnp.dot(sel, hi, preferred_element_type=_F32)
            + jnp.dot(sel, lo, preferred_element_type=_F32))


def _div_pow2(x, n):
    assert n & (n - 1) == 0
    return lax.shift_right_logical(x, jnp.int32(n.bit_length() - 1))


def _mod_pow2(x, n):
    assert n & (n - 1) == 0
    return lax.bitwise_and(x, jnp.int32(n - 1))


def _rms(x):
    return x * lax.rsqrt(jnp.mean(x * x, -1, keepdims=True) + RMS_EPS)


def _sigmoid(x):
    return 1.0 / (1.0 + jnp.exp(-x))


def _affine(x, scale, shift=None):
    mb = scale.shape[0]
    if mb == 1:
        y = x * scale
        return y if shift is None else y + shift
    tm, d = x.shape
    y = x.reshape(mb, tm // mb, d) * scale[:, None, :]
    if shift is not None:
        y = y + shift[:, None, :]
    return y.reshape(tm, d)


def _swiglu(n_bf, wgu_ref, wdn_ref, tf):
    d_ff = wdn_ref.shape[0]
    acc = None
    for j in range(d_ff // tf):
        g = jnp.dot(n_bf, wgu_ref[:, j * tf:(j + 1) * tf], preferred_element_type=_F32)
        u = jnp.dot(n_bf, wgu_ref[:, d_ff + j * tf:d_ff + (j + 1) * tf], preferred_element_type=_F32)
        hm = (g * _sigmoid(g) * u).astype(_BF16)
        part = jnp.dot(hm, wdn_ref[j * tf:(j + 1) * tf, :], preferred_element_type=_F32)
        acc = part if acc is None else acc + part
    return acc


def _mod_kernel(c_ref, w_ref, b_ref, o_ref):
    c = c_ref[...]
    s = c * _sigmoid(c)
    o = _dot(s, w_ref[...]) + b_ref[...]
    d = c.shape[1]
    for k in range(o_ref.shape[0]):
        o_ref[k] = o[:, k * d:(k + 1) * d]


def _mod_call(c_all, w_ada, b_ada):
    n, d = c_all.shape
    per_step = MOD_PER_STEP if N_MOD % MOD_PER_STEP == 0 else 1
    return pl.pallas_call(
        _mod_kernel,
        out_shape=jax.ShapeDtypeStruct((N_MOD, n, d), _F32),
        grid=(N_MOD // per_step,),
        in_specs=[pl.BlockSpec((n, d), lambda j: (0, 0)),
                  pl.BlockSpec((d, per_step * d), lambda j: (0, j)),
                  pl.BlockSpec((1, per_step * d), lambda j: (0, j))],
        out_specs=pl.BlockSpec((per_step, n, d), lambda j: (j, 0, 0)),
        compiler_params=pltpu.CompilerParams(dimension_semantics=("arbitrary",),
                                             vmem_limit_bytes=_VMEM_LIMIT),
        name="adaln_mod",
    )(c_all, w_ada, b_ada.reshape(1, N_MOD * d))


def _mod_rows(mod_ref, pick):
    if pick is None:
        return lambda j: mod_ref[j]
    seq0, tiles_per_seq = pick
    row = lax.rem(seq0 + lax.div(pl.program_id(0), jnp.int32(tiles_per_seq)), jnp.int32(SUBLANES))
    return lambda j: mod_ref[j, pl.ds(row, 1), :]


def _ffn_in_kernel(x_ref, mod_ref, wgu_ref, wdn_ref, win_ref, h_ref, p_ref, *, tf, pick):
    x = x_ref[...]
    mod = _mod_rows(mod_ref, pick)
    sh1, sc1, g1, sh2, sc2 = (mod(j) for j in range(5))
    n1 = _affine(_rms(x), 1.0 + sc1, sh1).astype(_BF16)
    h = x + _affine(_swiglu(n1, wgu_ref, wdn_ref, tf), 0.5 * g1)
    h_ref[...] = h
    n2 = _affine(_rms(h), 1.0 + sc2, sh2).astype(_BF16)
    p_ref[...] = jnp.dot(n2, win_ref[...], preferred_element_type=_F32)


def _resident(shape):
    nd = len(shape)
    return pl.BlockSpec(shape, lambda *_: (0,) * nd, pipeline_mode=pl.Buffered(1))


def _mod_spec(mod, seq0, seqs_per_tile, tiles_per_seq):
    n_mod, _, d = mod.shape
    if seqs_per_tile > 1:
        assert seqs_per_tile % SUBLANES == 0 and seq0 % seqs_per_tile == 0 and tiles_per_seq == 1
        first = seq0 // seqs_per_tile
        return pl.BlockSpec((n_mod, seqs_per_tile, d), lambda i: (0, first + i, 0)), None
    return (pl.BlockSpec((n_mod, SUBLANES, d), lambda i: (0, (seq0 + i // tiles_per_seq) // SUBLANES, 0)),
            (seq0, tiles_per_seq))


def _ffn_in_call(x, mod, wgu, wdn, win, *, tm, seq0, seqs_per_tile, tiles_per_seq, tf):
    n, d = x.shape
    pc = win.shape[1]
    mod_spec, pick = _mod_spec(mod, seq0, seqs_per_tile, tiles_per_seq)
    return pl.pallas_call(
        functools.partial(_ffn_in_kernel, tf=tf, pick=pick),
        out_shape=(jax.ShapeDtypeStruct((n, d), _F32), jax.ShapeDtypeStruct((n, pc), _F32)),
        grid=(n // tm,),
        in_specs=[pl.BlockSpec((tm, d), lambda i: (i, 0)), mod_spec,
                  _resident(wgu.shape), _resident(wdn.shape), _resident(win.shape)],
        out_specs=(pl.BlockSpec((tm, d), lambda i: (i, 0)),
                   pl.BlockSpec((tm, pc), lambda i: (i, 0))),
        compiler_params=pltpu.CompilerParams(dimension_semantics=("arbitrary",),
                                             vmem_limit_bytes=_VMEM_LIMIT),
        name="ffn1_inproj",
    )(x, mod, wgu, wdn, win)


def _out_ffn_kernel(ym_ref, h_ref, mod_ref, wout_ref, wgu_ref, wdn_ref, fg_ref, o_ref, *, tf, final, pick):
    mod = _mod_rows(mod_ref, pick)
    g2, sh3, sc3, g3 = (mod(j) for j in range(5, 9))
    h = h_ref[...] + _affine(jnp.dot(ym_ref[...].astype(_BF16), wout_ref[...],
                                     preferred_element_type=_F32), g2)
    n3 = _affine(_rms(h), 1.0 + sc3, sh3).astype(_BF16)
    h = h + _affine(_swiglu(n3, wgu_ref, wdn_ref, tf), 0.5 * g3)
    if final:
        h = _rms(h) * fg_ref[...]
    o_ref[...] = h


def _out_ffn_call(ym, h, mod, wout, wgu, wdn, fg, *, tm, seq0, seqs_per_tile, tiles_per_seq, tf, final):
    n, d = h.shape
    mod_spec, pick = _mod_spec(mod, seq0, seqs_per_tile, tiles_per_seq)
    return pl.pallas_call(
        functools.partial(_out_ffn_kernel, tf=tf, final=final, pick=pick),
        out_shape=jax.ShapeDtypeStruct((n, d), _F32),
        grid=(n // tm,),
        in_specs=[pl.BlockSpec((tm, ym.shape[1]), lambda i: (i, 0)),
                  pl.BlockSpec((tm, d), lambda i: (i, 0)), mod_spec,
                  _resident(wout.shape), _resident(wgu.shape), _resident(wdn.shape),
                  _resident(fg.shape)],
        out_specs=pl.BlockSpec((tm, d), lambda i: (i, 0)),
        compiler_params=pltpu.CompilerParams(dimension_semantics=("arbitrary",),
                                             vmem_limit_bytes=_VMEM_LIMIT),
        name="outproj_ffn2",
    )(ym, h, mod, wout, wgu, wdn, fg)


def _block_diag(y, mask_bf):
    reps = mask_bf.shape[0] // y.shape[0]
    return jnp.concatenate([y.astype(_BF16)] * reps, axis=0) * mask_bf


def _nilpotent_inverse(n, bd, order):
    every = range(len(n))
    c = n[0].shape[0]
    tm = n
    levels = order.bit_length() - 2
    if levels <= 0:
        return tm
    nb = bd(n)
    npow = [_dot(n[i], nb[i]) for i in every]
    for lvl in range(levels):
        nb = bd(npow)
        if lvl + 1 < levels:
            res = [_dot(jnp.concatenate([tm[i], npow[i]], axis=0), nb[i]) for i in every]
            tm = [tm[i] + npow[i] + res[i][:c] for i in every]
            npow = [res[i][c:] for i in every]
        else:
            tm = [tm[i] + npow[i] + _dot(tm[i], nb[i]) for i in every]
    return tm


def _wkv_local(rt, at, bt, kt, vv, mask_bf, strict, incl, diag):
    n = len(rt)
    c = rt[0].shape[0]
    every = range(n)
    bd = lambda xs: [_block_diag(x, mask_bf) for x in xs]
    x = [jnp.concatenate([at[i], rt[i]], axis=0) for i in every]
    bb, kb = bd(bt), bd(kt)
    xb = [_dot_nt(x[i], bb[i]) for i in every]
    xk = [_dot_nt(x[i], kb[i]) for i in every]
    lab = [xb[i][:c] * strict for i in every]
    mrb = [xb[i][c:] * incl for i in every]
    lak = [xk[i][:c] * strict for i in every]
    mrk = [xk[i][c:] * incl for i in every]
    vb = bd(vv)
    wy = [_dot(jnp.concatenate([lak[i], mrk[i]], axis=0), vb[i]) for i in every]
    w1 = [wy[i][:c] for i in every]
    y0 = [wy[i][c:] for i in every]
    if isinstance(diag, int):
        tm = _nilpotent_inverse(lab, bd, diag)
    else:
        assert c == 4 * INV_BLOCK
        nd = [lab[i] * diag for i in every]
        noff = [lab[i] - nd[i] for i in every]
        tdm = _nilpotent_inverse(nd, bd, INV_BLOCK)
        noffb = bd(noff)
        m1 = [noff[i] + _dot(tdm[i], noffb[i]) for i in every]
        m1b = bd(m1)
        m2 = [_dot(m1[i], m1b[i]) for i in every]
        m2b = bd(m2)
        q = [m1[i] + m2[i] + _dot(m1[i], m2b[i]) for i in every]
        tdmb = bd(tdm)
        tm = [q[i] + tdm[i] + _dot(q[i], tdmb[i]) for i in every]
    atb = bd(at)
    ap = [at[i] + _dot(tm[i], atb[i]) for i in every]
    w1b = bd(w1)
    u0 = [w1[i] + _dot(tm[i], w1b[i]) for i in every]
    return [(ap[i], u0[i], mrb[i], y0[i]) for i in every]


def _wkv_state_step(ap, rt, u0, vv, bdec, kdec, pc, st, mask):
    every = range(len(rt))
    c = rt[0].shape[0]
    ar = [_dot_nt(jnp.concatenate([ap[i], rt[i]], axis=0), st[i]) for i in every]
    u = [ar[i][:c] + u0[i] for i in every]
    upd = [_dot_tn(jnp.concatenate([u[i], vv[i]], axis=0), jnp.concatenate([bdec[i], kdec[i]], axis=0))
           for i in every]
    st_new = [st[i] * pc[i] + upd[i] * mask for i in every]
    return u, [ar[i][c:] for i in every], st_new


def _token_features(p, first, trow, prm, segsum, rw):
    mu_ref, w0_ref, a0_ref, wwa_ref, wg_ref, kk_ref, ka_ref = prm
    r_cols = mu_ref.shape[1]
    pr = p[:, :r_cols]
    prev = jnp.where(trow == 0, first, pltpu.roll(pr, 1, 0))
    xm = pr + (prev - pr) * mu_ref[...]
    r = xm[:, 0:rw]
    k = xm[:, rw:2 * rw]
    v = xm[:, 2 * rw:3 * rw]
    lo_w = wwa_ref.shape[0]
    x_lo = xm[:, 3 * rw:3 * rw + lo_w]
    gd = xm[:, 3 * rw + lo_w:r_cols]
    lane = lax.broadcasted_iota(jnp.int32, (1, lo_w), 1)
    x_lo = jnp.where(lane < lo_w // 2, jnp.tanh(x_lo), x_lo)
    wa = _dot(x_lo, wwa_ref[...])
    wpre = w0_ref[...] + wa[:, :rw]
    lw = -DECAY_SCALE * _sigmoid(wpre)
    a = _sigmoid(a0_ref[...] + wa[:, rw:])
    gate = _dot(_sigmoid(gd), wg_ref[...])
    kkv = k * kk_ref[...]
    kkn = kkv * lax.rsqrt(jnp.maximum(segsum(kkv * kkv), 1e-24))
    k2 = k * (1.0 + (a - 1.0) * ka_ref[...])
    return dict(r=r, v=v, lw=lw, a=a, gate=gate, kkn=kkn, k2=k2)


def _decay_factors(f, cum, tot):
    pinc = jnp.exp(cum)
    pinv = jnp.exp(-cum)
    pend = jnp.exp(tot - cum)
    beta = f["kkn"] * f["a"]
    return dict(rt=f["r"] * pinc, at=-f["kkn"] * jnp.exp(cum - f["lw"]), bt=beta * pinv, kt=f["k2"] * pinv,
                bdec=beta * pend, kdec=f["k2"] * pend, pinc=pinc)


def _rwkv_output(yw, f, rk_ref, gnw_ref, gnb_ref, segsum, head_dim):
    inv_n = 1.0 / head_dim
    mean = segsum(yw) * inv_n
    dev = yw - mean
    var = segsum(dev * dev) * inv_n
    yn = dev * lax.rsqrt(var + GN_EPS) * gnw_ref[...] + gnb_ref[...]
    bonus = segsum(f["r"] * f["k2"] * rk_ref[...]) * f["v"]
    return (yn + bonus) * f["gate"]


def _gmlp_branch(p, r_cols, lng_ref, lnb_ref, ws2_ref, tril2, bsf_ref):
    gwid = lng_ref.shape[1]
    ct = ws2_ref.shape[1]
    rows = p.shape[0]
    pu = p[:, r_cols:r_cols + gwid]
    pv = p[:, r_cols + gwid:r_cols + 2 * gwid]
    pm = jnp.mean(pv, -1, keepdims=True)
    pd = pv - pm
    pvar = jnp.mean(pd * pd, -1, keepdims=True)
    vn = pd * lax.rsqrt(pvar + LN_EPS) * lng_ref[...] + lnb_ref[...]
    n_pair = ws2_ref.shape[0]
    pw = gwid // n_pair
    first_half = lax.broadcasted_iota(jnp.int32, (1, pw), 1) < pw // 2
    ws_tril = [(ws2_ref[j] * tril2).astype(_BF16) for j in range(n_pair)]
    mixed_rows = []
    for q in range(rows // ct):
        outs = []
        for j in range(n_pair):
            vp = vn[q * ct:(q + 1) * ct, j * pw:(j + 1) * pw]
            rhs = jnp.concatenate([jnp.where(first_half, vp, 0.0), jnp.where(first_half, 0.0, vp)], axis=0)
            outs.append(_dot(ws_tril[j], rhs))
        mixed_rows.append(jnp.concatenate(outs, axis=1) + bsf_ref[...])
    mixed = mixed_rows[0] if len(mixed_rows) == 1 else jnp.concatenate(mixed_rows, axis=0)
    return pu * mixed, vn


def _state_from_heads(wkv_ref, s, g, head_dim):
    zero = jnp.zeros((head_dim, head_dim), _F32)
    blocks = []
    for h in range(HEAD_GROUP):
        blk = wkv_ref[s, g * HEAD_GROUP + h]
        blocks.append(jnp.concatenate([zero] * h + [blk] + [zero] * (HEAD_GROUP - 1 - h), axis=1))
    return jnp.concatenate(blocks, axis=0)


def _state_to_heads(wkv_ref, s, g, st, head_dim):
    for h in range(HEAD_GROUP):
        wkv_ref[s, g * HEAD_GROUP + h] = st[h * head_dim:(h + 1) * head_dim, h * head_dim:(h + 1) * head_dim]


def _mixer_kernel(p_ref, mu_ref, w0_ref, a0_ref, wwa_ref, wg_ref, kk_ref, ka_ref, rk_ref, gnw_ref, gnb_ref,
                  lng_ref, lnb_ref, ws2_ref, bsf_ref, hmask_ref, hmaskb_ref, tmask_ref, segm_ref, tril2_ref,
                  y_ref, shift_ref, wkv_ref, st_s, carry_s, *, n_heads, head_dim):
    nb, tr, _ = p_ref.shape
    c = WKV_CHUNK
    gw = HEAD_GROUP * head_dim
    n_groups = n_heads // HEAD_GROUP
    rw = n_heads * head_dim
    rows = nb * tr
    ti = pl.program_id(1)
    n_t = pl.num_programs(1)
    r_cols = mu_ref.shape[1]

    @pl.when(ti == 0)
    def _():
        carry_s[...] = jnp.zeros_like(carry_s)
        st_s[...] = jnp.zeros_like(st_s)

    mask = hmask_ref[...]
    mask_bf = hmaskb_ref[...]
    strict, incl, diag = tmask_ref[0], tmask_ref[1], tmask_ref[2]
    tri_bf = segm_ref[...]

    def segsum(x):
        return jnp.concatenate([_dot(x[:, g * gw:(g + 1) * gw], mask_bf) for g in range(n_groups)], axis=1)

    p = p_ref[0] if nb == 1 else jnp.concatenate([p_ref[s] for s in range(nb)], axis=0)
    trow = _mod_pow2(lax.broadcasted_iota(jnp.int32, (rows, 1), 0), tr)
    if nb == 1:
        first = carry_s[0]
    else:
        first = jnp.concatenate([jnp.broadcast_to(carry_s[s], (tr, r_cols)) for s in range(nb)], axis=0)
    for s in range(nb):
        carry_s[s] = p_ref[s, tr - 1:tr, :r_cols]
    f = _token_features(p, first, trow, (mu_ref, w0_ref, a0_ref, wwa_ref, wg_ref, kk_ref, ka_ref), segsum, rw)
    sr = tri_bf.shape[0]
    cum = jnp.concatenate([_dot_sel_lhs(tri_bf, f["lw"][i * sr:(i + 1) * sr]) for i in range(rows // sr)],
                          axis=0)
    tot = jnp.concatenate([jnp.broadcast_to(cum[j * c + c - 1:j * c + c], (c, rw)) for j in range(rows // c)],
                          axis=0)
    d = _decay_factors(f, cum, tot)

    n_chunks = rows // c
    chunks_per_seq = tr // c

    def blk(x, j, g):
        return x[j * c:(j + 1) * c, g * gw:(g + 1) * gw]

    probs = [(j, g) for j in range(n_chunks) for g in range(n_groups)]
    cut = lambda x, ps: [blk(x, j, g) for j, g in ps]
    local = dict(zip(probs, _wkv_local(cut(d["rt"], probs), cut(d["at"], probs), cut(d["bt"], probs),
                                       cut(d["kt"], probs), cut(f["v"], probs), mask_bf, strict, incl, diag)))
    y_blk = {}
    for jj in range(chunks_per_seq):
        ps = [(s * chunks_per_seq + jj, g) for s in range(nb) for g in range(n_groups)]
        loc = [local[pr_] for pr_ in ps]
        sts = [st_s[j // chunks_per_seq, g] for j, g in ps]
        pcs = [blk(d["pinc"], j, g)[c - 1:c] for j, g in ps]
        u, ys, st_new = _wkv_state_step([l[0] for l in loc], cut(d["rt"], ps), [l[1] for l in loc],
                                        cut(f["v"], ps), cut(d["bdec"], ps), cut(d["kdec"], ps), pcs, sts, mask)
        for i, (j, g) in enumerate(ps):
            st_s[j // chunks_per_seq, g] = st_new[i]
            y_blk[j, g] = ys[i] + _dot(loc[i][2], _block_diag(u[i], mask_bf)) + loc[i][3]
    y_rows = [jnp.concatenate([y_blk[j, g] for g in range(n_groups)], axis=1) for j in range(n_chunks)]
    yw = y_rows[0] if n_chunks == 1 else jnp.concatenate(y_rows, axis=0)

    y_r = _rwkv_output(yw, f, rk_ref, gnw_ref, gnb_ref, segsum, head_dim)
    y_c, _ = _gmlp_branch(p, r_cols, lng_ref, lnb_ref, ws2_ref, tril2_ref[...], bsf_ref)
    y_all = jnp.concatenate([y_r, y_c], axis=1)
    for s in range(nb):
        y_ref[s] = y_all[s * tr:(s + 1) * tr]

    @pl.when(ti == n_t - 1)
    def _():
        shift_ref[...] = carry_s[...]
        for s in range(nb):
            for g in range(n_groups):
                _state_to_heads(wkv_ref, s, g, st_s[s, g], head_dim)


def _mixer_short_kernel(p_ref, shift0_ref, wkv0_ref, mu_ref, w0_ref, a0_ref, wwa_ref, wg_ref, kk_ref, ka_ref,
                        rk_ref, gnw_ref, gnb_ref, lng_ref, lnb_ref, ws2_ref, bsf_ref, hmask_ref, hmaskb_ref,
                        tmask_ref, segm_ref, tril2_ref, y_ref, shift_ref, wkv_ref, vn_ref, *, n_heads, head_dim):
    nb, ts, _ = p_ref.shape
    c = WKV_CHUNK
    assert nb * ts == c
    gw = HEAD_GROUP * head_dim
    n_groups = n_heads // HEAD_GROUP
    rw = n_heads * head_dim
    r_cols = mu_ref.shape[1]

    mask = hmask_ref[...]
    mask_bf = hmaskb_ref[...]
    strict, incl = tmask_ref[0], tmask_ref[1]

    def segsum(x):
        return jnp.concatenate([_dot(x[:, g * gw:(g + 1) * gw], mask_bf) for g in range(n_groups)], axis=1)

    p = p_ref[...].reshape(c, p_ref.shape[2])
    trow = _mod_pow2(lax.broadcasted_iota(jnp.int32, (c, 1), 0), ts)
    first = jnp.concatenate([jnp.broadcast_to(shift0_ref[s], (ts, r_cols)) for s in range(nb)], axis=0)
    for s in range(nb):
        shift_ref[s] = p_ref[s, ts - 1:ts, :r_cols]
    f = _token_features(p, first, trow, (mu_ref, w0_ref, a0_ref, wwa_ref, wg_ref, kk_ref, ka_ref), segsum, rw)
    cum = _dot_sel_lhs(segm_ref[...], f["lw"])
    tot = jnp.concatenate([jnp.broadcast_to(cum[s * ts + ts - 1:(s + 1) * ts], (ts, rw)) for s in range(nb)],
                          axis=0)
    d = _decay_factors(f, cum, tot)

    groups = range(n_groups)
    gcut = lambda x: [x[:, g * gw:(g + 1) * gw] for g in groups]
    local = _wkv_local(gcut(d["rt"]), gcut(d["at"]), gcut(d["bt"]), gcut(d["kt"]), gcut(f["v"]), mask_bf,
                       strict, incl, ts)

    ps = [(s, g) for s in range(nb) for g in groups]
    scut = lambda x: [x[s * ts:(s + 1) * ts, g * gw:(g + 1) * gw] for s, g in ps]
    lcut = lambda idx: [local[g][idx][s * ts:(s + 1) * ts] for s, g in ps]
    sts = [_state_from_heads(wkv0_ref, s, g, head_dim) for s, g in ps]
    pcs = [d["pinc"][s * ts + ts - 1:(s + 1) * ts, g * gw:(g + 1) * gw] for s, g in ps]
    u, ys, st_new = _wkv_state_step(lcut(0), scut(d["rt"]), lcut(1), scut(f["v"]), scut(d["bdec"]),
                                    scut(d["kdec"]), pcs, sts, mask)
    for i, (s, g) in enumerate(ps):
        _state_to_heads(wkv_ref, s, g, st_new[i], head_dim)
    rows_of = lambda xs, g: jnp.concatenate([xs[i] for i, (s, gg) in enumerate(ps) if gg == g], axis=0)
    yw = jnp.concatenate([rows_of(ys, g) + _dot(local[g][2], _block_diag(rows_of(u, g), mask_bf)) + local[g][3]
                          for g in groups], axis=1)

    y_r = _rwkv_output(yw, f, rk_ref, gnw_ref, gnb_ref, segsum, head_dim)
    y_c, vn = _gmlp_branch(p, r_cols, lng_ref, lnb_ref, ws2_ref, tril2_ref[...], bsf_ref)
    y_ref[...] = jnp.concatenate([y_r, y_c], axis=1).reshape(nb, ts, -1)
    vn_ref[...] = vn.reshape(nb, ts, -1)


def _mixer_masks(rows, seq, ct, head_dim):
    c, gw = WKV_CHUNK, HEAD_GROUP * head_dim
    blk = np.arange(gw) // head_dim
    hmask = (blk[:, None] == blk[None, :]).astype(np.float32)
    t = np.arange(c)[:, None]
    s = np.arange(HEAD_GROUP * c)[None, :] % c
    strict = (s < t) & (s // seq == t // seq)
    incl = (s <= t) & (s // seq == t // seq)
    tmask = np.stack([strict, incl, strict & (s // INV_BLOCK == t // INV_BLOCK)]).astype(np.float32)
    r = np.arange(rows)
    seg = min(c, seq)
    same = (r[:, None] // seg) == (r[None, :] // seg)
    segm = (same & (r[None, :] <= r[:, None])).astype(np.float32)
    ctm = np.arange(ct)
    tril1 = ((ctm[None, :] <= ctm[:, None]) & (ctm[None, :] // seq == ctm[:, None] // seq))
    tril2 = np.concatenate([tril1, tril1], axis=1).astype(np.float32)
    return [jnp.asarray(hmask), jnp.asarray(hmask, _BF16), jnp.asarray(tmask), jnp.asarray(segm, _BF16),
            jnp.asarray(tril2)]


def _gmlp_weights(w_s, b_s, ct, seq, gwid):
    hc = w_s.shape[0]
    n_seq = ct // min(seq, ct)
    ws = w_s[:, :ct // n_seq, :ct // n_seq]
    bs = b_s[:, :ct // n_seq]
    if n_seq > 1:
        ws = jnp.einsum("st,hij->hsitj", jnp.eye(n_seq, dtype=ws.dtype), ws).reshape(hc, ct, ct)
        bs = jnp.tile(bs, (1, n_seq))
    ws2 = ws.reshape(hc // 2, 2, ct, ct).transpose(0, 2, 1, 3).reshape(hc // 2, ct, 2 * ct)
    return ws2, jnp.repeat(bs.T, gwid // hc, axis=1)


def _mixer_params(prm, rows, seq, ct, head_dim):
    gwid = prm["lng"].shape[1]
    ws2, bsf = _gmlp_weights(prm["w_s"], prm["b_s"], ct, seq, gwid)
    params = [prm["mu"], prm["w0"], prm["a0"], prm["wwa"], prm["wg"], prm["k_k"], prm["k_a"], prm["r_k"],
              prm["gn_w"], prm["gn_b"], prm["lng"], prm["lnb"], ws2, bsf]
    return params + _mixer_masks(rows, seq, ct, head_dim)


def _full_spec(a):
    nd = a.ndim
    return pl.BlockSpec(a.shape, lambda b, i: (0,) * nd)


def _mixer_call(p, prm, *, nb, tr, n_heads, head_dim):
    nseq, t, pcols = p.shape
    rw = n_heads * head_dim
    r_cols = prm["mu"].shape[1]
    gwid = prm["lng"].shape[1]
    gw = HEAD_GROUP * head_dim
    n_groups = n_heads // HEAD_GROUP
    rows = nb * tr
    assert nseq % nb == 0 and t % tr == 0 and tr % GMLP_CHUNK == 0 and rows % SEG_ROWS == 0
    params = _mixer_params(prm, SEG_ROWS, t, GMLP_CHUNK, head_dim)
    return pl.pallas_call(
        functools.partial(_mixer_kernel, n_heads=n_heads, head_dim=head_dim),
        out_shape=(jax.ShapeDtypeStruct((nseq, t, rw + gwid), _F32),
                   jax.ShapeDtypeStruct((nseq, 1, r_cols), _F32),
                   jax.ShapeDtypeStruct((nseq, n_heads, head_dim, head_dim), _F32)),
        grid=(nseq // nb, t // tr),
        in_specs=[pl.BlockSpec((nb, tr, pcols), lambda b, i: (b, i, 0))] + [_full_spec(a) for a in params],
        out_specs=(pl.BlockSpec((nb, tr, rw + gwid), lambda b, i: (b, i, 0)),
                   pl.BlockSpec((nb, 1, r_cols), lambda b, i: (b, 0, 0)),
                   pl.BlockSpec((nb, n_heads, head_dim, head_dim), lambda b, i: (b, 0, 0, 0))),
        scratch_shapes=[pltpu.VMEM((nb, n_groups, gw, gw), _F32), pltpu.VMEM((nb, 1, r_cols), _F32)],
        compiler_params=pltpu.CompilerParams(dimension_semantics=("arbitrary", "arbitrary"),
                                             vmem_limit_bytes=_VMEM_LIMIT),
        name="mixers_fresh",
    )(p, *params)


def _mixer_short_call(p, shift0, wkv0, prm, *, n_heads, head_dim):
    nseq, ts, pcols = p.shape
    nb = WKV_CHUNK // ts
    assert nb * ts == WKV_CHUNK and ts % SUBLANES == 0 and nseq % nb == 0
    rw = n_heads * head_dim
    r_cols = prm["mu"].shape[1]
    gwid = prm["lng"].shape[1]
    params = _mixer_params(prm, WKV_CHUNK, ts, WKV_CHUNK, head_dim)
    seqs = lambda *tail: pl.BlockSpec((nb,) + tail, lambda b, i: (b,) + (0,) * len(tail))
    return pl.pallas_call(
        functools.partial(_mixer_short_kernel, n_heads=n_heads, head_dim=head_dim),
        out_shape=(jax.ShapeDtypeStruct((nseq, ts, rw + gwid), _F32),
                   jax.ShapeDtypeStruct((nseq, 1, r_cols), _F32),
                   jax.ShapeDtypeStruct((nseq, n_heads, head_dim, head_dim), _F32),
                   jax.ShapeDtypeStruct((nseq, ts, gwid), _F32)),
        grid=(nseq // nb, 1),
        in_specs=[seqs(ts, pcols), seqs(1, r_cols), seqs(n_heads, head_dim, head_dim)]
                 + [_full_spec(a) for a in params],
        out_specs=(seqs(ts, rw + gwid), seqs(1, r_cols), seqs(n_heads, head_dim, head_dim), seqs(ts, gwid)),
        compiler_params=pltpu.CompilerParams(dimension_semantics=("arbitrary", "arbitrary"),
                                             vmem_limit_bytes=_VMEM_LIMIT),
        name="mixers_state",
    )(p, shift0, wkv0, *params)


def _pick_tile(n, pref):
    t = min(n, pref)
    while n % t:
        t //= 2
    return t


def kernel(x_prompt, x_sample, state_shift, state_wkv, c_prompt, c_sample, w_ada, b_ada, ffn1_gu, ffn1_dn, w_in, mu_shift, w0, w_lora_up, a0, a_lora_up, g_lora_up, k_k, k_a, r_k, gn_w, gn_b, ln_v_g, ln_v_b, w_s, b_s, w_out, ffn2_gu, ffn2_dn, final_g):
    depth = w_ada.shape[0]
    bp, tp, d = x_prompt.shape
    bs, ts, _ = x_sample.shape
    n_heads, head_dim = r_k.shape[1], r_k.shape[2]
    rw = n_heads * head_dim
    w_rank, a_rank = w_lora_up.shape[1], a_lora_up.shape[1]
    assert w_rank == a_rank and n_heads % HEAD_GROUP == 0 and head_dim == WKV_CHUNK
    d_ff = ffn1_dn.shape[1]
    tf = 256 if d_ff % 256 == 0 else 128

    xp = x_prompt.reshape(bp * tp, d)
    xs = x_sample.reshape(bs * ts, d)
    assert bs % SUBLANES == 0
    n_c = bs + bp
    n_c_pad = -(-n_c // 16) * 16
    c_all = jnp.concatenate([c_sample, c_prompt, jnp.zeros((n_c_pad - n_c, d), _F32)], axis=0)

    tm_p = _pick_tile(tp, 512)
    tr_p = _pick_tile(tp, 256)
    tm_s = ts * bs
    nb_p = _pick_tile(bp, 2)

    shp, wkp, shs, wks, cvs = [], [], [], [], []
    for l in range(depth):
        mod = _mod_call(c_all, w_ada[l], b_ada[l])
        tiles_p = dict(tm=tm_p, seq0=bs, seqs_per_tile=1, tiles_per_seq=tp // tm_p, tf=tf)
        tiles_s = dict(tm=tm_s, seq0=0, seqs_per_tile=bs, tiles_per_seq=1, tf=tf)

        wgu1, wdn1 = ffn1_gu[l].astype(_BF16), ffn1_dn[l].astype(_BF16)
        wgu2, wdn2 = ffn2_gu[l].astype(_BF16), ffn2_dn[l].astype(_BF16)
        win, wout = w_in[l].astype(_BF16), w_out[l].astype(_BF16)
        zw = jnp.zeros((w_rank, rw), _F32)
        wwa = jnp.concatenate([jnp.concatenate([w_lora_up[l], zw], axis=1),
                               jnp.concatenate([zw, a_lora_up[l]], axis=1)], axis=0).astype(_BF16)
        row = lambda a: a.reshape(1, -1)
        prm = dict(mu=row(mu_shift[l]), w0=row(w0[l]), a0=row(a0[l]), wwa=wwa,
                   wg=g_lora_up[l].astype(_BF16), k_k=row(k_k[l]), k_a=row(k_a[l]), r_k=row(r_k[l]),
                   gn_w=row(gn_w[l]), gn_b=row(gn_b[l]), lng=row(ln_v_g[l]), lnb=row(ln_v_b[l]),
                   w_s=w_s[l], b_s=b_s[l])
        fg = final_g.reshape(1, d)
        final = l == depth - 1

        hp, pp = _ffn_in_call(xp, mod, wgu1, wdn1, win, **tiles_p)
        ymp, sh_p, wk_p = _mixer_call(pp.reshape(bp, tp, -1), prm, nb=nb_p, tr=tr_p,
                                      n_heads=n_heads, head_dim=head_dim)
        xp = _out_ffn_call(ymp.reshape(bp * tp, -1), hp, mod, wout, wgu2, wdn2, fg, final=final, **tiles_p)

        hs, ps = _ffn_in_call(xs, mod, wgu1, wdn1, win, **tiles_s)
        yms, sh_s, wk_s, v_s = _mixer_short_call(ps.reshape(bs, ts, -1), state_shift[l][:, None, :], state_wkv[l], prm,
                                                 n_heads=n_heads, head_dim=head_dim)
        xs = _out_ffn_call(yms.reshape(bs * ts, -1), hs, mod, wout, wgu2, wdn2, fg, final=final, **tiles_s)

        shp.append(sh_p[:, 0]); wkp.append(wk_p); shs.append(sh_s[:, 0]); wks.append(wk_s); cvs.append(v_s)

    y_prompt = xp.reshape(bp, tp, d)
    y_sample = xs.reshape(bs, ts, d)
    return (y_prompt, y_sample, jnp.stack(shp), jnp.stack(wkp), jnp.stack(shs), jnp.stack(wks), jnp.stack(cvs))
```

```python
import functools
import math

import jax
import jax.numpy as jnp
import numpy as np
from jax import lax
from jax.experimental import pallas as pl
from jax.experimental.pallas import tpu as pltpu

_F32 = jnp.float32
_BF16 = jnp.bfloat16

RMS_EPS = 1e-6
LN_EPS = 1e-5
GN_EPS = 64e-5
N_MOD = 9
WKV_CHUNK = 64
HEAD_GROUP = 4
INV_BLOCK = 16
GMLP_CHUNK = 128
SEG_ROWS = 256
SUBLANES = 8
MOD_PER_STEP = 3
WEIGHT_STAGE_STEPS = 16
DECAY_SCALE = math.exp(-0.5)

_V7X_VMEM_BYTES = 64 * 1024 * 1024
_VMEM_LIMIT = _V7X_VMEM_BYTES - 8 * 1024 * 1024


def _dot(a, b):
    return jnp.dot(a.astype(_BF16), b.astype(_BF16), preferred_element_type=_F32)


def _dot_nt(a, b):
    return lax.dot_general(a.astype(_BF16), b.astype(_BF16), (((1,), (1,)), ((), ())),
                           preferred_element_type=_F32)


def _dot_tn(a, b):
    return lax.dot_general(a.astype(_BF16), b.astype(_BF16), (((0,), (0,)), ((), ())),
                           preferred_element_type=_F32)


def _split_hi_lo(x):
    hi = x.astype(_BF16)
    lo = (x - hi.astype(_F32)).astype(_BF16)
    return hi, lo


def _dot_sel_lhs(sel, x):
    hi, lo = _split_hi_lo(x)
    return (jnp.dot(sel, hi, preferred_element_type=_F32)
            + jnp.dot(sel, lo, preferred_element_type=_F32))


def _div_pow2(x, n):
    assert n & (n - 1) == 0
    return lax.shift_right_logical(x, jnp.int32(n.bit_length() - 1))


def _mod_pow2(x, n):
    assert n & (n - 1) == 0
    return lax.bitwise_and(x, jnp.int32(n - 1))


def _rms(x):
    return x * lax.rsqrt(jnp.mean(x * x, -1, keepdims=True) + RMS_EPS)


def _sigmoid(x):
    return 1.0 / (1.0 + jnp.exp(-x))


def _affine(x, scale, shift=None):
    mb = scale.shape[0]
    if mb == 1:
        y = x * scale
        return y if shift is None else y + shift
    tm, d = x.shape
    y = x.reshape(mb, tm // mb, d) * scale[:, None, :]
    if shift is not None:
        y = y + shift[:, None, :]
    return y.reshape(tm, d)


def _swiglu(n_bf, wgu_ref, wdn_ref, tf):
    d_ff = wdn_ref.shape[0]
    acc = None
    for j in range(d_ff // tf):
        g = jnp.dot(n_bf, wgu_ref[:, j * tf:(j + 1) * tf], preferred_element_type=_F32)
        u = jnp.dot(n_bf, wgu_ref[:, d_ff + j * tf:d_ff + (j + 1) * tf], preferred_element_type=_F32)
        hm = (g * _sigmoid(g) * u).astype(_BF16)
        part = jnp.dot(hm, wdn_ref[j * tf:(j + 1) * tf, :], preferred_element_type=_F32)
        acc = part if acc is None else acc + part
    return acc


def _mod_kernel(c_ref, w_ref, b_ref, o_ref):
    c = c_ref[...]
    s = c * _sigmoid(c)
    o = _dot(s, w_ref[...]) + b_ref[...]
    d = c.shape[1]
    for k in range(o_ref.shape[0]):
        o_ref[k] = o[:, k * d:(k + 1) * d]


def _mod_call(c_all, w_ada, b_ada):
    n, d = c_all.shape
    per_step = MOD_PER_STEP if N_MOD % MOD_PER_STEP == 0 else 1
    return pl.pallas_call(
        _mod_kernel,
        out_shape=jax.ShapeDtypeStruct((N_MOD, n, d), _F32),
        grid=(N_MOD // per_step,),
        in_specs=[pl.BlockSpec((n, d), lambda j: (0, 0)),
                  pl.BlockSpec((d, per_step * d), lambda j: (0, j)),
                  pl.BlockSpec((1, per_step * d), lambda j: (0, j))],
        out_specs=pl.BlockSpec((per_step, n, d), lambda j: (j, 0, 0)),
        compiler_params=pltpu.CompilerParams(dimension_semantics=("arbitrary",),
                                             vmem_limit_bytes=_VMEM_LIMIT),
        name="adaln_mod",
    )(c_all, w_ada, b_ada.reshape(1, N_MOD * d))


def _mod_rows(mod_ref, pick, tile):
    if pick is None:
        return lambda j: mod_ref[j]
    seq0, tiles_per_seq = pick
    row = lax.rem(seq0 + lax.div(tile, jnp.int32(tiles_per_seq)), jnp.int32(SUBLANES))
    return lambda j: mod_ref[j, pl.ds(row, 1), :]


def _mod_spec(mod, seq0, seqs_per_tile, tiles_per_seq, tile):
    n_mod, _, d = mod.shape
    if seqs_per_tile > 1:
        assert seqs_per_tile % SUBLANES == 0 and seq0 % seqs_per_tile == 0 and tiles_per_seq == 1
        first = seq0 // seqs_per_tile
        return pl.BlockSpec((n_mod, seqs_per_tile, d), lambda i: (0, first + tile(i), 0)), None
    return (pl.BlockSpec((n_mod, SUBLANES, d), lambda i: (0, (seq0 + tile(i) // tiles_per_seq) // SUBLANES, 0)),
            (seq0, tiles_per_seq))


def _resident(shape):
    nd = len(shape)
    return pl.BlockSpec(shape, lambda *_: (0,) * nd, pipeline_mode=pl.Buffered(1))


def _dense_call(body, name, acts, mod, weights, extra, out_widths, *, tm, seq0, seqs_per_tile, tiles_per_seq,
                stage):
    n = acts[0].shape[0]
    tile = (lambda i: jnp.maximum(i - stage, 0)) if stage else (lambda i: i)
    mod_spec, pick = _mod_spec(mod, seq0, seqs_per_tile, tiles_per_seq, tile)
    rows = lambda width: pl.BlockSpec((tm, width), lambda i: (tile(i), 0))
    n_act, n_w, n_x, n_out = len(acts), len(weights), len(extra), len(out_widths)

    def kernel(*refs):
        it = iter(refs)
        act_refs = [next(it) for _ in range(n_act)]
        mod_ref = next(it)
        w_in_refs = [next(it) for _ in range(n_w)]
        x_refs = [next(it) for _ in range(n_x)]
        out_refs = [next(it) for _ in range(n_out)]
        step = pl.program_id(0)

        def run(w_refs):
            t = step - stage
            body(act_refs, _mod_rows(mod_ref, pick, t), w_refs, x_refs, out_refs)

        if not stage:
            run(w_in_refs)
            return
        w_out_refs = [next(it) for _ in range(n_w)]
        w_bf_refs = [next(it) for _ in range(n_w)]

        @pl.when(step < stage)
        def _():
            for src, dst, out in zip(w_in_refs, w_bf_refs, w_out_refs):
                chunk = src.shape[0]
                v = src[...].astype(_BF16)
                dst[pl.ds(pl.multiple_of(step * chunk, chunk), chunk), :] = v
                out[...] = v

        pl.when(step >= stage)(lambda: run(w_bf_refs))

    out_shape = [jax.ShapeDtypeStruct((n, w), _F32) for w in out_widths]
    out_specs = [rows(w) for w in out_widths]
    scratch = []
    if stage:
        assert all(w.shape[0] % (stage * 2 * SUBLANES) == 0 for w in weights)
        chunk_spec = lambda w: pl.BlockSpec((w.shape[0] // stage, w.shape[1]),
                                            lambda i: (jnp.minimum(i, stage - 1), 0))
        w_specs = [chunk_spec(w) for w in weights]
        out_shape += [jax.ShapeDtypeStruct(w.shape, _BF16) for w in weights]
        out_specs += [chunk_spec(w) for w in weights]
        scratch = [pltpu.VMEM(w.shape, _BF16) for w in weights]
    else:
        w_specs = [_resident(w.shape) for w in weights]
    return pl.pallas_call(
        kernel,
        out_shape=tuple(out_shape),
        grid=(stage + n // tm,),
        in_specs=[rows(a.shape[1]) for a in acts] + [mod_spec] + w_specs + [_resident(e.shape) for e in extra],
        out_specs=tuple(out_specs),
        scratch_shapes=scratch,
        compiler_params=pltpu.CompilerParams(dimension_semantics=("arbitrary",),
                                             vmem_limit_bytes=_VMEM_LIMIT),
        name=name,
    )(*acts, mod, *weights, *extra)


def _ffn_in_body(acts, mod, weights, extra, outs, *, tf):
    (x_ref,), (wgu_ref, wdn_ref, win_ref), (h_ref, p_ref) = acts, weights, outs
    x = x_ref[...]
    sh1, sc1, g1, sh2, sc2 = (mod(j) for j in range(5))
    n1 = _affine(_rms(x), 1.0 + sc1, sh1).astype(_BF16)
    h = x + _affine(_swiglu(n1, wgu_ref, wdn_ref, tf), 0.5 * g1)
    h_ref[...] = h
    n2 = _affine(_rms(h), 1.0 + sc2, sh2).astype(_BF16)
    p_ref[...] = jnp.dot(n2, win_ref[...], preferred_element_type=_F32)


def _ffn_in_call(x, mod, wgu, wdn, win, *, tf, **tiling):
    return _dense_call(functools.partial(_ffn_in_body, tf=tf), "ffn1_inproj", [x], mod, [wgu, wdn, win], [],
                       [x.shape[1], win.shape[1]], **tiling)


def _out_ffn_body(acts, mod, weights, extra, outs, *, tf, final):
    (ym_ref, h_ref), (wout_ref, wgu_ref, wdn_ref), (fg_ref,), (o_ref,) = acts, weights, extra, outs
    g2, sh3, sc3, g3 = (mod(j) for j in range(5, 9))
    h = h_ref[...] + _affine(jnp.dot(ym_ref[...].astype(_BF16), wout_ref[...],
                                     preferred_element_type=_F32), g2)
    n3 = _affine(_rms(h), 1.0 + sc3, sh3).astype(_BF16)
    h = h + _affine(_swiglu(n3, wgu_ref, wdn_ref, tf), 0.5 * g3)
    if final:
        h = _rms(h) * fg_ref[...]
    o_ref[...] = h


def _out_ffn_call(ym, h, mod, wout, wgu, wdn, fg, *, tf, final, **tiling):
    return _dense_call(functools.partial(_out_ffn_body, tf=tf, final=final), "outproj_ffn2", [ym, h], mod,
                       [wout, wgu, wdn], [fg], [h.shape[1]], **tiling)


def _block_diag(y, mask_bf):
    reps = mask_bf.shape[0] // y.shape[0]
    return jnp.concatenate([y.astype(_BF16)] * reps, axis=0) * mask_bf


def _nilpotent_inverse(n, bd, order):
    every = range(len(n))
    c = n[0].shape[0]
    tm = n
    levels = order.bit_length() - 2
    if levels <= 0:
        return tm
    nb = bd(n)
    npow = [_dot(n[i], nb[i]) for i in every]
    for lvl in range(levels):
        nb = bd(npow)
        if lvl + 1 < levels:
            res = [_dot(jnp.concatenate([tm[i], npow[i]], axis=0), nb[i]) for i in every]
            tm = [tm[i] + npow[i] + res[i][:c] for i in every]
            npow = [res[i][c:] for i in every]
        else:
            tm = [tm[i] + npow[i] + _dot(tm[i], nb[i]) for i in every]
    return tm


def _wkv_local(rt, at, bt, kt, vv, mask_bf, strict, incl, diag):
    n = len(rt)
    c = rt[0].shape[0]
    every = range(n)
    bd = lambda xs: [_block_diag(x, mask_bf) for x in xs]
    x = [jnp.concatenate([at[i], rt[i]], axis=0) for i in every]
    bb, kb = bd(bt), bd(kt)
    xb = [_dot_nt(x[i], bb[i]) for i in every]
    xk = [_dot_nt(x[i], kb[i]) for i in every]
    lab = [xb[i][:c] * strict for i in every]
    mrb = [xb[i][c:] * incl for i in every]
    lak = [xk[i][:c] * strict for i in every]
    mrk = [xk[i][c:] * incl for i in every]
    vb = bd(vv)
    wy = [_dot(jnp.concatenate([lak[i], mrk[i]], axis=0), vb[i]) for i in every]
    w1 = [wy[i][:c] for i in every]
    y0 = [wy[i][c:] for i in every]
    if isinstance(diag, int):
        tm = _nilpotent_inverse(lab, bd, diag)
    else:
        assert c == 4 * INV_BLOCK
        nd = [lab[i] * diag for i in every]
        noff = [lab[i] - nd[i] for i in every]
        tdm = _nilpotent_inverse(nd, bd, INV_BLOCK)
        noffb = bd(noff)
        m1 = [noff[i] + _dot(tdm[i], noffb[i]) for i in every]
        m1b = bd(m1)
        m2 = [_dot(m1[i], m1b[i]) for i in every]
        m2b = bd(m2)
        q = [m1[i] + m2[i] + _dot(m1[i], m2b[i]) for i in every]
        tdmb = bd(tdm)
        tm = [q[i] + tdm[i] + _dot(q[i], tdmb[i]) for i in every]
    atb = bd(at)
    ap = [at[i] + _dot(tm[i], atb[i]) for i in every]
    w1b = bd(w1)
    u0 = [w1[i] + _dot(tm[i], w1b[i]) for i in every]
    return [(ap[i], u0[i], mrb[i], y0[i]) for i in every]


def _wkv_state_step(ap, rt, u0, vv, bdec, kdec, pc, st, mask):
    every = range(len(rt))
    c = rt[0].shape[0]
    ar = [_dot_nt(jnp.concatenate([ap[i], rt[i]], axis=0), st[i]) for i in every]
    u = [ar[i][:c] + u0[i] for i in every]
    upd = [_dot_tn(jnp.concatenate([u[i], vv[i]], axis=0), jnp.concatenate([bdec[i], kdec[i]], axis=0))
           for i in every]
    st_new = [st[i] * pc[i] + upd[i] * mask for i in every]
    return u, [ar[i][c:] for i in every], st_new


def _token_features(p, first, trow, prm, segsum, rw):
    mu_ref, w0_ref, a0_ref, wwa_ref, wg_ref, kk_ref, ka_ref = prm
    r_cols = mu_ref.shape[1]
    pr = p[:, :r_cols]
    prev = jnp.where(trow == 0, first, pltpu.roll(pr, 1, 0))
    xm = pr + (prev - pr) * mu_ref[...]
    r = xm[:, 0:rw]
    k = xm[:, rw:2 * rw]
    v = xm[:, 2 * rw:3 * rw]
    lo_w = wwa_ref.shape[0]
    x_lo = xm[:, 3 * rw:3 * rw + lo_w]
    gd = xm[:, 3 * rw + lo_w:r_cols]
    lane = lax.broadcasted_iota(jnp.int32, (1, lo_w), 1)
    x_lo = jnp.where(lane < lo_w // 2, jnp.tanh(x_lo), x_lo)
    wa = _dot(x_lo, wwa_ref[...])
    wpre = w0_ref[...] + wa[:, :rw]
    lw = -DECAY_SCALE * _sigmoid(wpre)
    a = _sigmoid(a0_ref[...] + wa[:, rw:])
    gate = _dot(_sigmoid(gd), wg_ref[...])
    kkv = k * kk_ref[...]
    kkn = kkv * lax.rsqrt(jnp.maximum(segsum(kkv * kkv), 1e-24))
    k2 = k * (1.0 + (a - 1.0) * ka_ref[...])
    return dict(r=r, v=v, lw=lw, a=a, gate=gate, kkn=kkn, k2=k2)


def _decay_factors(f, cum, tot):
    pinc = jnp.exp(cum)
    pinv = jnp.exp(-cum)
    pend = jnp.exp(tot - cum)
    beta = f["kkn"] * f["a"]
    return dict(rt=f["r"] * pinc, at=-f["kkn"] * jnp.exp(cum - f["lw"]), bt=beta * pinv, kt=f["k2"] * pinv,
                bdec=beta * pend, kdec=f["k2"] * pend, pinc=pinc)


def _rwkv_output(yw, f, rk_ref, gnw_ref, gnb_ref, segsum, head_dim):
    inv_n = 1.0 / head_dim
    mean = segsum(yw) * inv_n
    dev = yw - mean
    var = segsum(dev * dev) * inv_n
    yn = dev * lax.rsqrt(var + GN_EPS) * gnw_ref[...] + gnb_ref[...]
    bonus = segsum(f["r"] * f["k2"] * rk_ref[...]) * f["v"]
    return (yn + bonus) * f["gate"]


def _gmlp_branch(p, r_cols, lng_ref, lnb_ref, ws2_ref, tril2, bsf_ref):
    gwid = lng_ref.shape[1]
    ct = ws2_ref.shape[1]
    rows = p.shape[0]
    pu = p[:, r_cols:r_cols + gwid]
    pv = p[:, r_cols + gwid:r_cols + 2 * gwid]
    pm = jnp.mean(pv, -1, keepdims=True)
    pd = pv - pm
    pvar = jnp.mean(pd * pd, -1, keepdims=True)
    vn = pd * lax.rsqrt(pvar + LN_EPS) * lng_ref[...] + lnb_ref[...]
    n_pair = ws2_ref.shape[0]
    pw = gwid // n_pair
    first_half = lax.broadcasted_iota(jnp.int32, (1, pw), 1) < pw // 2
    ws_tril = [(ws2_ref[j] * tril2).astype(_BF16) for j in range(n_pair)]
    mixed_rows = []
    for q in range(rows // ct):
        outs = []
        for j in range(n_pair):
            vp = vn[q * ct:(q + 1) * ct, j * pw:(j + 1) * pw]
            rhs = jnp.concatenate([jnp.where(first_half, vp, 0.0), jnp.where(first_half, 0.0, vp)], axis=0)
            outs.append(_dot(ws_tril[j], rhs))
        mixed_rows.append(jnp.concatenate(outs, axis=1) + bsf_ref[...])
    mixed = mixed_rows[0] if len(mixed_rows) == 1 else jnp.concatenate(mixed_rows, axis=0)
    return pu * mixed, vn


def _state_from_heads(wkv_ref, s, g, head_dim):
    zero = jnp.zeros((head_dim, head_dim), _F32)
    blocks = []
    for h in range(HEAD_GROUP):
        blk = wkv_ref[s, g * HEAD_GROUP + h]
        blocks.append(jnp.concatenate([zero] * h + [blk] + [zero] * (HEAD_GROUP - 1 - h), axis=1))
    return jnp.concatenate(blocks, axis=0)


def _state_to_heads(wkv_ref, s, g, st, head_dim):
    for h in range(HEAD_GROUP):
        wkv_ref[s, g * HEAD_GROUP + h] = st[h * head_dim:(h + 1) * head_dim, h * head_dim:(h + 1) * head_dim]


def _mixer_kernel(p_ref, mu_ref, w0_ref, a0_ref, wwa_ref, wg_ref, kk_ref, ka_ref, rk_ref, gnw_ref, gnb_ref,
                  lng_ref, lnb_ref, ws2_ref, bsf_ref, hmask_ref, hmaskb_ref, tmask_ref, segm_ref, tril2_ref,
                  y_ref, shift_ref, wkv_ref, st_s, carry_s, *, n_heads, head_dim):
    nb, tr, _ = p_ref.shape
    c = WKV_CHUNK
    gw = HEAD_GROUP * head_dim
    n_groups = n_heads // HEAD_GROUP
    rw = n_heads * head_dim
    rows = nb * tr
    ti = pl.program_id(1)
    n_t = pl.num_programs(1)
    r_cols = mu_ref.shape[1]

    @pl.when(ti == 0)
    def _():
        carry_s[...] = jnp.zeros_like(carry_s)
        st_s[...] = jnp.zeros_like(st_s)

    mask = hmask_ref[...]
    mask_bf = hmaskb_ref[...]
    strict, incl, diag = tmask_ref[0], tmask_ref[1], tmask_ref[2]
    tri_bf = segm_ref[...]

    def segsum(x):
        return jnp.concatenate([_dot(x[:, g * gw:(g + 1) * gw], mask_bf) for g in range(n_groups)], axis=1)

    p = p_ref[0] if nb == 1 else jnp.concatenate([p_ref[s] for s in range(nb)], axis=0)
    trow = _mod_pow2(lax.broadcasted_iota(jnp.int32, (rows, 1), 0), tr)
    if nb == 1:
        first = carry_s[0]
    else:
        first = jnp.concatenate([jnp.broadcast_to(carry_s[s], (tr, r_cols)) for s in range(nb)], axis=0)
    for s in range(nb):
        carry_s[s] = p_ref[s, tr - 1:tr, :r_cols]
    f = _token_features(p, first, trow, (mu_ref, w0_ref, a0_ref, wwa_ref, wg_ref, kk_ref, ka_ref), segsum, rw)
    sr = tri_bf.shape[0]
    cum = jnp.concatenate([_dot_sel_lhs(tri_bf, f["lw"][i * sr:(i + 1) * sr]) for i in range(rows // sr)],
                          axis=0)
    tot = jnp.concatenate([jnp.broadcast_to(cum[j * c + c - 1:j * c + c], (c, rw)) for j in range(rows // c)],
                          axis=0)
    d = _decay_factors(f, cum, tot)

    n_chunks = rows // c
    chunks_per_seq = tr // c

    def blk(x, j, g):
        return x[j * c:(j + 1) * c, g * gw:(g + 1) * gw]

    probs = [(j, g) for j in range(n_chunks) for g in range(n_groups)]
    cut = lambda x, ps: [blk(x, j, g) for j, g in ps]
    local = dict(zip(probs, _wkv_local(cut(d["rt"], probs), cut(d["at"], probs), cut(d["bt"], probs),
                                       cut(d["kt"], probs), cut(f["v"], probs), mask_bf, strict, incl, diag)))
    y_blk = {}
    for jj in range(chunks_per_seq):
        ps = [(s * chunks_per_seq + jj, g) for s in range(nb) for g in range(n_groups)]
        loc = [local[pr_] for pr_ in ps]
        sts = [st_s[j // chunks_per_seq, g] for j, g in ps]
        pcs = [blk(d["pinc"], j, g)[c - 1:c] for j, g in ps]
        u, ys, st_new = _wkv_state_step([l[0] for l in loc], cut(d["rt"], ps), [l[1] for l in loc],
                                        cut(f["v"], ps), cut(d["bdec"], ps), cut(d["kdec"], ps), pcs, sts, mask)
        for i, (j, g) in enumerate(ps):
            st_s[j // chunks_per_seq, g] = st_new[i]
            y_blk[j, g] = ys[i] + _dot(loc[i][2], _block_diag(u[i], mask_bf)) + loc[i][3]
    y_rows = [jnp.concatenate([y_blk[j, g] for g in range(n_groups)], axis=1) for j in range(n_chunks)]
    yw = y_rows[0] if n_chunks == 1 else jnp.concatenate(y_rows, axis=0)

    y_r = _rwkv_output(yw, f, rk_ref, gnw_ref, gnb_ref, segsum, head_dim)
    y_c, _ = _gmlp_branch(p, r_cols, lng_ref, lnb_ref, ws2_ref, tril2_ref[...], bsf_ref)
    y_all = jnp.concatenate([y_r, y_c], axis=1)
    for s in range(nb):
        y_ref[s] = y_all[s * tr:(s + 1) * tr]

    @pl.when(ti == n_t - 1)
    def _():
        shift_ref[...] = carry_s[...]
        for s in range(nb):
            for g in range(n_groups):
                _state_to_heads(wkv_ref, s, g, st_s[s, g], head_dim)


def _mixer_short_kernel(p_ref, shift0_ref, wkv0_ref, mu_ref, w0_ref, a0_ref, wwa_ref, wg_ref, kk_ref, ka_ref,
                        rk_ref, gnw_ref, gnb_ref, lng_ref, lnb_ref, ws2_ref, bsf_ref, hmask_ref, hmaskb_ref,
                        tmask_ref, segm_ref, tril2_ref, y_ref, shift_ref, wkv_ref, vn_ref, *, n_heads, head_dim):
    nb, ts, _ = p_ref.shape
    c = WKV_CHUNK
    assert nb * ts == c
    gw = HEAD_GROUP * head_dim
    n_groups = n_heads // HEAD_GROUP
    rw = n_heads * head_dim
    r_cols = mu_ref.shape[1]

    mask = hmask_ref[...]
    mask_bf = hmaskb_ref[...]
    strict, incl = tmask_ref[0], tmask_ref[1]

    def segsum(x):
        return jnp.concatenate([_dot(x[:, g * gw:(g + 1) * gw], mask_bf) for g in range(n_groups)], axis=1)

    p = p_ref[...].reshape(c, p_ref.shape[2])
    trow = _mod_pow2(lax.broadcasted_iota(jnp.int32, (c, 1), 0), ts)
    first = jnp.concatenate([jnp.broadcast_to(shift0_ref[s], (ts, r_cols)) for s in range(nb)], axis=0)
    for s in range(nb):
        shift_ref[s] = p_ref[s, ts - 1:ts, :r_cols]
    f = _token_features(p, first, trow, (mu_ref, w0_ref, a0_ref, wwa_ref, wg_ref, kk_ref, ka_ref), segsum, rw)
    cum = _dot_sel_lhs(segm_ref[...], f["lw"])
    tot = jnp.concatenate([jnp.broadcast_to(cum[s * ts + ts - 1:(s + 1) * ts], (ts, rw)) for s in range(nb)],
                          axis=0)
    d = _decay_factors(f, cum, tot)

    groups = range(n_groups)
    gcut = lambda x: [x[:, g * gw:(g + 1) * gw] for g in groups]
    local = _wkv_local(gcut(d["rt"]), gcut(d["at"]), gcut(d["bt"]), gcut(d["kt"]), gcut(f["v"]), mask_bf,
                       strict, incl, ts)

    ps = [(s, g) for s in range(nb) for g in groups]
    scut = lambda x: [x[s * ts:(s + 1) * ts, g * gw:(g + 1) * gw] for s, g in ps]
    lcut = lambda idx: [local[g][idx][s * ts:(s + 1) * ts] for s, g in ps]
    sts = [_state_from_heads(wkv0_ref, s, g, head_dim) for s, g in ps]
    pcs = [d["pinc"][s * ts + ts - 1:(s + 1) * ts, g * gw:(g + 1) * gw] for s, g in ps]
    u, ys, st_new = _wkv_state_step(lcut(0), scut(d["rt"]), lcut(1), scut(f["v"]), scut(d["bdec"]),
                                    scut(d["kdec"]), pcs, sts, mask)
    for i, (s, g) in enumerate(ps):
        _state_to_heads(wkv_ref, s, g, st_new[i], head_dim)
    rows_of = lambda xs, g: jnp.concatenate([xs[i] for i, (s, gg) in enumerate(ps) if gg == g], axis=0)
    yw = jnp.concatenate([rows_of(ys, g) + _dot(local[g][2], _block_diag(rows_of(u, g), mask_bf)) + local[g][3]
                          for g in groups], axis=1)

    y_r = _rwkv_output(yw, f, rk_ref, gnw_ref, gnb_ref, segsum, head_dim)
    y_c, vn = _gmlp_branch(p, r_cols, lng_ref, lnb_ref, ws2_ref, tril2_ref[...], bsf_ref)
    y_ref[...] = jnp.concatenate([y_r, y_c], axis=1).reshape(nb, ts, -1)
    vn_ref[...] = vn.reshape(nb, ts, -1)


def _mixer_masks(rows, seq, ct, head_dim):
    c, gw = WKV_CHUNK, HEAD_GROUP * head_dim
    blk = np.arange(gw) // head_dim
    hmask = (blk[:, None] == blk[None, :]).astype(np.float32)
    t = np.arange(c)[:, None]
    s = np.arange(HEAD_GROUP * c)[None, :] % c
    strict = (s < t) & (s // seq == t // seq)
    incl = (s <= t) & (s // seq == t // seq)
    tmask = np.stack([strict, incl, strict & (s // INV_BLOCK == t // INV_BLOCK)]).astype(np.float32)
    r = np.arange(rows)
    seg = min(c, seq)
    same = (r[:, None] // seg) == (r[None, :] // seg)
    segm = (same & (r[None, :] <= r[:, None])).astype(np.float32)
    ctm = np.arange(ct)
    tril1 = ((ctm[None, :] <= ctm[:, None]) & (ctm[None, :] // seq == ctm[:, None] // seq))
    tril2 = np.concatenate([tril1, tril1], axis=1).astype(np.float32)
    return [jnp.asarray(hmask), jnp.asarray(hmask, _BF16), jnp.asarray(tmask), jnp.asarray(segm, _BF16),
            jnp.asarray(tril2)]


def _gmlp_weights(w_s, b_s, ct, seq, gwid):
    hc = w_s.shape[0]
    n_seq = ct // min(seq, ct)
    ws = w_s[:, :ct // n_seq, :ct // n_seq]
    bs = b_s[:, :ct // n_seq]
    if n_seq > 1:
        ws = jnp.einsum("st,hij->hsitj", jnp.eye(n_seq, dtype=ws.dtype), ws).reshape(hc, ct, ct)
        bs = jnp.tile(bs, (1, n_seq))
    ws2 = ws.reshape(hc // 2, 2, ct, ct).transpose(0, 2, 1, 3).reshape(hc // 2, ct, 2 * ct)
    return ws2, jnp.repeat(bs.T, gwid // hc, axis=1)


def _mixer_params(prm, rows, seq, ct, head_dim):
    gwid = prm["lng"].shape[1]
    ws2, bsf = _gmlp_weights(prm["w_s"], prm["b_s"], ct, seq, gwid)
    params = [prm["mu"], prm["w0"], prm["a0"], prm["wwa"], prm["wg"], prm["k_k"], prm["k_a"], prm["r_k"],
              prm["gn_w"], prm["gn_b"], prm["lng"], prm["lnb"], ws2, bsf]
    return params + _mixer_masks(rows, seq, ct, head_dim)


def _full_spec(a):
    nd = a.ndim
    return pl.BlockSpec(a.shape, lambda b, i: (0,) * nd)


def _mixer_call(p, prm, *, nb, tr, n_heads, head_dim):
    nseq, t, pcols = p.shape
    rw = n_heads * head_dim
    r_cols = prm["mu"].shape[1]
    gwid = prm["lng"].shape[1]
    gw = HEAD_GROUP * head_dim
    n_groups = n_heads // HEAD_GROUP
    rows = nb * tr
    assert nseq % nb == 0 and t % tr == 0 and tr % GMLP_CHUNK == 0 and rows % SEG_ROWS == 0
    params = _mixer_params(prm, SEG_ROWS, t, GMLP_CHUNK, head_dim)
    return pl.pallas_call(
        functools.partial(_mixer_kernel, n_heads=n_heads, head_dim=head_dim),
        out_shape=(jax.ShapeDtypeStruct((nseq, t, rw + gwid), _F32),
                   jax.ShapeDtypeStruct((nseq, 1, r_cols), _F32),
                   jax.ShapeDtypeStruct((nseq, n_heads, head_dim, head_dim), _F32)),
        grid=(nseq // nb, t // tr),
        in_specs=[pl.BlockSpec((nb, tr, pcols), lambda b, i: (b, i, 0))] + [_full_spec(a) for a in params],
        out_specs=(pl.BlockSpec((nb, tr, rw + gwid), lambda b, i: (b, i, 0)),
                   pl.BlockSpec((nb, 1, r_cols), lambda b, i: (b, 0, 0)),
                   pl.BlockSpec((nb, n_heads, head_dim, head_dim), lambda b, i: (b, 0, 0, 0))),
        scratch_shapes=[pltpu.VMEM((nb, n_groups, gw, gw), _F32), pltpu.VMEM((nb, 1, r_cols), _F32)],
        compiler_params=pltpu.CompilerParams(dimension_semantics=("arbitrary", "arbitrary"),
                                             vmem_limit_bytes=_VMEM_LIMIT),
        name="mixers_fresh",
    )(p, *params)


def _mixer_short_call(p, shift0, wkv0, prm, *, n_heads, head_dim):
    nseq, ts, pcols = p.shape
    nb = WKV_CHUNK // ts
    assert nb * ts == WKV_CHUNK and ts % SUBLANES == 0 and nseq % nb == 0
    rw = n_heads * head_dim
    r_cols = prm["mu"].shape[1]
    gwid = prm["lng"].shape[1]
    params = _mixer_params(prm, WKV_CHUNK, ts, WKV_CHUNK, head_dim)
    seqs = lambda *tail: pl.BlockSpec((nb,) + tail, lambda b, i: (b,) + (0,) * len(tail))
    return pl.pallas_call(
        functools.partial(_mixer_short_kernel, n_heads=n_heads, head_dim=head_dim),
        out_shape=(jax.ShapeDtypeStruct((nseq, ts, rw + gwid), _F32),
                   jax.ShapeDtypeStruct((nseq, 1, r_cols), _F32),
                   jax.ShapeDtypeStruct((nseq, n_heads, head_dim, head_dim), _F32),
                   jax.ShapeDtypeStruct((nseq, ts, gwid), _F32)),
        grid=(nseq // nb, 1),
        in_specs=[seqs(ts, pcols), seqs(1, r_cols), seqs(n_heads, head_dim, head_dim)]
                 + [_full_spec(a) for a in params],
        out_specs=(seqs(ts, rw + gwid), seqs(1, r_cols), seqs(n_heads, head_dim, head_dim), seqs(ts, gwid)),
        compiler_params=pltpu.CompilerParams(dimension_semantics=("arbitrary", "arbitrary"),
                                             vmem_limit_bytes=_VMEM_LIMIT),
        name="mixers_state",
    )(p, shift0, wkv0, *params)


def _pick_tile(n, pref):
    t = min(n, pref)
    while n % t:
        t //= 2
    return t


def kernel(x_prompt, x_sample, state_shift, state_wkv, c_prompt, c_sample, w_ada, b_ada, ffn1_gu, ffn1_dn, w_in, mu_shift, w0, w_lora_up, a0, a_lora_up, g_lora_up, k_k, k_a, r_k, gn_w, gn_b, ln_v_g, ln_v_b, w_s, b_s, w_out, ffn2_gu, ffn2_dn, final_g):
    depth = w_ada.shape[0]
    bp, tp, d = x_prompt.shape
    bs, ts, _ = x_sample.shape
    n_heads, head_dim = r_k.shape[1], r_k.shape[2]
    rw = n_heads * head_dim
    w_rank, a_rank = w_lora_up.shape[1], a_lora_up.shape[1]
    assert w_rank == a_rank and n_heads % HEAD_GROUP == 0 and head_dim == WKV_CHUNK
    d_ff = ffn1_dn.shape[1]
    tf = 256 if d_ff % 256 == 0 else 128

    xp = x_prompt.reshape(bp * tp, d)
    xs = x_sample.reshape(bs * ts, d)
    assert bs % SUBLANES == 0
    n_c = bs + bp
    n_c_pad = -(-n_c // 16) * 16
    c_all = jnp.concatenate([c_sample, c_prompt, jnp.zeros((n_c_pad - n_c, d), _F32)], axis=0)

    tm_p = _pick_tile(tp, 512)
    tr_p = _pick_tile(tp, 256)
    tm_s = ts * bs
    nb_p = _pick_tile(bp, 2)

    shp, wkp, shs, wks, cvs = [], [], [], [], []
    for l in range(depth):
        mod = _mod_call(c_all, w_ada[l], b_ada[l])
        tiles_p = dict(tm=tm_p, seq0=bs, seqs_per_tile=1, tiles_per_seq=tp // tm_p, tf=tf, stage=WEIGHT_STAGE_STEPS)
        tiles_s = dict(tm=tm_s, seq0=0, seqs_per_tile=bs, tiles_per_seq=1, tf=tf, stage=0)

        zw = jnp.zeros((w_rank, rw), _F32)
        wwa = jnp.concatenate([jnp.concatenate([w_lora_up[l], zw], axis=1),
                               jnp.concatenate([zw, a_lora_up[l]], axis=1)], axis=0).astype(_BF16)
        row = lambda a: a.reshape(1, -1)
        prm = dict(mu=row(mu_shift[l]), w0=row(w0[l]), a0=row(a0[l]), wwa=wwa,
                   wg=g_lora_up[l].astype(_BF16), k_k=row(k_k[l]), k_a=row(k_a[l]), r_k=row(r_k[l]),
                   gn_w=row(gn_w[l]), gn_b=row(gn_b[l]), lng=row(ln_v_g[l]), lnb=row(ln_v_b[l]),
                   w_s=w_s[l], b_s=b_s[l])
        fg = final_g.reshape(1, d)
        final = l == depth - 1

        hp, pp, wgu1, wdn1, win = _ffn_in_call(xp, mod, ffn1_gu[l], ffn1_dn[l], w_in[l], **tiles_p)
        ymp, sh_p, wk_p = _mixer_call(pp.reshape(bp, tp, -1), prm, nb=nb_p, tr=tr_p,
                                      n_heads=n_heads, head_dim=head_dim)
        xp, wout, wgu2, wdn2 = _out_ffn_call(ymp.reshape(bp * tp, -1), hp, mod, w_out[l], ffn2_gu[l], ffn2_dn[l],
                                             fg, final=final, **tiles_p)

        hs, ps = _ffn_in_call(xs, mod, wgu1, wdn1, win, **tiles_s)
        yms, sh_s, wk_s, v_s = _mixer_short_call(ps.reshape(bs, ts, -1), state_shift[l][:, None, :], state_wkv[l], prm,
                                                 n_heads=n_heads, head_dim=head_dim)
        xs, = _out_ffn_call(yms.reshape(bs * ts, -1), hs, mod, wout, wgu2, wdn2, fg, final=final, **tiles_s)

        shp.append(sh_p[:, 0]); wkp.append(wk_p); shs.append(sh_s[:, 0]); wks.append(wk_s); cvs.append(v_s)

    y_prompt = xp.reshape(bp, tp, d)
    y_sample = xs.reshape(bs, ts, d)
    return (y_prompt, y_sample, jnp.stack(shp), jnp.stack(wkp), jnp.stack(shs), jnp.stack(wks), jnp.stack(cvs))
```

```python
import functools
import math

import jax
import jax.numpy as jnp
import numpy as np
from jax import lax
from jax.experimental import pallas as pl
from jax.experimental.pallas import tpu as pltpu

_F32 = jnp.float32
_BF16 = jnp.bfloat16

RMS_EPS = 1e-6
LN_EPS = 1e-5
GN_EPS = 64e-5
N_MOD = 9
WKV_CHUNK = 64
HEAD_GROUP = 4
INV_BLOCK = 16
GMLP_CHUNK = 128
SEG_ROWS = 256
SUBLANES = 8
MOD_PER_STEP = 3
WEIGHT_STAGE_STEPS = 16
DECAY_SCALE = math.exp(-0.5)

_V7X_VMEM_BYTES = 64 * 1024 * 1024
_VMEM_LIMIT = _V7X_VMEM_BYTES - 8 * 1024 * 1024


def _dot(a, b):
    return jnp.dot(a.astype(_BF16), b.astype(_BF16), preferred_element_type=_F32)


def _dot_nt(a, b):
    return lax.dot_general(a.astype(_BF16), b.astype(_BF16), (((1,), (1,)), ((), ())),
                           preferred_element_type=_F32)


def _dot_tn(a, b):
    return lax.dot_general(a.astype(_BF16), b.astype(_BF16), (((0,), (0,)), ((), ())),
                           preferred_element_type=_F32)


def _split_hi_lo(x):
    hi = x.astype(_BF16)
    lo = (x - hi.astype(_F32)).astype(_BF16)
    return hi, lo


def _dot_sel_lhs(sel, x):
    hi, lo = _split_hi_lo(x)
    return (jnp.dot(sel, hi, preferred_element_type=_F32)
            + jnp.dot(sel, lo, preferred_element_type=_F32))


def _div_pow2(x, n):
    assert n & (n - 1) == 0
    return lax.shift_right_logical(x, jnp.int32(n.bit_length() - 1))


def _mod_pow2(x, n):
    assert n & (n - 1) == 0
    return lax.bitwise_and(x, jnp.int32(n - 1))


def _rms(x):
    return x * lax.rsqrt(jnp.mean(x * x, -1, keepdims=True) + RMS_EPS)


def _sigmoid(x):
    return 1.0 / (1.0 + jnp.exp(-x))


def _affine(x, scale, shift=None):
    mb = scale.shape[0]
    if mb == 1:
        y = x * scale
        return y if shift is None else y + shift
    tm, d = x.shape
    y = x.reshape(mb, tm // mb, d) * scale[:, None, :]
    if shift is not None:
        y = y + shift[:, None, :]
    return y.reshape(tm, d)


def _swiglu(n_bf, wgu_ref, wdn_ref, tf):
    d_ff = wdn_ref.shape[0]
    acc = None
    for j in range(d_ff // tf):
        g = jnp.dot(n_bf, wgu_ref[:, j * tf:(j + 1) * tf], preferred_element_type=_F32)
        u = jnp.dot(n_bf, wgu_ref[:, d_ff + j * tf:d_ff + (j + 1) * tf], preferred_element_type=_F32)
        hm = (g * _sigmoid(g) * u).astype(_BF16)
        part = jnp.dot(hm, wdn_ref[j * tf:(j + 1) * tf, :], preferred_element_type=_F32)
        acc = part if acc is None else acc + part
    return acc


def _mod_kernel(c_ref, w_ref, b_ref, o_ref):
    c = c_ref[...]
    s = c * _sigmoid(c)
    o = _dot(s, w_ref[...]) + b_ref[...]
    d = c.shape[1]
    for k in range(o_ref.shape[0]):
        o_ref[k] = o[:, k * d:(k + 1) * d]


def _mod_call(c_all, w_ada, b_ada):
    n, d = c_all.shape
    per_step = MOD_PER_STEP if N_MOD % MOD_PER_STEP == 0 else 1
    return pl.pallas_call(
        _mod_kernel,
        out_shape=jax.ShapeDtypeStruct((N_MOD, n, d), _F32),
        grid=(N_MOD // per_step,),
        in_specs=[pl.BlockSpec((n, d), lambda j: (0, 0)),
                  pl.BlockSpec((d, per_step * d), lambda j: (0, j)),
                  pl.BlockSpec((1, per_step * d), lambda j: (0, j))],
        out_specs=pl.BlockSpec((per_step, n, d), lambda j: (j, 0, 0)),
        compiler_params=pltpu.CompilerParams(dimension_semantics=("arbitrary",),
                                             vmem_limit_bytes=_VMEM_LIMIT),
        name="adaln_mod",
    )(c_all, w_ada, b_ada.reshape(1, N_MOD * d))


def _mod_rows(mod_ref, pick, tile):
    if pick is None:
        return lambda j: mod_ref[j]
    seq0, tiles_per_seq = pick
    row = lax.rem(seq0 + lax.div(tile, jnp.int32(tiles_per_seq)), jnp.int32(SUBLANES))
    return lambda j: mod_ref[j, pl.ds(row, 1), :]


def _mod_spec(mod, seq0, seqs_per_tile, tiles_per_seq, tile):
    n_mod, _, d = mod.shape
    if seqs_per_tile > 1:
        assert seqs_per_tile % SUBLANES == 0 and seq0 % seqs_per_tile == 0 and tiles_per_seq == 1
        first = seq0 // seqs_per_tile
        return pl.BlockSpec((n_mod, seqs_per_tile, d), lambda i: (0, first + tile(i), 0)), None
    return (pl.BlockSpec((n_mod, SUBLANES, d), lambda i: (0, (seq0 + tile(i) // tiles_per_seq) // SUBLANES, 0)),
            (seq0, tiles_per_seq))


def _resident(shape):
    nd = len(shape)
    return pl.BlockSpec(shape, lambda *_: (0,) * nd, pipeline_mode=pl.Buffered(1))


def _dense_call(body, name, acts, mod, weights, extra, out_widths, *, tm, seq0, seqs_per_tile, tiles_per_seq,
                stage):
    n = acts[0].shape[0]
    tile = (lambda i: jnp.maximum(i - stage, 0)) if stage else (lambda i: i)
    mod_spec, pick = _mod_spec(mod, seq0, seqs_per_tile, tiles_per_seq, tile)
    rows = lambda width: pl.BlockSpec((tm, width), lambda i: (tile(i), 0))
    n_act, n_w, n_x, n_out = len(acts), len(weights), len(extra), len(out_widths)

    def kernel(*refs):
        it = iter(refs)
        act_refs = [next(it) for _ in range(n_act)]
        mod_ref = next(it)
        w_in_refs = [next(it) for _ in range(n_w)]
        x_refs = [next(it) for _ in range(n_x)]
        out_refs = [next(it) for _ in range(n_out)]
        step = pl.program_id(0)

        def run(w_refs):
            t = step - stage
            body(act_refs, _mod_rows(mod_ref, pick, t), w_refs, x_refs, out_refs)

        if not stage:
            run(w_in_refs)
            return
        w_out_refs = [next(it) for _ in range(n_w)]
        w_bf_refs = [next(it) for _ in range(n_w)]

        @pl.when(step < stage)
        def _():
            for src, dst, out in zip(w_in_refs, w_bf_refs, w_out_refs):
                chunk = src.shape[0]
                v = src[...].astype(_BF16)
                dst[pl.ds(pl.multiple_of(step * chunk, chunk), chunk), :] = v
                out[...] = v

        pl.when(step >= stage)(lambda: run(w_bf_refs))

    out_shape = [jax.ShapeDtypeStruct((n, w), _F32) for w in out_widths]
    out_specs = [rows(w) for w in out_widths]
    scratch = []
    if stage:
        assert all(w.shape[0] % (stage * 2 * SUBLANES) == 0 for w in weights)
        chunk_spec = lambda w: pl.BlockSpec((w.shape[0] // stage, w.shape[1]),
                                            lambda i: (jnp.minimum(i, stage - 1), 0))
        w_specs = [chunk_spec(w) for w in weights]
        out_shape += [jax.ShapeDtypeStruct(w.shape, _BF16) for w in weights]
        out_specs += [chunk_spec(w) for w in weights]
        scratch = [pltpu.VMEM(w.shape, _BF16) for w in weights]
    else:
        w_specs = [_resident(w.shape) for w in weights]
    return pl.pallas_call(
        kernel,
        out_shape=tuple(out_shape),
        grid=(stage + n // tm,),
        in_specs=[rows(a.shape[1]) for a in acts] + [mod_spec] + w_specs + [_resident(e.shape) for e in extra],
        out_specs=tuple(out_specs),
        scratch_shapes=scratch,
        compiler_params=pltpu.CompilerParams(dimension_semantics=("arbitrary",),
                                             vmem_limit_bytes=_VMEM_LIMIT),
        name=name,
    )(*acts, mod, *weights, *extra)


def _ffn_in_body(acts, mod, weights, extra, outs, *, tf):
    (x_ref,), (wgu_ref, wdn_ref, win_ref), (h_ref, p_ref) = acts, weights, outs
    x = x_ref[...]
    sh1, sc1, g1, sh2, sc2 = (mod(j) for j in range(5))
    n1 = _affine(_rms(x), 1.0 + sc1, sh1).astype(_BF16)
    h = x + _affine(_swiglu(n1, wgu_ref, wdn_ref, tf), 0.5 * g1)
    h_ref[...] = h
    n2 = _affine(_rms(h), 1.0 + sc2, sh2).astype(_BF16)
    p_ref[...] = jnp.dot(n2, win_ref[...], preferred_element_type=_F32)


def _ffn_in_call(x, mod, wgu, wdn, win, *, tf, **tiling):
    return _dense_call(functools.partial(_ffn_in_body, tf=tf), "ffn1_inproj", [x], mod, [wgu, wdn, win], [],
                       [x.shape[1], win.shape[1]], **tiling)


def _out_ffn_body(acts, mod, weights, extra, outs, *, tf, final):
    (ym_ref, h_ref), (wout_ref, wgu_ref, wdn_ref), (fg_ref,), (o_ref,) = acts, weights, extra, outs
    g2, sh3, sc3, g3 = (mod(j) for j in range(5, 9))
    h = h_ref[...] + _affine(jnp.dot(ym_ref[...].astype(_BF16), wout_ref[...],
                                     preferred_element_type=_F32), g2)
    n3 = _affine(_rms(h), 1.0 + sc3, sh3).astype(_BF16)
    h = h + _affine(_swiglu(n3, wgu_ref, wdn_ref, tf), 0.5 * g3)
    if final:
        h = _rms(h) * fg_ref[...]
    o_ref[...] = h


def _out_ffn_call(ym, h, mod, wout, wgu, wdn, fg, *, tf, final, **tiling):
    return _dense_call(functools.partial(_out_ffn_body, tf=tf, final=final), "outproj_ffn2", [ym, h], mod,
                       [wout, wgu, wdn], [fg], [h.shape[1]], **tiling)


def _block_diag(y, mask_bf):
    reps = mask_bf.shape[0] // y.shape[0]
    return jnp.concatenate([y.astype(_BF16)] * reps, axis=0) * mask_bf


def _nilpotent_inverse(n, bd, order):
    every = range(len(n))
    c = n[0].shape[0]
    tm = n
    levels = order.bit_length() - 2
    if levels <= 0:
        return tm
    nb = bd(n)
    npow = [_dot(n[i], nb[i]) for i in every]
    for lvl in range(levels):
        nb = bd(npow)
        if lvl + 1 < levels:
            res = [_dot(jnp.concatenate([tm[i], npow[i]], axis=0), nb[i]) for i in every]
            tm = [tm[i] + npow[i] + res[i][:c] for i in every]
            npow = [res[i][c:] for i in every]
        else:
            tm = [tm[i] + npow[i] + _dot(tm[i], nb[i]) for i in every]
    return tm


def _wkv_local(rt, at, bt, kt, vv, mask_bf, strict, incl, diag):
    n = len(rt)
    c = rt[0].shape[0]
    every = range(n)
    bd = lambda xs: [_block_diag(x, mask_bf) for x in xs]
    x = [jnp.concatenate([at[i], rt[i]], axis=0) for i in every]
    bb, kb = bd(bt), bd(kt)
    xb = [_dot_nt(x[i], bb[i]) for i in every]
    xk = [_dot_nt(x[i], kb[i]) for i in every]
    lab = [xb[i][:c] * strict for i in every]
    mrb = [xb[i][c:] * incl for i in every]
    lak = [xk[i][:c] * strict for i in every]
    mrk = [xk[i][c:] * incl for i in every]
    vb = bd(vv)
    wy = [_dot(jnp.concatenate([lak[i], mrk[i]], axis=0), vb[i]) for i in every]
    w1 = [wy[i][:c] for i in every]
    y0 = [wy[i][c:] for i in every]
    if isinstance(diag, int):
        tm = _nilpotent_inverse(lab, bd, diag)
    else:
        assert c == 4 * INV_BLOCK
        nd = [lab[i] * diag for i in every]
        noff = [lab[i] - nd[i] for i in every]
        tdm = _nilpotent_inverse(nd, bd, INV_BLOCK)
        noffb = bd(noff)
        m1 = [noff[i] + _dot(tdm[i], noffb[i]) for i in every]
        m1b = bd(m1)
        m2 = [_dot(m1[i], m1b[i]) for i in every]
        m2b = bd(m2)
        q = [m1[i] + m2[i] + _dot(m1[i], m2b[i]) for i in every]
        tdmb = bd(tdm)
        tm = [q[i] + tdm[i] + _dot(q[i], tdmb[i]) for i in every]
    atb = bd(at)
    ap = [at[i] + _dot(tm[i], atb[i]) for i in every]
    w1b = bd(w1)
    u0 = [w1[i] + _dot(tm[i], w1b[i]) for i in every]
    return [(ap[i], u0[i], mrb[i], y0[i]) for i in every]


def _wkv_state_step(ap, rt, u0, vv, bdec, kdec, pc, st, mask):
    every = range(len(rt))
    c = rt[0].shape[0]
    ar = [_dot_nt(jnp.concatenate([ap[i], rt[i]], axis=0), st[i]) for i in every]
    u = [ar[i][:c] + u0[i] for i in every]
    upd = [_dot_tn(jnp.concatenate([u[i], vv[i]], axis=0), jnp.concatenate([bdec[i], kdec[i]], axis=0))
           for i in every]
    st_new = [st[i] * pc[i] + upd[i] * mask for i in every]
    return u, [ar[i][c:] for i in every], st_new


def _token_features(p, first, trow, prm, segsum, rw):
    mu_ref, w0_ref, a0_ref, wwa_ref, wg_ref, kk_ref, ka_ref = prm
    r_cols = mu_ref.shape[1]
    pr = p[:, :r_cols]
    prev = jnp.where(trow == 0, first, pltpu.roll(pr, 1, 0))
    xm = pr + (prev - pr) * mu_ref[...]
    r = xm[:, 0:rw]
    k = xm[:, rw:2 * rw]
    v = xm[:, 2 * rw:3 * rw]
    lo_w = wwa_ref.shape[0]
    x_lo = xm[:, 3 * rw:3 * rw + lo_w]
    gd = xm[:, 3 * rw + lo_w:r_cols]
    lane = lax.broadcasted_iota(jnp.int32, (1, lo_w), 1)
    x_lo = jnp.where(lane < lo_w // 2, jnp.tanh(x_lo), x_lo)
    wa = _dot(x_lo, wwa_ref[...])
    wpre = w0_ref[...] + wa[:, :rw]
    lw = -DECAY_SCALE * _sigmoid(wpre)
    a = _sigmoid(a0_ref[...] + wa[:, rw:])
    gate = _dot(_sigmoid(gd), wg_ref[...])
    kkv = k * kk_ref[...]
    kkn = kkv * lax.rsqrt(jnp.maximum(segsum(kkv * kkv), 1e-24))
    k2 = k * (1.0 + (a - 1.0) * ka_ref[...])
    return dict(r=r, v=v, lw=lw, a=a, gate=gate, kkn=kkn, k2=k2)


def _decay_factors(f, cum, tot):
    pinc = jnp.exp(cum)
    pinv = jnp.exp(-cum)
    pend = jnp.exp(tot - cum)
    beta = f["kkn"] * f["a"]
    return dict(rt=f["r"] * pinc, at=-f["kkn"] * jnp.exp(cum - f["lw"]), bt=beta * pinv, kt=f["k2"] * pinv,
                bdec=beta * pend, kdec=f["k2"] * pend, pinc=pinc)


def _rwkv_output(yw, f, rk_ref, gnw_ref, gnb_ref, segsum, head_dim):
    inv_n = 1.0 / head_dim
    mean = segsum(yw) * inv_n
    dev = yw - mean
    var = segsum(dev * dev) * inv_n
    yn = dev * lax.rsqrt(var + GN_EPS) * gnw_ref[...] + gnb_ref[...]
    bonus = segsum(f["r"] * f["k2"] * rk_ref[...]) * f["v"]
    return (yn + bonus) * f["gate"]


def _gmlp_branch(p, r_cols, lng_ref, lnb_ref, ws2_ref, tril2, bsf_ref):
    gwid = lng_ref.shape[1]
    ct = ws2_ref.shape[1]
    rows = p.shape[0]
    pu = p[:, r_cols:r_cols + gwid]
    pv = p[:, r_cols + gwid:r_cols + 2 * gwid]
    pm = jnp.mean(pv, -1, keepdims=True)
    pd = pv - pm
    pvar = jnp.mean(pd * pd, -1, keepdims=True)
    vn = pd * lax.rsqrt(pvar + LN_EPS) * lng_ref[...] + lnb_ref[...]
    n_pair = ws2_ref.shape[0]
    pw = gwid // n_pair
    first_half = lax.broadcasted_iota(jnp.int32, (1, pw), 1) < pw // 2
    ws_tril = [(ws2_ref[j] * tril2).astype(_BF16) for j in range(n_pair)]
    mixed_rows = []
    for q in range(rows // ct):
        outs = []
        for j in range(n_pair):
            vp = vn[q * ct:(q + 1) * ct, j * pw:(j + 1) * pw]
            rhs = jnp.concatenate([jnp.where(first_half, vp, 0.0), jnp.where(first_half, 0.0, vp)], axis=0)
            outs.append(_dot(ws_tril[j], rhs))
        mixed_rows.append(jnp.concatenate(outs, axis=1) + bsf_ref[...])
    mixed = mixed_rows[0] if len(mixed_rows) == 1 else jnp.concatenate(mixed_rows, axis=0)
    return pu * mixed, vn


def _state_from_heads(wkv_ref, s, g, head_dim):
    zero = jnp.zeros((head_dim, head_dim), _F32)
    blocks = []
    for h in range(HEAD_GROUP):
        blk = wkv_ref[s, g * HEAD_GROUP + h]
        blocks.append(jnp.concatenate([zero] * h + [blk] + [zero] * (HEAD_GROUP - 1 - h), axis=1))
    return jnp.concatenate(blocks, axis=0)


def _state_to_heads(wkv_ref, s, g, st, head_dim):
    for h in range(HEAD_GROUP):
        wkv_ref[s, g * HEAD_GROUP + h] = st[h * head_dim:(h + 1) * head_dim, h * head_dim:(h + 1) * head_dim]


def _mixer_kernel(p_ref, mu_ref, w0_ref, a0_ref, wwa_ref, wg_ref, kk_ref, ka_ref, rk_ref, gnw_ref, gnb_ref,
                  lng_ref, lnb_ref, ws2_ref, bsf_ref, hmask_ref, hmaskb_ref, tmask_ref, segm_ref, tril2_ref,
                  y_ref, shift_ref, wkv_ref, st_s, carry_s, *, n_heads, head_dim):
    nb, tr, _ = p_ref.shape
    c = WKV_CHUNK
    gw = HEAD_GROUP * head_dim
    n_groups = n_heads // HEAD_GROUP
    rw = n_heads * head_dim
    rows = nb * tr
    ti = pl.program_id(1)
    n_t = pl.num_programs(1)
    r_cols = mu_ref.shape[1]

    @pl.when(ti == 0)
    def _():
        carry_s[...] = jnp.zeros_like(carry_s)
        st_s[...] = jnp.zeros_like(st_s)

    mask = hmask_ref[...]
    mask_bf = hmaskb_ref[...]
    strict, incl, diag = tmask_ref[0], tmask_ref[1], tmask_ref[2]
    tri_bf = segm_ref[...]

    def segsum(x):
        return jnp.concatenate([_dot(x[:, g * gw:(g + 1) * gw], mask_bf) for g in range(n_groups)], axis=1)

    p = p_ref[0] if nb == 1 else jnp.concatenate([p_ref[s] for s in range(nb)], axis=0)
    trow = _mod_pow2(lax.broadcasted_iota(jnp.int32, (rows, 1), 0), tr)
    if nb == 1:
        first = carry_s[0]
    else:
        first = jnp.concatenate([jnp.broadcast_to(carry_s[s], (tr, r_cols)) for s in range(nb)], axis=0)
    for s in range(nb):
        carry_s[s] = p_ref[s, tr - 1:tr, :r_cols]
    f = _token_features(p, first, trow, (mu_ref, w0_ref, a0_ref, wwa_ref, wg_ref, kk_ref, ka_ref), segsum, rw)
    sr = tri_bf.shape[0]
    cum = jnp.concatenate([_dot_sel_lhs(tri_bf, f["lw"][i * sr:(i + 1) * sr]) for i in range(rows // sr)],
                          axis=0)
    tot = jnp.concatenate([jnp.broadcast_to(cum[j * c + c - 1:j * c + c], (c, rw)) for j in range(rows // c)],
                          axis=0)
    d = _decay_factors(f, cum, tot)

    n_chunks = rows // c
    chunks_per_seq = tr // c

    def blk(x, j, g):
        return x[j * c:(j + 1) * c, g * gw:(g + 1) * gw]

    probs = [(j, g) for j in range(n_chunks) for g in range(n_groups)]
    cut = lambda x, ps: [blk(x, j, g) for j, g in ps]
    local = dict(zip(probs, _wkv_local(cut(d["rt"], probs), cut(d["at"], probs), cut(d["bt"], probs),
                                       cut(d["kt"], probs), cut(f["v"], probs), mask_bf, strict, incl, diag)))
    y_blk = {}
    for jj in range(chunks_per_seq):
        ps = [(s * chunks_per_seq + jj, g) for s in range(nb) for g in range(n_groups)]
        loc = [local[pr_] for pr_ in ps]
        sts = [st_s[j // chunks_per_seq, g] for j, g in ps]
        pcs = [blk(d["pinc"], j, g)[c - 1:c] for j, g in ps]
        u, ys, st_new = _wkv_state_step([l[0] for l in loc], cut(d["rt"], ps), [l[1] for l in loc],
                                        cut(f["v"], ps), cut(d["bdec"], ps), cut(d["kdec"], ps), pcs, sts, mask)
        for i, (j, g) in enumerate(ps):
            st_s[j // chunks_per_seq, g] = st_new[i]
            y_blk[j, g] = ys[i] + _dot(loc[i][2], _block_diag(u[i], mask_bf)) + loc[i][3]
    y_rows = [jnp.concatenate([y_blk[j, g] for g in range(n_groups)], axis=1) for j in range(n_chunks)]
    yw = y_rows[0] if n_chunks == 1 else jnp.concatenate(y_rows, axis=0)

    y_r = _rwkv_output(yw, f, rk_ref, gnw_ref, gnb_ref, segsum, head_dim)
    y_c, _ = _gmlp_branch(p, r_cols, lng_ref, lnb_ref, ws2_ref, tril2_ref[...], bsf_ref)
    y_all = jnp.concatenate([y_r, y_c], axis=1)
    for s in range(nb):
        y_ref[s] = y_all[s * tr:(s + 1) * tr]

    @pl.when(ti == n_t - 1)
    def _():
        shift_ref[...] = carry_s[...]
        for s in range(nb):
            for g in range(n_groups):
                _state_to_heads(wkv_ref, s, g, st_s[s, g], head_dim)


def _mixer_short_kernel(p_ref, shift0_ref, wkv0_ref, mu_ref, w0_ref, a0_ref, wwa_ref, wg_ref, kk_ref, ka_ref,
                        rk_ref, gnw_ref, gnb_ref, lng_ref, lnb_ref, ws2_ref, bsf_ref, hmask_ref, hmaskb_ref,
                        tmask_ref, segm_ref, tril2_ref, y_ref, shift_ref, wkv_ref, vn_ref, *, n_heads, head_dim):
    nb, ts, _ = p_ref.shape
    c = WKV_CHUNK
    assert nb * ts == c
    gw = HEAD_GROUP * head_dim
    n_groups = n_heads // HEAD_GROUP
    rw = n_heads * head_dim
    r_cols = mu_ref.shape[1]

    mask = hmask_ref[...]
    mask_bf = hmaskb_ref[...]
    strict, incl = tmask_ref[0], tmask_ref[1]

    def segsum(x):
        return jnp.concatenate([_dot(x[:, g * gw:(g + 1) * gw], mask_bf) for g in range(n_groups)], axis=1)

    p = p_ref[...].reshape(c, p_ref.shape[2])
    trow = _mod_pow2(lax.broadcasted_iota(jnp.int32, (c, 1), 0), ts)
    first = jnp.concatenate([jnp.broadcast_to(shift0_ref[s], (ts, r_cols)) for s in range(nb)], axis=0)
    for s in range(nb):
        shift_ref[s] = p_ref[s, ts - 1:ts, :r_cols]
    f = _token_features(p, first, trow, (mu_ref, w0_ref, a0_ref, wwa_ref, wg_ref, kk_ref, ka_ref), segsum, rw)
    cum = _dot_sel_lhs(segm_ref[...], f["lw"])
    tot = jnp.concatenate([jnp.broadcast_to(cum[s * ts + ts - 1:(s + 1) * ts], (ts, rw)) for s in range(nb)],
                          axis=0)
    d = _decay_factors(f, cum, tot)

    groups = range(n_groups)
    gcut = lambda x: [x[:, g * gw:(g + 1) * gw] for g in groups]
    local = _wkv_local(gcut(d["rt"]), gcut(d["at"]), gcut(d["bt"]), gcut(d["kt"]), gcut(f["v"]), mask_bf,
                       strict, incl, ts)

    ps = [(s, g) for s in range(nb) for g in groups]
    scut = lambda x: [x[s * ts:(s + 1) * ts, g * gw:(g + 1) * gw] for s, g in ps]
    lcut = lambda idx: [local[g][idx][s * ts:(s + 1) * ts] for s, g in ps]
    sts = [_state_from_heads(wkv0_ref, s, g, head_dim) for s, g in ps]
    pcs = [d["pinc"][s * ts + ts - 1:(s + 1) * ts, g * gw:(g + 1) * gw] for s, g in ps]
    u, ys, st_new = _wkv_state_step(lcut(0), scut(d["rt"]), lcut(1), scut(f["v"]), scut(d["bdec"]),
                                    scut(d["kdec"]), pcs, sts, mask)
    for i, (s, g) in enumerate(ps):
        _state_to_heads(wkv_ref, s, g, st_new[i], head_dim)
    rows_of = lambda xs, g: jnp.concatenate([xs[i] for i, (s, gg) in enumerate(ps) if gg == g], axis=0)
    yw = jnp.concatenate([rows_of(ys, g) + _dot(local[g][2], _block_diag(rows_of(u, g), mask_bf)) + local[g][3]
                          for g in groups], axis=1)

    y_r = _rwkv_output(yw, f, rk_ref, gnw_ref, gnb_ref, segsum, head_dim)
    y_c, vn = _gmlp_branch(p, r_cols, lng_ref, lnb_ref, ws2_ref, tril2_ref[...], bsf_ref)
    y_ref[...] = jnp.concatenate([y_r, y_c], axis=1).reshape(nb, ts, -1)
    vn_ref[...] = vn.reshape(nb, ts, -1)


def _mixer_masks(rows, seq, ct, head_dim):
    c, gw = WKV_CHUNK, HEAD_GROUP * head_dim
    blk = np.arange(gw) // head_dim
    hmask = (blk[:, None] == blk[None, :]).astype(np.float32)
    t = np.arange(c)[:, None]
    s = np.arange(HEAD_GROUP * c)[None, :] % c
    strict = (s < t) & (s // seq == t // seq)
    incl = (s <= t) & (s // seq == t // seq)
    tmask = np.stack([strict, incl, strict & (s // INV_BLOCK == t // INV_BLOCK)]).astype(np.float32)
    r = np.arange(rows)
    seg = min(c, seq)
    same = (r[:, None] // seg) == (r[None, :] // seg)
    segm = (same & (r[None, :] <= r[:, None])).astype(np.float32)
    ctm = np.arange(ct)
    tril1 = ((ctm[None, :] <= ctm[:, None]) & (ctm[None, :] // seq == ctm[:, None] // seq))
    tril2 = np.concatenate([tril1, tril1], axis=1).astype(np.float32)
    return [jnp.asarray(hmask), jnp.asarray(hmask, _BF16), jnp.asarray(tmask), jnp.asarray(segm, _BF16),
            jnp.asarray(tril2)]


def _gmlp_weights(w_s, b_s, ct, seq, gwid):
    hc = w_s.shape[0]
    n_seq = ct // min(seq, ct)
    ws = w_s[:, :ct // n_seq, :ct // n_seq]
    bs = b_s[:, :ct // n_seq]
    if n_seq > 1:
        ws = jnp.einsum("st,hij->hsitj", jnp.eye(n_seq, dtype=ws.dtype), ws).reshape(hc, ct, ct)
        bs = jnp.tile(bs, (1, n_seq))
    ws2 = ws.reshape(hc // 2, 2, ct, ct).transpose(0, 2, 1, 3).reshape(hc // 2, ct, 2 * ct)
    return ws2, jnp.repeat(bs.T, gwid // hc, axis=1)


def _mixer_params(prm, rows, seq, ct, head_dim):
    gwid = prm["lng"].shape[1]
    ws2, bsf = _gmlp_weights(prm["w_s"], prm["b_s"], ct, seq, gwid)
    params = [prm["mu"], prm["w0"], prm["a0"], prm["wwa"], prm["wg"], prm["k_k"], prm["k_a"], prm["r_k"],
              prm["gn_w"], prm["gn_b"], prm["lng"], prm["lnb"], ws2, bsf]
    return params + _mixer_masks(rows, seq, ct, head_dim)


def _full_spec(a):
    nd = a.ndim
    return pl.BlockSpec(a.shape, lambda b, i: (0,) * nd)


def _mixer_call(p, prm, *, nb, tr, n_heads, head_dim):
    nseq, t, pcols = p.shape
    rw = n_heads * head_dim
    r_cols = prm["mu"].shape[1]
    gwid = prm["lng"].shape[1]
    gw = HEAD_GROUP * head_dim
    n_groups = n_heads // HEAD_GROUP
    rows = nb * tr
    assert nseq % nb == 0 and t % tr == 0 and tr % GMLP_CHUNK == 0 and rows % SEG_ROWS == 0
    params = _mixer_params(prm, SEG_ROWS, t, GMLP_CHUNK, head_dim)
    return pl.pallas_call(
        functools.partial(_mixer_kernel, n_heads=n_heads, head_dim=head_dim),
        out_shape=(jax.ShapeDtypeStruct((nseq, t, rw + gwid), _F32),
                   jax.ShapeDtypeStruct((nseq, 1, r_cols), _F32),
                   jax.ShapeDtypeStruct((nseq, n_heads, head_dim, head_dim), _F32)),
        grid=(nseq // nb, t // tr),
        in_specs=[pl.BlockSpec((nb, tr, pcols), lambda b, i: (b, i, 0))] + [_full_spec(a) for a in params],
        out_specs=(pl.BlockSpec((nb, tr, rw + gwid), lambda b, i: (b, i, 0)),
                   pl.BlockSpec((nb, 1, r_cols), lambda b, i: (b, 0, 0)),
                   pl.BlockSpec((nb, n_heads, head_dim, head_dim), lambda b, i: (b, 0, 0, 0))),
        scratch_shapes=[pltpu.VMEM((nb, n_groups, gw, gw), _F32), pltpu.VMEM((nb, 1, r_cols), _F32)],
        compiler_params=pltpu.CompilerParams(dimension_semantics=("arbitrary", "arbitrary"),
                                             vmem_limit_bytes=_VMEM_LIMIT),
        name="mixers_fresh",
    )(p, *params)


def _mixer_short_call(p, shift0, wkv0, prm, *, n_heads, head_dim):
    nseq, ts, pcols = p.shape
    nb = WKV_CHUNK // ts
    assert nb * ts == WKV_CHUNK and ts % SUBLANES == 0 and nseq % nb == 0
    rw = n_heads * head_dim
    r_cols = prm["mu"].shape[1]
    gwid = prm["lng"].shape[1]
    params = _mixer_params(prm, WKV_CHUNK, ts, WKV_CHUNK, head_dim)
    seqs = lambda *tail: pl.BlockSpec((nb,) + tail, lambda b, i: (b,) + (0,) * len(tail))
    return pl.pallas_call(
        functools.partial(_mixer_short_kernel, n_heads=n_heads, head_dim=head_dim),
        out_shape=(jax.ShapeDtypeStruct((nseq, ts, rw + gwid), _F32),
                   jax.ShapeDtypeStruct((nseq, 1, r_cols), _F32),
                   jax.ShapeDtypeStruct((nseq, n_heads, head_dim, head_dim), _F32),
                   jax.ShapeDtypeStruct((nseq, ts, gwid), _F32)),
        grid=(nseq // nb, 1),
        in_specs=[seqs(ts, pcols), seqs(1, r_cols), seqs(n_heads, head_dim, head_dim)]
                 + [_full_spec(a) for a in params],
        out_specs=(seqs(ts, rw + gwid), seqs(1, r_cols), seqs(n_heads, head_dim, head_dim), seqs(ts, gwid)),
        compiler_params=pltpu.CompilerParams(dimension_semantics=("arbitrary", "arbitrary"),
                                             vmem_limit_bytes=_VMEM_LIMIT),
        name="mixers_state",
    )(p, shift0, wkv0, *params)


def _pick_tile(n, pref):
    t = min(n, pref)
    while n % t:
        t //= 2
    return t


def kernel(x_prompt, x_sample, state_shift, state_wkv, c_prompt, c_sample, w_ada, b_ada, ffn1_gu, ffn1_dn, w_in, mu_shift, w0, w_lora_up, a0, a_lora_up, g_lora_up, k_k, k_a, r_k, gn_w, gn_b, ln_v_g, ln_v_b, w_s, b_s, w_out, ffn2_gu, ffn2_dn, final_g):
    depth = w_ada.shape[0]
    bp, tp, d = x_prompt.shape
    bs, ts, _ = x_sample.shape
    n_heads, head_dim = r_k.shape[1], r_k.shape[2]
    rw = n_heads * head_dim
    w_rank, a_rank = w_lora_up.shape[1], a_lora_up.shape[1]
    assert w_rank == a_rank and n_heads % HEAD_GROUP == 0 and head_dim == WKV_CHUNK
    d_ff = ffn1_dn.shape[1]
    tf = 256 if d_ff % 256 == 0 else 128

    xp = x_prompt.reshape(bp * tp, d)
    xs = x_sample.reshape(bs * ts, d)
    assert bs % SUBLANES == 0
    n_c = bs + bp
    n_c_pad = -(-n_c // 16) * 16
    c_all = jnp.concatenate([c_sample, c_prompt, jnp.zeros((n_c_pad - n_c, d), _F32)], axis=0)

    tm_p = _pick_tile(tp, 512)
    tr_p = _pick_tile(tp, 256)
    tm_s = ts * _pick_tile(bs, 64)
    nb_p = _pick_tile(bp, 2)

    shp, wkp, shs, wks, cvs = [], [], [], [], []
    for l in range(depth):
        mod = _mod_call(c_all, w_ada[l], b_ada[l])
        tiles_p = dict(tm=tm_p, seq0=bs, seqs_per_tile=1, tiles_per_seq=tp // tm_p, tf=tf, stage=WEIGHT_STAGE_STEPS)
        tiles_s = dict(tm=tm_s, seq0=0, seqs_per_tile=tm_s // ts, tiles_per_seq=1, tf=tf, stage=0)

        zw = jnp.zeros((w_rank, rw), _F32)
        wwa = jnp.concatenate([jnp.concatenate([w_lora_up[l], zw], axis=1),
                               jnp.concatenate([zw, a_lora_up[l]], axis=1)], axis=0).astype(_BF16)
        row = lambda a: a.reshape(1, -1)
        prm = dict(mu=row(mu_shift[l]), w0=row(w0[l]), a0=row(a0[l]), wwa=wwa,
                   wg=g_lora_up[l].astype(_BF16), k_k=row(k_k[l]), k_a=row(k_a[l]), r_k=row(r_k[l]),
                   gn_w=row(gn_w[l]), gn_b=row(gn_b[l]), lng=row(ln_v_g[l]), lnb=row(ln_v_b[l]),
                   w_s=w_s[l], b_s=b_s[l])
        fg = final_g.reshape(1, d)
        final = l == depth - 1

        hp, pp, wgu1, wdn1, win = _ffn_in_call(xp, mod, ffn1_gu[l], ffn1_dn[l], w_in[l], **tiles_p)
        ymp, sh_p, wk_p = _mixer_call(pp.reshape(bp, tp, -1), prm, nb=nb_p, tr=tr_p,
                                      n_heads=n_heads, head_dim=head_dim)
        xp, wout, wgu2, wdn2 = _out_ffn_call(ymp.reshape(bp * tp, -1), hp, mod, w_out[l], ffn2_gu[l], ffn2_dn[l],
                                             fg, final=final, **tiles_p)

        hs, ps = _ffn_in_call(xs, mod, wgu1, wdn1, win, **tiles_s)
        yms, sh_s, wk_s, v_s = _mixer_short_call(ps.reshape(bs, ts, -1), state_shift[l][:, None, :], state_wkv[l], prm,
                                                 n_heads=n_heads, head_dim=head_dim)
        xs, = _out_ffn_call(yms.reshape(bs * ts, -1), hs, mod, wout, wgu2, wdn2, fg, final=final, **tiles_s)

        shp.append(sh_p[:, 0]); wkp.append(wk_p); shs.append(sh_s[:, 0]); wks.append(wk_s); cvs.append(v_s)

    y_prompt = xp.reshape(bp, tp, d)
    y_sample = xs.reshape(bs, ts, d)
    return (y_prompt, y_sample, jnp.stack(shp), jnp.stack(wkp), jnp.stack(shs), jnp.stack(wks), jnp.stack(cvs))
```

```python
import functools
import math

import jax
import jax.numpy as jnp
import numpy as np
from jax import lax
from jax.experimental import pallas as pl
from jax.experimental.pallas import tpu as pltpu

_F32 = jnp.float32
_BF16 = jnp.bfloat16

RMS_EPS = 1e-6
LN_EPS = 1e-5
GN_EPS = 64e-5
N_MOD = 9
WKV_CHUNK = 64
HEAD_GROUP = 4
INV_BLOCK = 16
GMLP_CHUNK = 128
SEG_ROWS = 256
SUBLANES = 8
MOD_PER_STEP = 3
WEIGHT_STAGE_STEPS = 16
DECAY_SCALE = math.exp(-0.5)

_V7X_VMEM_BYTES = 64 * 1024 * 1024
_VMEM_LIMIT = _V7X_VMEM_BYTES - 8 * 1024 * 1024


def _dot(a, b):
    return jnp.dot(a.astype(_BF16), b.astype(_BF16), preferred_element_type=_F32)


def _dot_nt(a, b):
    return lax.dot_general(a.astype(_BF16), b.astype(_BF16), (((1,), (1,)), ((), ())),
                           preferred_element_type=_F32)


def _dot_tn(a, b):
    return lax.dot_general(a.astype(_BF16), b.astype(_BF16), (((0,), (0,)), ((), ())),
                           preferred_element_type=_F32)


def _split_hi_lo(x):
    hi = x.astype(_BF16)
    lo = (x - hi.astype(_F32)).astype(_BF16)
    return hi, lo


def _dot_sel_lhs(sel, x):
    hi, lo = _split_hi_lo(x)
    return (jnp.dot(sel, hi, preferred_element_type=_F32)
            + jnp.dot(sel, lo, preferred_element_type=_F32))


def _div_pow2(x, n):
    assert n & (n - 1) == 0
    return lax.shift_right_logical(x, jnp.int32(n.bit_length() - 1))


def _mod_pow2(x, n):
    assert n & (n - 1) == 0
    return lax.bitwise_and(x, jnp.int32(n - 1))


def _rms(x):
    return x * lax.rsqrt(jnp.mean(x * x, -1, keepdims=True) + RMS_EPS)


def _sigmoid(x):
    return 1.0 / (1.0 + jnp.exp(-x))


def _affine(x, scale, shift=None):
    mb = scale.shape[0]
    if mb == 1:
        y = x * scale
        return y if shift is None else y + shift
    tm, d = x.shape
    y = x.reshape(mb, tm // mb, d) * scale[:, None, :]
    if shift is not None:
        y = y + shift[:, None, :]
    return y.reshape(tm, d)


def _swiglu(n_bf, wgu_ref, wdn_ref, tf):
    d_ff = wdn_ref.shape[0]
    acc = None
    for j in range(d_ff // tf):
        g = jnp.dot(n_bf, wgu_ref[:, j * tf:(j + 1) * tf], preferred_element_type=_F32)
        u = jnp.dot(n_bf, wgu_ref[:, d_ff + j * tf:d_ff + (j + 1) * tf], preferred_element_type=_F32)
        hm = (g * _sigmoid(g) * u).astype(_BF16)
        part = jnp.dot(hm, wdn_ref[j * tf:(j + 1) * tf, :], preferred_element_type=_F32)
        acc = part if acc is None else acc + part
    return acc


def _mod_kernel(c_ref, w_ref, b_ref, o_ref):
    c = c_ref[...]
    s = c * _sigmoid(c)
    o = _dot(s, w_ref[...]) + b_ref[...]
    d = c.shape[1]
    for k in range(o_ref.shape[0]):
        o_ref[k] = o[:, k * d:(k + 1) * d]


def _mod_call(c_all, w_ada, b_ada):
    n, d = c_all.shape
    per_step = MOD_PER_STEP if N_MOD % MOD_PER_STEP == 0 else 1
    return pl.pallas_call(
        _mod_kernel,
        out_shape=jax.ShapeDtypeStruct((N_MOD, n, d), _F32),
        grid=(N_MOD // per_step,),
        in_specs=[pl.BlockSpec((n, d), lambda j: (0, 0)),
                  pl.BlockSpec((d, per_step * d), lambda j: (0, j)),
                  pl.BlockSpec((1, per_step * d), lambda j: (0, j))],
        out_specs=pl.BlockSpec((per_step, n, d), lambda j: (j, 0, 0)),
        compiler_params=pltpu.CompilerParams(dimension_semantics=("arbitrary",),
                                             vmem_limit_bytes=_VMEM_LIMIT),
        name="adaln_mod",
    )(c_all, w_ada, b_ada.reshape(1, N_MOD * d))


def _mod_rows(mod_ref, pick, tile):
    if pick is None:
        return lambda j: mod_ref[j]
    seq0, tiles_per_seq = pick
    row = lax.rem(seq0 + lax.div(tile, jnp.int32(tiles_per_seq)), jnp.int32(SUBLANES))
    return lambda j: mod_ref[j, pl.ds(row, 1), :]


def _mod_spec(mod, seq0, seqs_per_tile, tiles_per_seq, tile):
    n_mod, _, d = mod.shape
    if seqs_per_tile > 1:
        assert seqs_per_tile % SUBLANES == 0 and seq0 % seqs_per_tile == 0 and tiles_per_seq == 1
        first = seq0 // seqs_per_tile
        return pl.BlockSpec((n_mod, seqs_per_tile, d), lambda i: (0, first + tile(i), 0)), None
    return (pl.BlockSpec((n_mod, SUBLANES, d), lambda i: (0, (seq0 + tile(i) // tiles_per_seq) // SUBLANES, 0)),
            (seq0, tiles_per_seq))


def _resident(shape):
    nd = len(shape)
    return pl.BlockSpec(shape, lambda *_: (0,) * nd, pipeline_mode=pl.Buffered(1))


def _dense_call(body, name, acts, mod, weights, extra, out_widths, *, tm, seq0, seqs_per_tile, tiles_per_seq,
                stage):
    n = acts[0].shape[0]
    tile = (lambda i: jnp.maximum(i - stage, 0)) if stage else (lambda i: i)
    mod_spec, pick = _mod_spec(mod, seq0, seqs_per_tile, tiles_per_seq, tile)
    rows = lambda width: pl.BlockSpec((tm, width), lambda i: (tile(i), 0))
    n_act, n_w, n_x, n_out = len(acts), len(weights), len(extra), len(out_widths)

    def kernel(*refs):
        it = iter(refs)
        act_refs = [next(it) for _ in range(n_act)]
        mod_ref = next(it)
        w_in_refs = [next(it) for _ in range(n_w)]
        x_refs = [next(it) for _ in range(n_x)]
        out_refs = [next(it) for _ in range(n_out)]
        step = pl.program_id(0)

        def run(w_refs):
            t = step - stage
            body(act_refs, _mod_rows(mod_ref, pick, t), w_refs, x_refs, out_refs)

        if not stage:
            run(w_in_refs)
            return
        w_out_refs = [next(it) for _ in range(n_w)]
        w_bf_refs = [next(it) for _ in range(n_w)]

        @pl.when(step < stage)
        def _():
            for src, dst, out in zip(w_in_refs, w_bf_refs, w_out_refs):
                chunk = src.shape[0]
                v = src[...].astype(_BF16)
                dst[pl.ds(pl.multiple_of(step * chunk, chunk), chunk), :] = v
                out[...] = v

        pl.when(step >= stage)(lambda: run(w_bf_refs))

    out_shape = [jax.ShapeDtypeStruct((n, w), _F32) for w in out_widths]
    out_specs = [rows(w) for w in out_widths]
    scratch = []
    if stage:
        assert all(w.shape[0] % (stage * 2 * SUBLANES) == 0 for w in weights)
        chunk_spec = lambda w: pl.BlockSpec((w.shape[0] // stage, w.shape[1]),
                                            lambda i: (jnp.minimum(i, stage - 1), 0))
        w_specs = [chunk_spec(w) for w in weights]
        out_shape += [jax.ShapeDtypeStruct(w.shape, _BF16) for w in weights]
        out_specs += [chunk_spec(w) for w in weights]
        scratch = [pltpu.VMEM(w.shape, _BF16) for w in weights]
    else:
        w_specs = [_resident(w.shape) for w in weights]
    return pl.pallas_call(
        kernel,
        out_shape=tuple(out_shape),
        grid=(stage + n // tm,),
        in_specs=[rows(a.shape[1]) for a in acts] + [mod_spec] + w_specs + [_resident(e.shape) for e in extra],
        out_specs=tuple(out_specs),
        scratch_shapes=scratch,
        compiler_params=pltpu.CompilerParams(dimension_semantics=("arbitrary",),
                                             vmem_limit_bytes=_VMEM_LIMIT),
        name=name,
    )(*acts, mod, *weights, *extra)


def _ffn_in_body(acts, mod, weights, extra, outs, *, tf):
    (x_ref,), (wgu_ref, wdn_ref, win_ref), (h_ref, p_ref) = acts, weights, outs
    x = x_ref[...]
    sh1, sc1, g1, sh2, sc2 = (mod(j) for j in range(5))
    n1 = _affine(_rms(x), 1.0 + sc1, sh1).astype(_BF16)
    h = x + _affine(_swiglu(n1, wgu_ref, wdn_ref, tf), 0.5 * g1)
    h_ref[...] = h
    n2 = _affine(_rms(h), 1.0 + sc2, sh2).astype(_BF16)
    p_ref[...] = jnp.dot(n2, win_ref[...], preferred_element_type=_F32)


def _ffn_in_call(x, mod, wgu, wdn, win, *, tf, **tiling):
    return _dense_call(functools.partial(_ffn_in_body, tf=tf), "ffn1_inproj", [x], mod, [wgu, wdn, win], [],
                       [x.shape[1], win.shape[1]], **tiling)


def _out_ffn_body(acts, mod, weights, extra, outs, *, tf, final):
    (ym_ref, h_ref), (wout_ref, wgu_ref, wdn_ref), (fg_ref,), (o_ref,) = acts, weights, extra, outs
    g2, sh3, sc3, g3 = (mod(j) for j in range(5, 9))
    h = h_ref[...] + _affine(jnp.dot(ym_ref[...].astype(_BF16), wout_ref[...],
                                     preferred_element_type=_F32), g2)
    n3 = _affine(_rms(h), 1.0 + sc3, sh3).astype(_BF16)
    h = h + _affine(_swiglu(n3, wgu_ref, wdn_ref, tf), 0.5 * g3)
    if final:
        h = _rms(h) * fg_ref[...]
    o_ref[...] = h


def _out_ffn_call(ym, h, mod, wout, wgu, wdn, fg, *, tf, final, **tiling):
    return _dense_call(functools.partial(_out_ffn_body, tf=tf, final=final), "outproj_ffn2", [ym, h], mod,
                       [wout, wgu, wdn], [fg], [h.shape[1]], **tiling)


def _rows_bf16(*xs):
    return jnp.concatenate([x.astype(_BF16) for x in xs], axis=0)


def _block_diag(y, mask_bf):
    reps = mask_bf.shape[0] // y.shape[0]
    return jnp.concatenate([y.astype(_BF16)] * reps, axis=0) * mask_bf


def _nilpotent_inverse(n, bd, order):
    every = range(len(n))
    c = n[0].shape[0]
    tm = n
    levels = order.bit_length() - 2
    if levels <= 0:
        return tm
    nb = bd(n)
    npow = [_dot(n[i], nb[i]) for i in every]
    for lvl in range(levels):
        nb = bd(npow)
        if lvl + 1 < levels:
            res = [_dot(jnp.concatenate([tm[i], npow[i]], axis=0), nb[i]) for i in every]
            tm = [tm[i] + npow[i] + res[i][:c] for i in every]
            npow = [res[i][c:] for i in every]
        else:
            tm = [tm[i] + npow[i] + _dot(tm[i], nb[i]) for i in every]
    return tm


def _wkv_local(rt, at, bt, kt, vv, mask_bf, strict, incl, diag):
    n = len(rt)
    c = rt[0].shape[0]
    every = range(n)
    bd = lambda xs: [_block_diag(x, mask_bf) for x in xs]
    x = [_rows_bf16(at[i], rt[i]) for i in every]
    bb, kb = bd(bt), bd(kt)
    xb = [_dot_nt(x[i], bb[i]) for i in every]
    xk = [_dot_nt(x[i], kb[i]) for i in every]
    lab = [xb[i][:c] * strict for i in every]
    mrb = [xb[i][c:] * incl for i in every]
    lak = [xk[i][:c] * strict for i in every]
    mrk = [xk[i][c:] * incl for i in every]
    vb = bd(vv)
    wy = [_dot(jnp.concatenate([lak[i], mrk[i]], axis=0), vb[i]) for i in every]
    w1 = [wy[i][:c] for i in every]
    y0 = [wy[i][c:] for i in every]
    if isinstance(diag, int):
        tm = _nilpotent_inverse(lab, bd, diag)
    else:
        assert c == 4 * INV_BLOCK
        nd = [lab[i] * diag for i in every]
        noff = [lab[i] - nd[i] for i in every]
        tdm = _nilpotent_inverse(nd, bd, INV_BLOCK)
        noffb = bd(noff)
        m1 = [noff[i] + _dot(tdm[i], noffb[i]) for i in every]
        m1b = bd(m1)
        m2 = [_dot(m1[i], m1b[i]) for i in every]
        m2b = bd(m2)
        q = [m1[i] + m2[i] + _dot(m1[i], m2b[i]) for i in every]
        tdmb = bd(tdm)
        tm = [q[i] + tdm[i] + _dot(q[i], tdmb[i]) for i in every]
    atb = bd(at)
    ap = [at[i] + _dot(tm[i], atb[i]) for i in every]
    w1b = bd(w1)
    u0 = [w1[i] + _dot(tm[i], w1b[i]) for i in every]
    return [(ap[i], u0[i], mrb[i], y0[i]) for i in every]


def _wkv_state_step(ap, rt, u0, vv, bdec, kdec, pc, st, mask):
    every = range(len(rt))
    c = rt[0].shape[0]
    ar = [_dot_nt(_rows_bf16(ap[i], rt[i]), st[i]) for i in every]
    u = [ar[i][:c] + u0[i] for i in every]
    upd = [_dot_tn(_rows_bf16(u[i], vv[i]), _rows_bf16(bdec[i], kdec[i])) for i in every]
    st_new = [st[i] * pc[i] + upd[i] * mask for i in every]
    return u, [ar[i][c:] for i in every], st_new


def _token_features(p, first, trow, prm, segsum, rw):
    mu_ref, w0_ref, a0_ref, wwa_ref, wg_ref, kk_ref, ka_ref = prm
    r_cols = mu_ref.shape[1]
    pr = p[:, :r_cols]
    prev = jnp.where(trow == 0, first, pltpu.roll(pr, 1, 0))
    xm = pr + (prev - pr) * mu_ref[...]
    r = xm[:, 0:rw]
    k = xm[:, rw:2 * rw]
    v = xm[:, 2 * rw:3 * rw]
    lo_w = wwa_ref.shape[0]
    x_lo = xm[:, 3 * rw:3 * rw + lo_w]
    gd = xm[:, 3 * rw + lo_w:r_cols]
    lane = lax.broadcasted_iota(jnp.int32, (1, lo_w), 1)
    x_lo = jnp.where(lane < lo_w // 2, jnp.tanh(x_lo), x_lo)
    wa = _dot(x_lo, wwa_ref[...])
    wpre = w0_ref[...] + wa[:, :rw]
    lw = -DECAY_SCALE * _sigmoid(wpre)
    a = _sigmoid(a0_ref[...] + wa[:, rw:])
    gate = _dot(_sigmoid(gd), wg_ref[...])
    kkv = k * kk_ref[...]
    kkn = kkv * lax.rsqrt(jnp.maximum(segsum(kkv * kkv), 1e-24))
    k2 = k * (1.0 + (a - 1.0) * ka_ref[...])
    return dict(r=r, v=v, lw=lw, a=a, gate=gate, kkn=kkn, k2=k2)


def _decay_factors(f, cum, tot):
    pinc = jnp.exp(cum)
    pinv = jnp.exp(-cum)
    pend = jnp.exp(tot - cum)
    beta = f["kkn"] * f["a"]
    ops = dict(rt=f["r"] * pinc, at=-f["kkn"] * jnp.exp(cum - f["lw"]), bt=beta * pinv, kt=f["k2"] * pinv,
               bdec=beta * pend, kdec=f["k2"] * pend, v=f["v"])
    return dict({name: x.astype(_BF16) for name, x in ops.items()}, pinc=pinc)


def _rwkv_output(yw, f, rk_ref, gnw_ref, gnb_ref, segsum, head_dim):
    inv_n = 1.0 / head_dim
    mean = segsum(yw) * inv_n
    dev = yw - mean
    var = segsum(dev * dev) * inv_n
    yn = dev * lax.rsqrt(var + GN_EPS) * gnw_ref[...] + gnb_ref[...]
    bonus = segsum(f["r"] * f["k2"] * rk_ref[...]) * f["v"]
    return (yn + bonus) * f["gate"]


def _gmlp_branch(p, r_cols, lng_ref, lnb_ref, ws2_ref, tril2, bsf_ref):
    gwid = lng_ref.shape[1]
    ct = ws2_ref.shape[1]
    rows = p.shape[0]
    pu = p[:, r_cols:r_cols + gwid]
    pv = p[:, r_cols + gwid:r_cols + 2 * gwid]
    pm = jnp.mean(pv, -1, keepdims=True)
    pd = pv - pm
    pvar = jnp.mean(pd * pd, -1, keepdims=True)
    vn = pd * lax.rsqrt(pvar + LN_EPS) * lng_ref[...] + lnb_ref[...]
    n_pair = ws2_ref.shape[0]
    pw = gwid // n_pair
    first_half = lax.broadcasted_iota(jnp.int32, (1, pw), 1) < pw // 2
    ws_tril = [(ws2_ref[j] * tril2).astype(_BF16) for j in range(n_pair)]
    mixed_rows = []
    for q in range(rows // ct):
        outs = []
        for j in range(n_pair):
            vp = vn[q * ct:(q + 1) * ct, j * pw:(j + 1) * pw]
            rhs = jnp.concatenate([jnp.where(first_half, vp, 0.0), jnp.where(first_half, 0.0, vp)], axis=0)
            outs.append(_dot(ws_tril[j], rhs))
        mixed_rows.append(jnp.concatenate(outs, axis=1) + bsf_ref[...])
    mixed = mixed_rows[0] if len(mixed_rows) == 1 else jnp.concatenate(mixed_rows, axis=0)
    return pu * mixed, vn


def _state_from_heads(wkv_ref, s, g, head_dim):
    zero = jnp.zeros((head_dim, head_dim), _F32)
    blocks = []
    for h in range(HEAD_GROUP):
        blk = wkv_ref[s, g * HEAD_GROUP + h]
        blocks.append(jnp.concatenate([zero] * h + [blk] + [zero] * (HEAD_GROUP - 1 - h), axis=1))
    return jnp.concatenate(blocks, axis=0)


def _state_to_heads(wkv_ref, s, g, st, head_dim):
    for h in range(HEAD_GROUP):
        wkv_ref[s, g * HEAD_GROUP + h] = st[h * head_dim:(h + 1) * head_dim, h * head_dim:(h + 1) * head_dim]


def _mixer_kernel(p_ref, mu_ref, w0_ref, a0_ref, wwa_ref, wg_ref, kk_ref, ka_ref, rk_ref, gnw_ref, gnb_ref,
                  lng_ref, lnb_ref, ws2_ref, bsf_ref, hmask_ref, hmaskb_ref, tmask_ref, segm_ref, tril2_ref,
                  y_ref, shift_ref, wkv_ref, st_s, carry_s, *, n_heads, head_dim):
    nb, tr, _ = p_ref.shape
    c = WKV_CHUNK
    gw = HEAD_GROUP * head_dim
    n_groups = n_heads // HEAD_GROUP
    rw = n_heads * head_dim
    rows = nb * tr
    ti = pl.program_id(1)
    n_t = pl.num_programs(1)
    r_cols = mu_ref.shape[1]

    @pl.when(ti == 0)
    def _():
        carry_s[...] = jnp.zeros_like(carry_s)
        st_s[...] = jnp.zeros_like(st_s)

    mask = hmask_ref[...]
    mask_bf = hmaskb_ref[...]
    strict, incl, diag = tmask_ref[0], tmask_ref[1], tmask_ref[2]
    tri_bf = segm_ref[...]

    def segsum(x):
        return jnp.concatenate([_dot(x[:, g * gw:(g + 1) * gw], mask_bf) for g in range(n_groups)], axis=1)

    p = p_ref[0] if nb == 1 else jnp.concatenate([p_ref[s] for s in range(nb)], axis=0)
    trow = _mod_pow2(lax.broadcasted_iota(jnp.int32, (rows, 1), 0), tr)
    if nb == 1:
        first = carry_s[0]
    else:
        first = jnp.concatenate([jnp.broadcast_to(carry_s[s], (tr, r_cols)) for s in range(nb)], axis=0)
    for s in range(nb):
        carry_s[s] = p_ref[s, tr - 1:tr, :r_cols]
    f = _token_features(p, first, trow, (mu_ref, w0_ref, a0_ref, wwa_ref, wg_ref, kk_ref, ka_ref), segsum, rw)
    sr = tri_bf.shape[0]
    cum = jnp.concatenate([_dot_sel_lhs(tri_bf, f["lw"][i * sr:(i + 1) * sr]) for i in range(rows // sr)],
                          axis=0)
    tot = jnp.concatenate([jnp.broadcast_to(cum[j * c + c - 1:j * c + c], (c, rw)) for j in range(rows // c)],
                          axis=0)
    d = _decay_factors(f, cum, tot)

    n_chunks = rows // c
    chunks_per_seq = tr // c

    def blk(x, j, g):
        return x[j * c:(j + 1) * c, g * gw:(g + 1) * gw]

    probs = [(j, g) for j in range(n_chunks) for g in range(n_groups)]
    cut = lambda x, ps: [blk(x, j, g) for j, g in ps]
    local = dict(zip(probs, _wkv_local(cut(d["rt"], probs), cut(d["at"], probs), cut(d["bt"], probs),
                                       cut(d["kt"], probs), cut(d["v"], probs), mask_bf, strict, incl, diag)))
    y_blk = {}
    for jj in range(chunks_per_seq):
        ps = [(s * chunks_per_seq + jj, g) for s in range(nb) for g in range(n_groups)]
        loc = [local[pr_] for pr_ in ps]
        sts = [st_s[j // chunks_per_seq, g] for j, g in ps]
        pcs = [blk(d["pinc"], j, g)[c - 1:c] for j, g in ps]
        u, ys, st_new = _wkv_state_step([l[0] for l in loc], cut(d["rt"], ps), [l[1] for l in loc],
                                        cut(d["v"], ps), cut(d["bdec"], ps), cut(d["kdec"], ps), pcs, sts, mask)
        for i, (j, g) in enumerate(ps):
            st_s[j // chunks_per_seq, g] = st_new[i]
            y_blk[j, g] = ys[i] + _dot(loc[i][2], _block_diag(u[i], mask_bf)) + loc[i][3]
    y_rows = [jnp.concatenate([y_blk[j, g] for g in range(n_groups)], axis=1) for j in range(n_chunks)]
    yw = y_rows[0] if n_chunks == 1 else jnp.concatenate(y_rows, axis=0)

    y_r = _rwkv_output(yw, f, rk_ref, gnw_ref, gnb_ref, segsum, head_dim)
    y_c, _ = _gmlp_branch(p, r_cols, lng_ref, lnb_ref, ws2_ref, tril2_ref[...], bsf_ref)
    y_all = jnp.concatenate([y_r, y_c], axis=1)
    for s in range(nb):
        y_ref[s] = y_all[s * tr:(s + 1) * tr]

    @pl.when(ti == n_t - 1)
    def _():
        shift_ref[...] = carry_s[...]
        for s in range(nb):
            for g in range(n_groups):
                _state_to_heads(wkv_ref, s, g, st_s[s, g], head_dim)


def _mixer_short_kernel(p_ref, shift0_ref, wkv0_ref, mu_ref, w0_ref, a0_ref, wwa_ref, wg_ref, kk_ref, ka_ref,
                        rk_ref, gnw_ref, gnb_ref, lng_ref, lnb_ref, ws2_ref, bsf_ref, hmask_ref, hmaskb_ref,
                        tmask_ref, segm_ref, tril2_ref, y_ref, shift_ref, wkv_ref, vn_ref, *, n_heads, head_dim):
    nb, ts, _ = p_ref.shape
    c = WKV_CHUNK
    assert nb * ts == c
    gw = HEAD_GROUP * head_dim
    n_groups = n_heads // HEAD_GROUP
    rw = n_heads * head_dim
    r_cols = mu_ref.shape[1]

    mask = hmask_ref[...]
    mask_bf = hmaskb_ref[...]
    strict, incl = tmask_ref[0], tmask_ref[1]

    def segsum(x):
        return jnp.concatenate([_dot(x[:, g * gw:(g + 1) * gw], mask_bf) for g in range(n_groups)], axis=1)

    p = p_ref[...].reshape(c, p_ref.shape[2])
    trow = _mod_pow2(lax.broadcasted_iota(jnp.int32, (c, 1), 0), ts)
    first = jnp.concatenate([jnp.broadcast_to(shift0_ref[s], (ts, r_cols)) for s in range(nb)], axis=0)
    for s in range(nb):
        shift_ref[s] = p_ref[s, ts - 1:ts, :r_cols]
    f = _token_features(p, first, trow, (mu_ref, w0_ref, a0_ref, wwa_ref, wg_ref, kk_ref, ka_ref), segsum, rw)
    cum = _dot_sel_lhs(segm_ref[...], f["lw"])
    tot = jnp.concatenate([jnp.broadcast_to(cum[s * ts + ts - 1:(s + 1) * ts], (ts, rw)) for s in range(nb)],
                          axis=0)
    d = _decay_factors(f, cum, tot)

    groups = range(n_groups)
    gcut = lambda x: [x[:, g * gw:(g + 1) * gw] for g in groups]
    local = _wkv_local(gcut(d["rt"]), gcut(d["at"]), gcut(d["bt"]), gcut(d["kt"]), gcut(d["v"]), mask_bf,
                       strict, incl, ts)

    ps = [(s, g) for s in range(nb) for g in groups]
    scut = lambda x: [x[s * ts:(s + 1) * ts, g * gw:(g + 1) * gw] for s, g in ps]
    lcut = lambda idx: [local[g][idx][s * ts:(s + 1) * ts] for s, g in ps]
    sts = [_state_from_heads(wkv0_ref, s, g, head_dim) for s, g in ps]
    pcs = [d["pinc"][s * ts + ts - 1:(s + 1) * ts, g * gw:(g + 1) * gw] for s, g in ps]
    u, ys, st_new = _wkv_state_step(lcut(0), scut(d["rt"]), lcut(1), scut(d["v"]), scut(d["bdec"]),
                                    scut(d["kdec"]), pcs, sts, mask)
    for i, (s, g) in enumerate(ps):
        _state_to_heads(wkv_ref, s, g, st_new[i], head_dim)
    rows_of = lambda xs, g: jnp.concatenate([xs[i] for i, (s, gg) in enumerate(ps) if gg == g], axis=0)
    yw = jnp.concatenate([rows_of(ys, g) + _dot(local[g][2], _block_diag(rows_of(u, g), mask_bf)) + local[g][3]
                          for g in groups], axis=1)

    y_r = _rwkv_output(yw, f, rk_ref, gnw_ref, gnb_ref, segsum, head_dim)
    y_c, vn = _gmlp_branch(p, r_cols, lng_ref, lnb_ref, ws2_ref, tril2_ref[...], bsf_ref)
    y_ref[...] = jnp.concatenate([y_r, y_c], axis=1).reshape(nb, ts, -1)
    vn_ref[...] = vn.reshape(nb, ts, -1)


def _mixer_masks(rows, seq, ct, head_dim):
    c, gw = WKV_CHUNK, HEAD_GROUP * head_dim
    blk = np.arange(gw) // head_dim
    hmask = (blk[:, None] == blk[None, :]).astype(np.float32)
    t = np.arange(c)[:, None]
    s = np.arange(HEAD_GROUP * c)[None, :] % c
    strict = (s < t) & (s // seq == t // seq)
    incl = (s <= t) & (s // seq == t // seq)
    tmask = np.stack([strict, incl, strict & (s // INV_BLOCK == t // INV_BLOCK)]).astype(np.float32)
    r = np.arange(rows)
    seg = min(c, seq)
    same = (r[:, None] // seg) == (r[None, :] // seg)
    segm = (same & (r[None, :] <= r[:, None])).astype(np.float32)
    ctm = np.arange(ct)
    tril1 = ((ctm[None, :] <= ctm[:, None]) & (ctm[None, :] // seq == ctm[:, None] // seq))
    tril2 = np.concatenate([tril1, tril1], axis=1).astype(np.float32)
    return [jnp.asarray(hmask), jnp.asarray(hmask, _BF16), jnp.asarray(tmask), jnp.asarray(segm, _BF16),
            jnp.asarray(tril2)]


def _gmlp_weights(w_s, b_s, ct, seq, gwid):
    hc = w_s.shape[0]
    n_seq = ct // min(seq, ct)
    ws = w_s[:, :ct // n_seq, :ct // n_seq]
    bs = b_s[:, :ct // n_seq]
    if n_seq > 1:
        ws = jnp.einsum("st,hij->hsitj", jnp.eye(n_seq, dtype=ws.dtype), ws).reshape(hc, ct, ct)
        bs = jnp.tile(bs, (1, n_seq))
    ws2 = ws.reshape(hc // 2, 2, ct, ct).transpose(0, 2, 1, 3).reshape(hc // 2, ct, 2 * ct)
    return ws2, jnp.repeat(bs.T, gwid // hc, axis=1)


def _mixer_params(prm, rows, seq, ct, head_dim):
    gwid = prm["lng"].shape[1]
    ws2, bsf = _gmlp_weights(prm["w_s"], prm["b_s"], ct, seq, gwid)
    params = [prm["mu"], prm["w0"], prm["a0"], prm["wwa"], prm["wg"], prm["k_k"], prm["k_a"], prm["r_k"],
              prm["gn_w"], prm["gn_b"], prm["lng"], prm["lnb"], ws2, bsf]
    return params + _mixer_masks(rows, seq, ct, head_dim)


def _full_spec(a):
    nd = a.ndim
    return pl.BlockSpec(a.shape, lambda b, i: (0,) * nd)


def _mixer_call(p, prm, *, nb, tr, n_heads, head_dim):
    nseq, t, pcols = p.shape
    rw = n_heads * head_dim
    r_cols = prm["mu"].shape[1]
    gwid = prm["lng"].shape[1]
    gw = HEAD_GROUP * head_dim
    n_groups = n_heads // HEAD_GROUP
    rows = nb * tr
    assert nseq % nb == 0 and t % tr == 0 and tr % GMLP_CHUNK == 0 and rows % SEG_ROWS == 0
    params = _mixer_params(prm, SEG_ROWS, t, GMLP_CHUNK, head_dim)
    return pl.pallas_call(
        functools.partial(_mixer_kernel, n_heads=n_heads, head_dim=head_dim),
        out_shape=(jax.ShapeDtypeStruct((nseq, t, rw + gwid), _F32),
                   jax.ShapeDtypeStruct((nseq, 1, r_cols), _F32),
                   jax.ShapeDtypeStruct((nseq, n_heads, head_dim, head_dim), _F32)),
        grid=(nseq // nb, t // tr),
        in_specs=[pl.BlockSpec((nb, tr, pcols), lambda b, i: (b, i, 0))] + [_full_spec(a) for a in params],
        out_specs=(pl.BlockSpec((nb, tr, rw + gwid), lambda b, i: (b, i, 0)),
                   pl.BlockSpec((nb, 1, r_cols), lambda b, i: (b, 0, 0)),
                   pl.BlockSpec((nb, n_heads, head_dim, head_dim), lambda b, i: (b, 0, 0, 0))),
        scratch_shapes=[pltpu.VMEM((nb, n_groups, gw, gw), _F32), pltpu.VMEM((nb, 1, r_cols), _F32)],
        compiler_params=pltpu.CompilerParams(dimension_semantics=("arbitrary", "arbitrary"),
                                             vmem_limit_bytes=_VMEM_LIMIT),
        name="mixers_fresh",
    )(p, *params)


def _mixer_short_call(p, shift0, wkv0, prm, *, n_heads, head_dim):
    nseq, ts, pcols = p.shape
    nb = WKV_CHUNK // ts
    assert nb * ts == WKV_CHUNK and ts % SUBLANES == 0 and nseq % nb == 0
    rw = n_heads * head_dim
    r_cols = prm["mu"].shape[1]
    gwid = prm["lng"].shape[1]
    params = _mixer_params(prm, WKV_CHUNK, ts, WKV_CHUNK, head_dim)
    seqs = lambda *tail: pl.BlockSpec((nb,) + tail, lambda b, i: (b,) + (0,) * len(tail))
    return pl.pallas_call(
        functools.partial(_mixer_short_kernel, n_heads=n_heads, head_dim=head_dim),
        out_shape=(jax.ShapeDtypeStruct((nseq, ts, rw + gwid), _F32),
                   jax.ShapeDtypeStruct((nseq, 1, r_cols), _F32),
                   jax.ShapeDtypeStruct((nseq, n_heads, head_dim, head_dim), _F32),
                   jax.ShapeDtypeStruct((nseq, ts, gwid), _F32)),
        grid=(nseq // nb, 1),
        in_specs=[seqs(ts, pcols), seqs(1, r_cols), seqs(n_heads, head_dim, head_dim)]
                 + [_full_spec(a) for a in params],
        out_specs=(seqs(ts, rw + gwid), seqs(1, r_cols), seqs(n_heads, head_dim, head_dim), seqs(ts, gwid)),
        compiler_params=pltpu.CompilerParams(dimension_semantics=("arbitrary", "arbitrary"),
                                             vmem_limit_bytes=_VMEM_LIMIT),
        name="mixers_state",
    )(p, shift0, wkv0, *params)


def _pick_tile(n, pref):
    t = min(n, pref)
    while n % t:
        t //= 2
    return t


def kernel(x_prompt, x_sample, state_shift, state_wkv, c_prompt, c_sample, w_ada, b_ada, ffn1_gu, ffn1_dn, w_in, mu_shift, w0, w_lora_up, a0, a_lora_up, g_lora_up, k_k, k_a, r_k, gn_w, gn_b, ln_v_g, ln_v_b, w_s, b_s, w_out, ffn2_gu, ffn2_dn, final_g):
    depth = w_ada.shape[0]
    bp, tp, d = x_prompt.shape
    bs, ts, _ = x_sample.shape
    n_heads, head_dim = r_k.shape[1], r_k.shape[2]
    rw = n_heads * head_dim
    w_rank, a_rank = w_lora_up.shape[1], a_lora_up.shape[1]
    assert w_rank == a_rank and n_heads % HEAD_GROUP == 0 and head_dim == WKV_CHUNK
    d_ff = ffn1_dn.shape[1]
    tf = 256 if d_ff % 256 == 0 else 128

    xp = x_prompt.reshape(bp * tp, d)
    xs = x_sample.reshape(bs * ts, d)
    assert bs % SUBLANES == 0
    n_c = bs + bp
    n_c_pad = -(-n_c // 16) * 16
    c_all = jnp.concatenate([c_sample, c_prompt, jnp.zeros((n_c_pad - n_c, d), _F32)], axis=0)

    tm_p = _pick_tile(tp, 512)
    tr_p = _pick_tile(tp, 256)
    tm_s = ts * _pick_tile(bs, 64)
    nb_p = _pick_tile(bp, 2)

    shp, wkp, shs, wks, cvs = [], [], [], [], []
    for l in range(depth):
        mod = _mod_call(c_all, w_ada[l], b_ada[l])
        tiles_p = dict(tm=tm_p, seq0=bs, seqs_per_tile=1, tiles_per_seq=tp // tm_p, tf=tf, stage=WEIGHT_STAGE_STEPS)
        tiles_s = dict(tm=tm_s, seq0=0, seqs_per_tile=tm_s // ts, tiles_per_seq=1, tf=tf, stage=0)

        zw = jnp.zeros((w_rank, rw), _F32)
        wwa = jnp.concatenate([jnp.concatenate([w_lora_up[l], zw], axis=1),
                               jnp.concatenate([zw, a_lora_up[l]], axis=1)], axis=0).astype(_BF16)
        row = lambda a: a.reshape(1, -1)
        prm = dict(mu=row(mu_shift[l]), w0=row(w0[l]), a0=row(a0[l]), wwa=wwa,
                   wg=g_lora_up[l].astype(_BF16), k_k=row(k_k[l]), k_a=row(k_a[l]), r_k=row(r_k[l]),
                   gn_w=row(gn_w[l]), gn_b=row(gn_b[l]), lng=row(ln_v_g[l]), lnb=row(ln_v_b[l]),
                   w_s=w_s[l], b_s=b_s[l])
        fg = final_g.reshape(1, d)
        final = l == depth - 1

        hp, pp, wgu1, wdn1, win = _ffn_in_call(xp, mod, ffn1_gu[l], ffn1_dn[l], w_in[l], **tiles_p)
        ymp, sh_p, wk_p = _mixer_call(pp.reshape(bp, tp, -1), prm, nb=nb_p, tr=tr_p,
                                      n_heads=n_heads, head_dim=head_dim)
        xp, wout, wgu2, wdn2 = _out_ffn_call(ymp.reshape(bp * tp, -1), hp, mod, w_out[l], ffn2_gu[l], ffn2_dn[l],
                                             fg, final=final, **tiles_p)

        hs, ps = _ffn_in_call(xs, mod, wgu1, wdn1, win, **tiles_s)
        yms, sh_s, wk_s, v_s = _mixer_short_call(ps.reshape(bs, ts, -1), state_shift[l][:, None, :], state_wkv[l], prm,
                                                 n_heads=n_heads, head_dim=head_dim)
        xs, = _out_ffn_call(yms.reshape(bs * ts, -1), hs, mod, wout, wgu2, wdn2, fg, final=final, **tiles_s)

        shp.append(sh_p[:, 0]); wkp.append(wk_p); shs.append(sh_s[:, 0]); wks.append(wk_s); cvs.append(v_s)

    y_prompt = xp.reshape(bp, tp, d)
    y_sample = xs.reshape(bs, ts, d)
    return (y_prompt, y_sample, jnp.stack(shp), jnp.stack(wkp), jnp.stack(shs), jnp.stack(wks), jnp.stack(cvs))
```

```python
import functools
import math

import jax
import jax.numpy as jnp
import numpy as np
from jax import lax
from jax.experimental import pallas as pl
from jax.experimental.pallas import tpu as pltpu

_F32 = jnp.float32
_BF16 = jnp.bfloat16

RMS_EPS = 1e-6
LN_EPS = 1e-5
GN_EPS = 64e-5
N_MOD = 9
WKV_CHUNK = 64
HEAD_GROUP = 4
INV_BLOCK = 16
GMLP_CHUNK = 128
SEG_ROWS = 256
SUBLANES = 8
MOD_PER_STEP = 3
WEIGHT_STAGE_STEPS = 16
DECAY_SCALE = math.exp(-0.5)

_V7X_VMEM_BYTES = 64 * 1024 * 1024
_VMEM_LIMIT = _V7X_VMEM_BYTES - 8 * 1024 * 1024


def _dot(a, b):
    return jnp.dot(a.astype(_BF16), b.astype(_BF16), preferred_element_type=_F32)


def _dot_nt(a, b):
    return lax.dot_general(a.astype(_BF16), b.astype(_BF16), (((1,), (1,)), ((), ())),
                           preferred_element_type=_F32)


def _dot_tn(a, b):
    return lax.dot_general(a.astype(_BF16), b.astype(_BF16), (((0,), (0,)), ((), ())),
                           preferred_element_type=_F32)


def _split_hi_lo(x):
    hi = x.astype(_BF16)
    lo = (x - hi.astype(_F32)).astype(_BF16)
    return hi, lo


def _dot_sel_lhs(sel, x):
    hi, lo = _split_hi_lo(x)
    return (jnp.dot(sel, hi, preferred_element_type=_F32)
            + jnp.dot(sel, lo, preferred_element_type=_F32))


def _div_pow2(x, n):
    assert n & (n - 1) == 0
    return lax.shift_right_logical(x, jnp.int32(n.bit_length() - 1))


def _mod_pow2(x, n):
    assert n & (n - 1) == 0
    return lax.bitwise_and(x, jnp.int32(n - 1))


def _rms(x):
    return x * lax.rsqrt(jnp.mean(x * x, -1, keepdims=True) + RMS_EPS)


def _sigmoid(x):
    return 1.0 / (1.0 + jnp.exp(-x))


def _affine(x, scale, shift=None):
    mb = scale.shape[0]
    if mb == 1:
        y = x * scale
        return y if shift is None else y + shift
    tm, d = x.shape
    y = x.reshape(mb, tm // mb, d) * scale[:, None, :]
    if shift is not None:
        y = y + shift[:, None, :]
    return y.reshape(tm, d)


def _swiglu(n_bf, wgu_ref, wdn_ref, tf):
    d_ff = wdn_ref.shape[0]
    acc = None
    for j in range(d_ff // tf):
        g = jnp.dot(n_bf, wgu_ref[:, j * tf:(j + 1) * tf], preferred_element_type=_F32)
        u = jnp.dot(n_bf, wgu_ref[:, d_ff + j * tf:d_ff + (j + 1) * tf], preferred_element_type=_F32)
        hm = (g * _sigmoid(g) * u).astype(_BF16)
        part = jnp.dot(hm, wdn_ref[j * tf:(j + 1) * tf, :], preferred_element_type=_F32)
        acc = part if acc is None else acc + part
    return acc


def _mod_kernel(c_ref, w_ref, b_ref, o_ref):
    c = c_ref[...]
    s = c * _sigmoid(c)
    o = _dot(s, w_ref[...]) + b_ref[...]
    d = c.shape[1]
    for k in range(o_ref.shape[0]):
        o_ref[k] = o[:, k * d:(k + 1) * d]


def _mod_call(c_all, w_ada, b_ada):
    n, d = c_all.shape
    per_step = MOD_PER_STEP if N_MOD % MOD_PER_STEP == 0 else 1
    return pl.pallas_call(
        _mod_kernel,
        out_shape=jax.ShapeDtypeStruct((N_MOD, n, d), _F32),
        grid=(N_MOD // per_step,),
        in_specs=[pl.BlockSpec((n, d), lambda j: (0, 0)),
                  pl.BlockSpec((d, per_step * d), lambda j: (0, j)),
                  pl.BlockSpec((1, per_step * d), lambda j: (0, j))],
        out_specs=pl.BlockSpec((per_step, n, d), lambda j: (j, 0, 0)),
        compiler_params=pltpu.CompilerParams(dimension_semantics=("arbitrary",),
                                             vmem_limit_bytes=_VMEM_LIMIT),
        name="adaln_mod",
    )(c_all, w_ada, b_ada.reshape(1, N_MOD * d))


def _mod_rows(mod_ref, pick, tile):
    if pick is None:
        return lambda j: mod_ref[j]
    seq0, tiles_per_seq = pick
    row = lax.rem(seq0 + lax.div(tile, jnp.int32(tiles_per_seq)), jnp.int32(SUBLANES))
    return lambda j: mod_ref[j, pl.ds(row, 1), :]


def _mod_spec(mod, seq0, seqs_per_tile, tiles_per_seq, tile):
    n_mod, _, d = mod.shape
    if seqs_per_tile > 1:
        assert seqs_per_tile % SUBLANES == 0 and seq0 % seqs_per_tile == 0 and tiles_per_seq == 1
        first = seq0 // seqs_per_tile
        return pl.BlockSpec((n_mod, seqs_per_tile, d), lambda i: (0, first + tile(i), 0)), None
    return (pl.BlockSpec((n_mod, SUBLANES, d), lambda i: (0, (seq0 + tile(i) // tiles_per_seq) // SUBLANES, 0)),
            (seq0, tiles_per_seq))


def _resident(shape):
    nd = len(shape)
    return pl.BlockSpec(shape, lambda *_: (0,) * nd, pipeline_mode=pl.Buffered(1))


def _dense_call(body, name, groups, mod, weights, extra, out_widths, *, stage, export=False):
    starts, n_steps = [], stage
    for g in groups:
        starts.append(n_steps)
        n_steps += g["acts"][0].shape[0] // g["tm"]
    n_w, n_x, n_out = len(weights), len(extra), len(out_widths)

    def tile_of(k):
        last = groups[k]["acts"][0].shape[0] // groups[k]["tm"] - 1
        return lambda i: jnp.clip(i - starts[k], 0, last)

    in_specs, picks, inputs = [], [], []
    for k, g in enumerate(groups):
        tile = tile_of(k)
        mod_spec, pick = _mod_spec(mod, g["seq0"], g["seqs_per_tile"], g["tiles_per_seq"], tile)
        picks.append(pick)
        in_specs += [pl.BlockSpec((g["tm"], a.shape[1]), lambda i, tile=tile: (tile(i), 0)) for a in g["acts"]]
        in_specs.append(mod_spec)
        inputs += list(g["acts"]) + [mod]
    out_shape, out_specs = [], []
    for k, g in enumerate(groups):
        tile = tile_of(k)
        out_shape += [jax.ShapeDtypeStruct((g["acts"][0].shape[0], w), _F32) for w in out_widths]
        out_specs += [pl.BlockSpec((g["tm"], w), lambda i, tile=tile: (tile(i), 0)) for w in out_widths]
    scratch = []
    if stage:
        assert all(w.shape[0] % (stage * 2 * SUBLANES) == 0 for w in weights)
        chunk_spec = lambda w: pl.BlockSpec((w.shape[0] // stage, w.shape[1]),
                                            lambda i: (jnp.minimum(i, stage - 1), 0))
        in_specs += [chunk_spec(w) for w in weights]
        scratch = [pltpu.VMEM(w.shape, _BF16) for w in weights]
        if export:
            out_shape += [jax.ShapeDtypeStruct(w.shape, _BF16) for w in weights]
            out_specs += [chunk_spec(w) for w in weights]
    else:
        in_specs += [_resident(w.shape) for w in weights]
    in_specs += [_resident(e.shape) for e in extra]

    def kernel(*refs):
        it = iter(refs)
        act_refs, mod_refs = [], []
        for g in groups:
            act_refs.append([next(it) for _ in g["acts"]])
            mod_refs.append(next(it))
        w_in_refs = [next(it) for _ in range(n_w)]
        x_refs = [next(it) for _ in range(n_x)]
        out_refs = [[next(it) for _ in range(n_out)] for _ in groups]
        w_out_refs = [next(it) for _ in range(n_w)] if stage and export else [None] * n_w
        w_refs = [next(it) for _ in range(n_w)] if stage else w_in_refs
        step = pl.program_id(0)

        if stage:
            @pl.when(step < stage)
            def _():
                for src, dst, out in zip(w_in_refs, w_refs, w_out_refs):
                    chunk = src.shape[0]
                    v = src[...].astype(_BF16)
                    dst[pl.ds(pl.multiple_of(step * chunk, chunk), chunk), :] = v
                    if out is not None:
                        out[...] = v

        for k in range(len(groups)):
            end = starts[k + 1] if k + 1 < len(groups) else n_steps

            def run(k=k):
                body(act_refs[k], _mod_rows(mod_refs[k], picks[k], step - starts[k]), w_refs, x_refs, out_refs[k])

            if len(groups) == 1 and not stage:
                run()
            else:
                pl.when((step >= starts[k]) & (step < end))(run)

    outs = pl.pallas_call(
        kernel,
        out_shape=tuple(out_shape),
        grid=(n_steps,),
        in_specs=in_specs,
        out_specs=tuple(out_specs),
        scratch_shapes=scratch,
        compiler_params=pltpu.CompilerParams(dimension_semantics=("arbitrary",),
                                             vmem_limit_bytes=_VMEM_LIMIT),
        name=name,
    )(*inputs, *weights, *extra)
    per_group = [outs[k * n_out:(k + 1) * n_out] for k in range(len(groups))]
    return per_group, list(outs[len(groups) * n_out:])


def _ffn_in_body(acts, mod, weights, extra, outs, *, tf):
    (x_ref,), (wgu_ref, wdn_ref, win_ref), (h_ref, p_ref) = acts, weights, outs
    x = x_ref[...]
    sh1, sc1, g1, sh2, sc2 = (mod(j) for j in range(5))
    n1 = _affine(_rms(x), 1.0 + sc1, sh1).astype(_BF16)
    h = x + _affine(_swiglu(n1, wgu_ref, wdn_ref, tf), 0.5 * g1)
    h_ref[...] = h
    n2 = _affine(_rms(h), 1.0 + sc2, sh2).astype(_BF16)
    p_ref[...] = jnp.dot(n2, win_ref[...], preferred_element_type=_F32)


def _ffn_in_call(groups, mod, wgu, wdn, win, *, tf, stage, export=False):
    d = groups[0]["acts"][0].shape[1]
    return _dense_call(functools.partial(_ffn_in_body, tf=tf), "ffn1_inproj", groups, mod, [wgu, wdn, win], [],
                       [d, win.shape[1]], stage=stage, export=export)


def _out_ffn_body(acts, mod, weights, extra, outs, *, tf, final):
    (ym_ref, h_ref), (wout_ref, wgu_ref, wdn_ref), (fg_ref,), (o_ref,) = acts, weights, extra, outs
    g2, sh3, sc3, g3 = (mod(j) for j in range(5, 9))
    h = h_ref[...] + _affine(jnp.dot(ym_ref[...].astype(_BF16), wout_ref[...],
                                     preferred_element_type=_F32), g2)
    n3 = _affine(_rms(h), 1.0 + sc3, sh3).astype(_BF16)
    h = h + _affine(_swiglu(n3, wgu_ref, wdn_ref, tf), 0.5 * g3)
    if final:
        h = _rms(h) * fg_ref[...]
    o_ref[...] = h


def _out_ffn_call(groups, mod, wout, wgu, wdn, fg, *, tf, final, stage):
    d = wdn.shape[1]
    return _dense_call(functools.partial(_out_ffn_body, tf=tf, final=final), "outproj_ffn2", groups, mod,
                       [wout, wgu, wdn], [fg], [d], stage=stage)[0]


def _rows_bf16(*xs):
    return jnp.concatenate([x.astype(_BF16) for x in xs], axis=0)


def _block_diag(y, mask_bf):
    reps = mask_bf.shape[0] // y.shape[0]
    return jnp.concatenate([y.astype(_BF16)] * reps, axis=0) * mask_bf


def _nilpotent_inverse(n, bd, order):
    every = range(len(n))
    c = n[0].shape[0]
    tm = n
    levels = order.bit_length() - 2
    if levels <= 0:
        return tm
    nb = bd(n)
    npow = [_dot(n[i], nb[i]) for i in every]
    for lvl in range(levels):
        nb = bd(npow)
        if lvl + 1 < levels:
            res = [_dot(jnp.concatenate([tm[i], npow[i]], axis=0), nb[i]) for i in every]
            tm = [tm[i] + npow[i] + res[i][:c] for i in every]
            npow = [res[i][c:] for i in every]
        else:
            tm = [tm[i] + npow[i] + _dot(tm[i], nb[i]) for i in every]
    return tm


def _wkv_local(rt, at, bt, kt, vv, mask_bf, strict, incl, diag):
    n = len(rt)
    c = rt[0].shape[0]
    every = range(n)
    bd = lambda xs: [_block_diag(x, mask_bf) for x in xs]
    x = [_rows_bf16(at[i], rt[i]) for i in every]
    bb, kb = bd(bt), bd(kt)
    xb = [_dot_nt(x[i], bb[i]) for i in every]
    xk = [_dot_nt(x[i], kb[i]) for i in every]
    lab = [xb[i][:c] * strict for i in every]
    mrb = [xb[i][c:] * incl for i in every]
    lak = [xk[i][:c] * strict for i in every]
    mrk = [xk[i][c:] * incl for i in every]
    vb = bd(vv)
    wy = [_dot(jnp.concatenate([lak[i], mrk[i]], axis=0), vb[i]) for i in every]
    w1 = [wy[i][:c] for i in every]
    y0 = [wy[i][c:] for i in every]
    if isinstance(diag, int):
        tm = _nilpotent_inverse(lab, bd, diag)
    else:
        assert c == 4 * INV_BLOCK
        nd = [lab[i] * diag for i in every]
        noff = [lab[i] - nd[i] for i in every]
        tdm = _nilpotent_inverse(nd, bd, INV_BLOCK)
        noffb = bd(noff)
        m1 = [noff[i] + _dot(tdm[i], noffb[i]) for i in every]
        m1b = bd(m1)
        m2 = [_dot(m1[i], m1b[i]) for i in every]
        m2b = bd(m2)
        q = [m1[i] + m2[i] + _dot(m1[i], m2b[i]) for i in every]
        tdmb = bd(tdm)
        tm = [q[i] + tdm[i] + _dot(q[i], tdmb[i]) for i in every]
    atb = bd(at)
    ap = [at[i] + _dot(tm[i], atb[i]) for i in every]
    w1b = bd(w1)
    u0 = [w1[i] + _dot(tm[i], w1b[i]) for i in every]
    return [(ap[i], u0[i], mrb[i], y0[i]) for i in every]


def _wkv_state_step(ap, rt, u0, vv, bdec, kdec, pc, st, mask):
    every = range(len(rt))
    c = rt[0].shape[0]
    ar = [_dot_nt(_rows_bf16(ap[i], rt[i]), st[i]) for i in every]
    u = [ar[i][:c] + u0[i] for i in every]
    upd = [_dot_tn(_rows_bf16(u[i], vv[i]), _rows_bf16(bdec[i], kdec[i])) for i in every]
    st_new = [st[i] * pc[i] + upd[i] * mask for i in every]
    return u, [ar[i][c:] for i in every], st_new


def _token_features(p, first, trow, prm, segsum, rw):
    mu_ref, w0_ref, a0_ref, wwa_ref, wg_ref, kk_ref, ka_ref = prm
    r_cols = mu_ref.shape[1]
    pr = p[:, :r_cols]
    prev = jnp.where(trow == 0, first, pltpu.roll(pr, 1, 0))
    xm = pr + (prev - pr) * mu_ref[...]
    r = xm[:, 0:rw]
    k = xm[:, rw:2 * rw]
    v = xm[:, 2 * rw:3 * rw]
    lo_w = wwa_ref.shape[0]
    x_lo = xm[:, 3 * rw:3 * rw + lo_w]
    gd = xm[:, 3 * rw + lo_w:r_cols]
    lane = lax.broadcasted_iota(jnp.int32, (1, lo_w), 1)
    x_lo = jnp.where(lane < lo_w // 2, jnp.tanh(x_lo), x_lo)
    wa = _dot(x_lo, wwa_ref[...])
    wpre = w0_ref[...] + wa[:, :rw]
    lw = -DECAY_SCALE * _sigmoid(wpre)
    a = _sigmoid(a0_ref[...] + wa[:, rw:])
    gate = _dot(_sigmoid(gd), wg_ref[...])
    kkv = k * kk_ref[...]
    kkn = kkv * lax.rsqrt(jnp.maximum(segsum(kkv * kkv), 1e-24))
    k2 = k * (1.0 + (a - 1.0) * ka_ref[...])
    return dict(r=r, v=v, lw=lw, a=a, gate=gate, kkn=kkn, k2=k2)


def _decay_factors(f, cum, tot):
    pinc = jnp.exp(cum)
    pinv = jnp.exp(-cum)
    pend = jnp.exp(tot - cum)
    beta = f["kkn"] * f["a"]
    ops = dict(rt=f["r"] * pinc, at=-f["kkn"] * jnp.exp(cum - f["lw"]), bt=beta * pinv, kt=f["k2"] * pinv,
               bdec=beta * pend, kdec=f["k2"] * pend, v=f["v"])
    return dict({name: x.astype(_BF16) for name, x in ops.items()}, pinc=pinc)


def _rwkv_output(yw, f, rk_ref, gnw_ref, gnb_ref, segsum, head_dim):
    inv_n = 1.0 / head_dim
    mean = segsum(yw) * inv_n
    dev = yw - mean
    var = segsum(dev * dev) * inv_n
    yn = dev * lax.rsqrt(var + GN_EPS) * gnw_ref[...] + gnb_ref[...]
    bonus = segsum(f["r"] * f["k2"] * rk_ref[...]) * f["v"]
    return (yn + bonus) * f["gate"]


def _gmlp_branch(p, r_cols, lng_ref, lnb_ref, ws2_ref, tril2, bsf_ref):
    gwid = lng_ref.shape[1]
    ct = ws2_ref.shape[1]
    rows = p.shape[0]
    pu = p[:, r_cols:r_cols + gwid]
    pv = p[:, r_cols + gwid:r_cols + 2 * gwid]
    pm = jnp.mean(pv, -1, keepdims=True)
    pd = pv - pm
    pvar = jnp.mean(pd * pd, -1, keepdims=True)
    vn = pd * lax.rsqrt(pvar + LN_EPS) * lng_ref[...] + lnb_ref[...]
    n_pair = ws2_ref.shape[0]
    pw = gwid // n_pair
    first_half = lax.broadcasted_iota(jnp.int32, (1, pw), 1) < pw // 2
    ws_tril = [(ws2_ref[j] * tril2).astype(_BF16) for j in range(n_pair)]
    mixed_rows = []
    for q in range(rows // ct):
        outs = []
        for j in range(n_pair):
            vp = vn[q * ct:(q + 1) * ct, j * pw:(j + 1) * pw]
            rhs = jnp.concatenate([jnp.where(first_half, vp, 0.0), jnp.where(first_half, 0.0, vp)], axis=0)
            outs.append(_dot(ws_tril[j], rhs))
        mixed_rows.append(jnp.concatenate(outs, axis=1) + bsf_ref[...])
    mixed = mixed_rows[0] if len(mixed_rows) == 1 else jnp.concatenate(mixed_rows, axis=0)
    return pu * mixed, vn


def _state_from_heads(wkv_ref, s, g, head_dim):
    zero = jnp.zeros((head_dim, head_dim), _F32)
    blocks = []
    for h in range(HEAD_GROUP):
        blk = wkv_ref[s, g * HEAD_GROUP + h]
        blocks.append(jnp.concatenate([zero] * h + [blk] + [zero] * (HEAD_GROUP - 1 - h), axis=1))
    return jnp.concatenate(blocks, axis=0)


def _state_to_heads(wkv_ref, s, g, st, head_dim):
    for h in range(HEAD_GROUP):
        wkv_ref[s, g * HEAD_GROUP + h] = st[h * head_dim:(h + 1) * head_dim, h * head_dim:(h + 1) * head_dim]


def _mixer_kernel(p_ref, mu_ref, w0_ref, a0_ref, wwa_ref, wg_ref, kk_ref, ka_ref, rk_ref, gnw_ref, gnb_ref,
                  lng_ref, lnb_ref, ws2_ref, bsf_ref, hmask_ref, hmaskb_ref, tmask_ref, segm_ref, tril2_ref,
                  y_ref, shift_ref, wkv_ref, st_s, carry_s, *, n_heads, head_dim):
    nb, tr, _ = p_ref.shape
    c = WKV_CHUNK
    gw = HEAD_GROUP * head_dim
    n_groups = n_heads // HEAD_GROUP
    rw = n_heads * head_dim
    rows = nb * tr
    ti = pl.program_id(1)
    n_t = pl.num_programs(1)
    r_cols = mu_ref.shape[1]

    @pl.when(ti == 0)
    def _():
        carry_s[...] = jnp.zeros_like(carry_s)
        st_s[...] = jnp.zeros_like(st_s)

    mask = hmask_ref[...]
    mask_bf = hmaskb_ref[...]
    strict, incl, diag = tmask_ref[0], tmask_ref[1], tmask_ref[2]
    tri_bf = segm_ref[...]

    def segsum(x):
        return jnp.concatenate([_dot(x[:, g * gw:(g + 1) * gw], mask_bf) for g in range(n_groups)], axis=1)

    p = p_ref[0] if nb == 1 else jnp.concatenate([p_ref[s] for s in range(nb)], axis=0)
    trow = _mod_pow2(lax.broadcasted_iota(jnp.int32, (rows, 1), 0), tr)
    if nb == 1:
        first = carry_s[0]
    else:
        first = jnp.concatenate([jnp.broadcast_to(carry_s[s], (tr, r_cols)) for s in range(nb)], axis=0)
    for s in range(nb):
        carry_s[s] = p_ref[s, tr - 1:tr, :r_cols]
    f = _token_features(p, first, trow, (mu_ref, w0_ref, a0_ref, wwa_ref, wg_ref, kk_ref, ka_ref), segsum, rw)
    sr = tri_bf.shape[0]
    cum = jnp.concatenate([_dot_sel_lhs(tri_bf, f["lw"][i * sr:(i + 1) * sr]) for i in range(rows // sr)],
                          axis=0)
    tot = jnp.concatenate([jnp.broadcast_to(cum[j * c + c - 1:j * c + c], (c, rw)) for j in range(rows // c)],
                          axis=0)
    d = _decay_factors(f, cum, tot)

    n_chunks = rows // c
    chunks_per_seq = tr // c

    def blk(x, j, g):
        return x[j * c:(j + 1) * c, g * gw:(g + 1) * gw]

    probs = [(j, g) for j in range(n_chunks) for g in range(n_groups)]
    cut = lambda x, ps: [blk(x, j, g) for j, g in ps]
    local = dict(zip(probs, _wkv_local(cut(d["rt"], probs), cut(d["at"], probs), cut(d["bt"], probs),
                                       cut(d["kt"], probs), cut(d["v"], probs), mask_bf, strict, incl, diag)))
    y_blk = {}
    for jj in range(chunks_per_seq):
        ps = [(s * chunks_per_seq + jj, g) for s in range(nb) for g in range(n_groups)]
        loc = [local[pr_] for pr_ in ps]
        sts = [st_s[j // chunks_per_seq, g] for j, g in ps]
        pcs = [blk(d["pinc"], j, g)[c - 1:c] for j, g in ps]
        u, ys, st_new = _wkv_state_step([l[0] for l in loc], cut(d["rt"], ps), [l[1] for l in loc],
                                        cut(d["v"], ps), cut(d["bdec"], ps), cut(d["kdec"], ps), pcs, sts, mask)
        for i, (j, g) in enumerate(ps):
            st_s[j // chunks_per_seq, g] = st_new[i]
            y_blk[j, g] = ys[i] + _dot(loc[i][2], _block_diag(u[i], mask_bf)) + loc[i][3]
    y_rows = [jnp.concatenate([y_blk[j, g] for g in range(n_groups)], axis=1) for j in range(n_chunks)]
    yw = y_rows[0] if n_chunks == 1 else jnp.concatenate(y_rows, axis=0)

    y_r = _rwkv_output(yw, f, rk_ref, gnw_ref, gnb_ref, segsum, head_dim)
    y_c, _ = _gmlp_branch(p, r_cols, lng_ref, lnb_ref, ws2_ref, tril2_ref[...], bsf_ref)
    y_all = jnp.concatenate([y_r, y_c], axis=1)
    for s in range(nb):
        y_ref[s] = y_all[s * tr:(s + 1) * tr]

    @pl.when(ti == n_t - 1)
    def _():
        shift_ref[...] = carry_s[...]
        for s in range(nb):
            for g in range(n_groups):
                _state_to_heads(wkv_ref, s, g, st_s[s, g], head_dim)


def _mixer_short_kernel(p_ref, shift0_ref, wkv0_ref, mu_ref, w0_ref, a0_ref, wwa_ref, wg_ref, kk_ref, ka_ref,
                        rk_ref, gnw_ref, gnb_ref, lng_ref, lnb_ref, ws2_ref, bsf_ref, hmask_ref, hmaskb_ref,
                        tmask_ref, segm_ref, tril2_ref, y_ref, shift_ref, wkv_ref, vn_ref, *, n_heads, head_dim):
    nb, ts, _ = p_ref.shape
    c = WKV_CHUNK
    assert nb * ts == c
    gw = HEAD_GROUP * head_dim
    n_groups = n_heads // HEAD_GROUP
    rw = n_heads * head_dim
    r_cols = mu_ref.shape[1]

    mask = hmask_ref[...]
    mask_bf = hmaskb_ref[...]
    strict, incl = tmask_ref[0], tmask_ref[1]

    def segsum(x):
        return jnp.concatenate([_dot(x[:, g * gw:(g + 1) * gw], mask_bf) for g in range(n_groups)], axis=1)

    p = p_ref[...].reshape(c, p_ref.shape[2])
    trow = _mod_pow2(lax.broadcasted_iota(jnp.int32, (c, 1), 0), ts)
    first = jnp.concatenate([jnp.broadcast_to(shift0_ref[s], (ts, r_cols)) for s in range(nb)], axis=0)
    for s in range(nb):
        shift_ref[s] = p_ref[s, ts - 1:ts, :r_cols]
    f = _token_features(p, first, trow, (mu_ref, w0_ref, a0_ref, wwa_ref, wg_ref, kk_ref, ka_ref), segsum, rw)
    cum = _dot_sel_lhs(segm_ref[...], f["lw"])
    tot = jnp.concatenate([jnp.broadcast_to(cum[s * ts + ts - 1:(s + 1) * ts], (ts, rw)) for s in range(nb)],
                          axis=0)
    d = _decay_factors(f, cum, tot)

    groups = range(n_groups)
    gcut = lambda x: [x[:, g * gw:(g + 1) * gw] for g in groups]
    local = _wkv_local(gcut(d["rt"]), gcut(d["at"]), gcut(d["bt"]), gcut(d["kt"]), gcut(d["v"]), mask_bf,
                       strict, incl, ts)

    ps = [(s, g) for s in range(nb) for g in groups]
    scut = lambda x: [x[s * ts:(s + 1) * ts, g * gw:(g + 1) * gw] for s, g in ps]
    lcut = lambda idx: [local[g][idx][s * ts:(s + 1) * ts] for s, g in ps]
    sts = [_state_from_heads(wkv0_ref, s, g, head_dim) for s, g in ps]
    pcs = [d["pinc"][s * ts + ts - 1:(s + 1) * ts, g * gw:(g + 1) * gw] for s, g in ps]
    u, ys, st_new = _wkv_state_step(lcut(0), scut(d["rt"]), lcut(1), scut(d["v"]), scut(d["bdec"]),
                                    scut(d["kdec"]), pcs, sts, mask)
    for i, (s, g) in enumerate(ps):
        _state_to_heads(wkv_ref, s, g, st_new[i], head_dim)
    rows_of = lambda xs, g: jnp.concatenate([xs[i] for i, (s, gg) in enumerate(ps) if gg == g], axis=0)
    yw = jnp.concatenate([rows_of(ys, g) + _dot(local[g][2], _block_diag(rows_of(u, g), mask_bf)) + local[g][3]
                          for g in groups], axis=1)

    y_r = _rwkv_output(yw, f, rk_ref, gnw_ref, gnb_ref, segsum, head_dim)
    y_c, vn = _gmlp_branch(p, r_cols, lng_ref, lnb_ref, ws2_ref, tril2_ref[...], bsf_ref)
    y_ref[...] = jnp.concatenate([y_r, y_c], axis=1).reshape(nb, ts, -1)
    vn_ref[...] = vn.reshape(nb, ts, -1)


def _mixer_masks(rows, seq, ct, head_dim):
    c, gw = WKV_CHUNK, HEAD_GROUP * head_dim
    blk = np.arange(gw) // head_dim
    hmask = (blk[:, None] == blk[None, :]).astype(np.float32)
    t = np.arange(c)[:, None]
    s = np.arange(HEAD_GROUP * c)[None, :] % c
    strict = (s < t) & (s // seq == t // seq)
    incl = (s <= t) & (s // seq == t // seq)
    tmask = np.stack([strict, incl, strict & (s // INV_BLOCK == t // INV_BLOCK)]).astype(np.float32)
    r = np.arange(rows)
    seg = min(c, seq)
    same = (r[:, None] // seg) == (r[None, :] // seg)
    segm = (same & (r[None, :] <= r[:, None])).astype(np.float32)
    ctm = np.arange(ct)
    tril1 = ((ctm[None, :] <= ctm[:, None]) & (ctm[None, :] // seq == ctm[:, None] // seq))
    tril2 = np.concatenate([tril1, tril1], axis=1).astype(np.float32)
    return [jnp.asarray(hmask), jnp.asarray(hmask, _BF16), jnp.asarray(tmask), jnp.asarray(segm, _BF16),
            jnp.asarray(tril2)]


def _gmlp_weights(w_s, b_s, ct, seq, gwid):
    hc = w_s.shape[0]
    n_seq = ct // min(seq, ct)
    ws = w_s[:, :ct // n_seq, :ct // n_seq]
    bs = b_s[:, :ct // n_seq]
    if n_seq > 1:
        ws = jnp.einsum("st,hij->hsitj", jnp.eye(n_seq, dtype=ws.dtype), ws).reshape(hc, ct, ct)
        bs = jnp.tile(bs, (1, n_seq))
    ws2 = ws.reshape(hc // 2, 2, ct, ct).transpose(0, 2, 1, 3).reshape(hc // 2, ct, 2 * ct)
    return ws2, jnp.repeat(bs.T, gwid // hc, axis=1)


def _mixer_params(prm, rows, seq, ct, head_dim):
    gwid = prm["lng"].shape[1]
    ws2, bsf = _gmlp_weights(prm["w_s"], prm["b_s"], ct, seq, gwid)
    params = [prm["mu"], prm["w0"], prm["a0"], prm["wwa"], prm["wg"], prm["k_k"], prm["k_a"], prm["r_k"],
              prm["gn_w"], prm["gn_b"], prm["lng"], prm["lnb"], ws2, bsf]
    return params + _mixer_masks(rows, seq, ct, head_dim)


def _full_spec(a):
    nd = a.ndim
    return pl.BlockSpec(a.shape, lambda b, i: (0,) * nd)


def _mixer_call(p, prm, *, nb, tr, n_heads, head_dim):
    nseq, t, pcols = p.shape
    rw = n_heads * head_dim
    r_cols = prm["mu"].shape[1]
    gwid = prm["lng"].shape[1]
    gw = HEAD_GROUP * head_dim
    n_groups = n_heads // HEAD_GROUP
    rows = nb * tr
    assert nseq % nb == 0 and t % tr == 0 and tr % GMLP_CHUNK == 0 and rows % SEG_ROWS == 0
    params = _mixer_params(prm, SEG_ROWS, t, GMLP_CHUNK, head_dim)
    return pl.pallas_call(
        functools.partial(_mixer_kernel, n_heads=n_heads, head_dim=head_dim),
        out_shape=(jax.ShapeDtypeStruct((nseq, t, rw + gwid), _F32),
                   jax.ShapeDtypeStruct((nseq, 1, r_cols), _F32),
                   jax.ShapeDtypeStruct((nseq, n_heads, head_dim, head_dim), _F32)),
        grid=(nseq // nb, t // tr),
        in_specs=[pl.BlockSpec((nb, tr, pcols), lambda b, i: (b, i, 0))] + [_full_spec(a) for a in params],
        out_specs=(pl.BlockSpec((nb, tr, rw + gwid), lambda b, i: (b, i, 0)),
                   pl.BlockSpec((nb, 1, r_cols), lambda b, i: (b, 0, 0)),
                   pl.BlockSpec((nb, n_heads, head_dim, head_dim), lambda b, i: (b, 0, 0, 0))),
        scratch_shapes=[pltpu.VMEM((nb, n_groups, gw, gw), _F32), pltpu.VMEM((nb, 1, r_cols), _F32)],
        compiler_params=pltpu.CompilerParams(dimension_semantics=("arbitrary", "arbitrary"),
                                             vmem_limit_bytes=_VMEM_LIMIT),
        name="mixers_fresh",
    )(p, *params)


def _mixer_short_call(p, shift0, wkv0, prm, *, n_heads, head_dim):
    nseq, ts, pcols = p.shape
    nb = WKV_CHUNK // ts
    assert nb * ts == WKV_CHUNK and ts % SUBLANES == 0 and nseq % nb == 0
    rw = n_heads * head_dim
    r_cols = prm["mu"].shape[1]
    gwid = prm["lng"].shape[1]
    params = _mixer_params(prm, WKV_CHUNK, ts, WKV_CHUNK, head_dim)
    seqs = lambda *tail: pl.BlockSpec((nb,) + tail, lambda b, i: (b,) + (0,) * len(tail))
    return pl.pallas_call(
        functools.partial(_mixer_short_kernel, n_heads=n_heads, head_dim=head_dim),
        out_shape=(jax.ShapeDtypeStruct((nseq, ts, rw + gwid), _F32),
                   jax.ShapeDtypeStruct((nseq, 1, r_cols), _F32),
                   jax.ShapeDtypeStruct((nseq, n_heads, head_dim, head_dim), _F32),
                   jax.ShapeDtypeStruct((nseq, ts, gwid), _F32)),
        grid=(nseq // nb, 1),
        in_specs=[seqs(ts, pcols), seqs(1, r_cols), seqs(n_heads, head_dim, head_dim)]
                 + [_full_spec(a) for a in params],
        out_specs=(seqs(ts, rw + gwid), seqs(1, r_cols), seqs(n_heads, head_dim, head_dim), seqs(ts, gwid)),
        compiler_params=pltpu.CompilerParams(dimension_semantics=("arbitrary", "arbitrary"),
                                             vmem_limit_bytes=_VMEM_LIMIT),
        name="mixers_state",
    )(p, shift0, wkv0, *params)


def _pick_tile(n, pref):
    t = min(n, pref)
    while n % t:
        t //= 2
    return t


def kernel(x_prompt, x_sample, state_shift, state_wkv, c_prompt, c_sample, w_ada, b_ada, ffn1_gu, ffn1_dn, w_in, mu_shift, w0, w_lora_up, a0, a_lora_up, g_lora_up, k_k, k_a, r_k, gn_w, gn_b, ln_v_g, ln_v_b, w_s, b_s, w_out, ffn2_gu, ffn2_dn, final_g):
    depth = w_ada.shape[0]
    bp, tp, d = x_prompt.shape
    bs, ts, _ = x_sample.shape
    n_heads, head_dim = r_k.shape[1], r_k.shape[2]
    rw = n_heads * head_dim
    w_rank, a_rank = w_lora_up.shape[1], a_lora_up.shape[1]
    assert w_rank == a_rank and n_heads % HEAD_GROUP == 0 and head_dim == WKV_CHUNK
    d_ff = ffn1_dn.shape[1]
    tf = 256 if d_ff % 256 == 0 else 128

    xp = x_prompt.reshape(bp * tp, d)
    xs = x_sample.reshape(bs * ts, d)
    assert bs % SUBLANES == 0
    n_c = bs + bp
    n_c_pad = -(-n_c // 16) * 16
    c_all = jnp.concatenate([c_sample, c_prompt, jnp.zeros((n_c_pad - n_c, d), _F32)], axis=0)

    tm_p = _pick_tile(tp, 512)
    tr_p = _pick_tile(tp, 256)
    tm_s = ts * _pick_tile(bs, 64)
    nb_p = _pick_tile(bp, 2)

    shp, wkp, shs, wks, cvs = [], [], [], [], []
    for l in range(depth):
        mod = _mod_call(c_all, w_ada[l], b_ada[l])
        grp_p = dict(tm=tm_p, seq0=bs, seqs_per_tile=1, tiles_per_seq=tp // tm_p)
        grp_s = dict(tm=tm_s, seq0=0, seqs_per_tile=tm_s // ts, tiles_per_seq=1)

        zw = jnp.zeros((w_rank, rw), _F32)
        wwa = jnp.concatenate([jnp.concatenate([w_lora_up[l], zw], axis=1),
                               jnp.concatenate([zw, a_lora_up[l]], axis=1)], axis=0).astype(_BF16)
        row = lambda a: a.reshape(1, -1)
        prm = dict(mu=row(mu_shift[l]), w0=row(w0[l]), a0=row(a0[l]), wwa=wwa,
                   wg=g_lora_up[l].astype(_BF16), k_k=row(k_k[l]), k_a=row(k_a[l]), r_k=row(r_k[l]),
                   gn_w=row(gn_w[l]), gn_b=row(gn_b[l]), lng=row(ln_v_g[l]), lnb=row(ln_v_b[l]),
                   w_s=w_s[l], b_s=b_s[l])
        fg = final_g.reshape(1, d)
        final = l == depth - 1

        ((hp, pp),), w1_bf = _ffn_in_call([dict(grp_p, acts=[xp])], mod, ffn1_gu[l], ffn1_dn[l], w_in[l],
                                          tf=tf, stage=WEIGHT_STAGE_STEPS, export=True)
        ((hs, ps),), _ = _ffn_in_call([dict(grp_s, acts=[xs])], mod, *w1_bf, tf=tf, stage=0)
        ymp, sh_p, wk_p = _mixer_call(pp.reshape(bp, tp, -1), prm, nb=nb_p, tr=tr_p,
                                      n_heads=n_heads, head_dim=head_dim)
        yms, sh_s, wk_s, v_s = _mixer_short_call(ps.reshape(bs, ts, -1), state_shift[l][:, None, :], state_wkv[l], prm,
                                                 n_heads=n_heads, head_dim=head_dim)
        half_s = dict(grp_s, tm=tm_s // 2, seqs_per_tile=tm_s // ts // 2)
        (xp,), (xs,) = _out_ffn_call([dict(grp_p, acts=[ymp.reshape(bp * tp, -1), hp]),
                                      dict(half_s, acts=[yms.reshape(bs * ts, -1), hs])], mod,
                                     w_out[l], ffn2_gu[l], ffn2_dn[l], fg, tf=tf, final=final,
                                     stage=WEIGHT_STAGE_STEPS)

        shp.append(sh_p[:, 0]); wkp.append(wk_p); shs.append(sh_s[:, 0]); wks.append(wk_s); cvs.append(v_s)

    y_prompt = xp.reshape(bp, tp, d)
    y_sample = xs.reshape(bs, ts, d)
    return (y_prompt, y_sample, jnp.stack(shp), jnp.stack(wkp), jnp.stack(shs), jnp.stack(wks), jnp.stack(cvs))
```

```python
import functools
import math

import jax
import jax.numpy as jnp
import numpy as np
from jax import lax
from jax.experimental import pallas as pl
from jax.experimental.pallas import tpu as pltpu

_F32 = jnp.float32
_BF16 = jnp.bfloat16

RMS_EPS = 1e-6
LN_EPS = 1e-5
GN_EPS = 64e-5
N_MOD = 9
WKV_CHUNK = 64
HEAD_GROUP = 4
INV_BLOCK = 16
GMLP_CHUNK = 128
SEG_ROWS = 256
SUBLANES = 8
MOD_PER_STEP = 3
WEIGHT_STAGE_STEPS = 16
DECAY_SCALE = math.exp(-0.5)

_V7X_VMEM_BYTES = 64 * 1024 * 1024
_VMEM_LIMIT = _V7X_VMEM_BYTES - 8 * 1024 * 1024


def _dot(a, b):
    return jnp.dot(a.astype(_BF16), b.astype(_BF16), preferred_element_type=_F32)


def _dot_nt(a, b):
    return lax.dot_general(a.astype(_BF16), b.astype(_BF16), (((1,), (1,)), ((), ())),
                           preferred_element_type=_F32)


def _dot_tn(a, b):
    return lax.dot_general(a.astype(_BF16), b.astype(_BF16), (((0,), (0,)), ((), ())),
                           preferred_element_type=_F32)


def _split_hi_lo(x):
    hi = x.astype(_BF16)
    lo = (x - hi.astype(_F32)).astype(_BF16)
    return hi, lo


def _dot_sel_lhs(sel, x):
    hi, lo = _split_hi_lo(x)
    return (jnp.dot(sel, hi, preferred_element_type=_F32)
            + jnp.dot(sel, lo, preferred_element_type=_F32))


def _mod_pow2(x, n):
    assert n & (n - 1) == 0
    return lax.bitwise_and(x, jnp.int32(n - 1))


def _rms(x):
    return x * lax.rsqrt(jnp.mean(x * x, -1, keepdims=True) + RMS_EPS)


def _sigmoid(x):
    return 1.0 / (1.0 + jnp.exp(-x))


def _affine(x, scale, shift=None):
    mb = scale.shape[0]
    if mb == 1:
        y = x * scale
        return y if shift is None else y + shift
    tm, d = x.shape
    y = x.reshape(mb, tm // mb, d) * scale[:, None, :]
    if shift is not None:
        y = y + shift[:, None, :]
    return y.reshape(tm, d)


def _swiglu(n_bf, wgu_ref, wdn_ref, tf):
    d_ff = wdn_ref.shape[0]
    acc = None
    for j in range(d_ff // tf):
        g = jnp.dot(n_bf, wgu_ref[:, j * tf:(j + 1) * tf], preferred_element_type=_F32)
        u = jnp.dot(n_bf, wgu_ref[:, d_ff + j * tf:d_ff + (j + 1) * tf], preferred_element_type=_F32)
        hm = (g * _sigmoid(g) * u).astype(_BF16)
        part = jnp.dot(hm, wdn_ref[j * tf:(j + 1) * tf, :], preferred_element_type=_F32)
        acc = part if acc is None else acc + part
    return acc


def _mod_kernel(c_ref, w_ref, b_ref, o_ref):
    c = c_ref[...]
    s = c * _sigmoid(c)
    o = _dot(s, w_ref[...]) + b_ref[...]
    d = c.shape[1]
    for k in range(o_ref.shape[0]):
        o_ref[k] = o[:, k * d:(k + 1) * d]


def _mod_call(c_all, w_ada, b_ada):
    n, d = c_all.shape
    per_step = MOD_PER_STEP if N_MOD % MOD_PER_STEP == 0 else 1
    return pl.pallas_call(
        _mod_kernel,
        out_shape=jax.ShapeDtypeStruct((N_MOD, n, d), _F32),
        grid=(N_MOD // per_step,),
        in_specs=[pl.BlockSpec((n, d), lambda j: (0, 0)),
                  pl.BlockSpec((d, per_step * d), lambda j: (0, j)),
                  pl.BlockSpec((1, per_step * d), lambda j: (0, j))],
        out_specs=pl.BlockSpec((per_step, n, d), lambda j: (j, 0, 0)),
        compiler_params=pltpu.CompilerParams(dimension_semantics=("arbitrary",),
                                             vmem_limit_bytes=_VMEM_LIMIT),
        name="adaln_mod",
    )(c_all, w_ada, b_ada.reshape(1, N_MOD * d))


def _mod_rows(mod_ref, pick, tile):
    if pick is None:
        return lambda j: mod_ref[j]
    seq0, tiles_per_seq = pick
    row = lax.rem(seq0 + lax.div(tile, jnp.int32(tiles_per_seq)), jnp.int32(SUBLANES))
    return lambda j: mod_ref[j, pl.ds(row, 1), :]


def _mod_spec(mod, seq0, seqs_per_tile, tiles_per_seq, tile):
    n_mod, _, d = mod.shape
    if seqs_per_tile > 1:
        assert seqs_per_tile % SUBLANES == 0 and seq0 % seqs_per_tile == 0 and tiles_per_seq == 1
        first = seq0 // seqs_per_tile
        return pl.BlockSpec((n_mod, seqs_per_tile, d), lambda i: (0, first + tile(i), 0)), None
    return (pl.BlockSpec((n_mod, SUBLANES, d), lambda i: (0, (seq0 + tile(i) // tiles_per_seq) // SUBLANES, 0)),
            (seq0, tiles_per_seq))


def _resident(shape):
    nd = len(shape)
    return pl.BlockSpec(shape, lambda *_: (0,) * nd, pipeline_mode=pl.Buffered(1))


def _dense_call(body, name, groups, mod, weights, extra, out_widths, *, stage, export=False):
    starts, n_steps = [], stage
    for g in groups:
        starts.append(n_steps)
        n_steps += g["acts"][0].shape[0] // g["tm"]
    n_w, n_x, n_out = len(weights), len(extra), len(out_widths)

    def tile_of(k):
        last = groups[k]["acts"][0].shape[0] // groups[k]["tm"] - 1
        return lambda i: jnp.clip(i - starts[k], 0, last)

    in_specs, picks, inputs = [], [], []
    for k, g in enumerate(groups):
        tile = tile_of(k)
        mod_spec, pick = _mod_spec(mod, g["seq0"], g["seqs_per_tile"], g["tiles_per_seq"], tile)
        picks.append(pick)
        in_specs += [pl.BlockSpec((g["tm"], a.shape[1]), lambda i, tile=tile: (tile(i), 0)) for a in g["acts"]]
        in_specs.append(mod_spec)
        inputs += list(g["acts"]) + [mod]
    out_shape, out_specs = [], []
    for k, g in enumerate(groups):
        tile = tile_of(k)
        out_shape += [jax.ShapeDtypeStruct((g["acts"][0].shape[0], w), _F32) for w in out_widths]
        out_specs += [pl.BlockSpec((g["tm"], w), lambda i, tile=tile: (tile(i), 0)) for w in out_widths]
    scratch = []
    if stage:
        assert all(w.shape[0] % (stage * 2 * SUBLANES) == 0 for w in weights)
        chunk_spec = lambda w: pl.BlockSpec((w.shape[0] // stage, w.shape[1]),
                                            lambda i: (jnp.minimum(i, stage - 1), 0))
        in_specs += [chunk_spec(w) for w in weights]
        scratch = [pltpu.VMEM(w.shape, _BF16) for w in weights]
        if export:
            out_shape += [jax.ShapeDtypeStruct(w.shape, _BF16) for w in weights]
            out_specs += [chunk_spec(w) for w in weights]
    else:
        in_specs += [_resident(w.shape) for w in weights]
    in_specs += [_resident(e.shape) for e in extra]

    def kernel(*refs):
        it = iter(refs)
        act_refs, mod_refs = [], []
        for g in groups:
            act_refs.append([next(it) for _ in g["acts"]])
            mod_refs.append(next(it))
        w_in_refs = [next(it) for _ in range(n_w)]
        x_refs = [next(it) for _ in range(n_x)]
        out_refs = [[next(it) for _ in range(n_out)] for _ in groups]
        w_out_refs = [next(it) for _ in range(n_w)] if stage and export else [None] * n_w
        w_refs = [next(it) for _ in range(n_w)] if stage else w_in_refs
        step = pl.program_id(0)

        if stage:
            @pl.when(step < stage)
            def _():
                for src, dst, out in zip(w_in_refs, w_refs, w_out_refs):
                    chunk = src.shape[0]
                    v = src[...].astype(_BF16)
                    dst[pl.ds(pl.multiple_of(step * chunk, chunk), chunk), :] = v
                    if out is not None:
                        out[...] = v

        for k in range(len(groups)):
            end = starts[k + 1] if k + 1 < len(groups) else n_steps

            def run(k=k):
                body(act_refs[k], _mod_rows(mod_refs[k], picks[k], step - starts[k]), w_refs, x_refs, out_refs[k])

            if len(groups) == 1 and not stage:
                run()
            else:
                pl.when((step >= starts[k]) & (step < end))(run)

    outs = pl.pallas_call(
        kernel,
        out_shape=tuple(out_shape),
        grid=(n_steps,),
        in_specs=in_specs,
        out_specs=tuple(out_specs),
        scratch_shapes=scratch,
        compiler_params=pltpu.CompilerParams(dimension_semantics=("arbitrary",),
                                             vmem_limit_bytes=_VMEM_LIMIT),
        name=name,
    )(*inputs, *weights, *extra)
    per_group = [outs[k * n_out:(k + 1) * n_out] for k in range(len(groups))]
    return per_group, list(outs[len(groups) * n_out:])


def _ffn_in_body(acts, mod, weights, extra, outs, *, tf):
    (x_ref,), (wgu_ref, wdn_ref, win_ref), (h_ref, p_ref) = acts, weights, outs
    x = x_ref[...]
    sh1, sc1, g1, sh2, sc2 = (mod(j) for j in range(5))
    n1 = _affine(_rms(x), 1.0 + sc1, sh1).astype(_BF16)
    h = x + _affine(_swiglu(n1, wgu_ref, wdn_ref, tf), 0.5 * g1)
    h_ref[...] = h
    n2 = _affine(_rms(h), 1.0 + sc2, sh2).astype(_BF16)
    p_ref[...] = jnp.dot(n2, win_ref[...], preferred_element_type=_F32)


def _ffn_in_call(groups, mod, wgu, wdn, win, *, tf, stage, export=False):
    d = groups[0]["acts"][0].shape[1]
    return _dense_call(functools.partial(_ffn_in_body, tf=tf), "ffn1_inproj", groups, mod, [wgu, wdn, win], [],
                       [d, win.shape[1]], stage=stage, export=export)


def _out_ffn_body(acts, mod, weights, extra, outs, *, tf, final):
    (ym_ref, h_ref), (wout_ref, wgu_ref, wdn_ref), (fg_ref,), (o_ref,) = acts, weights, extra, outs
    g2, sh3, sc3, g3 = (mod(j) for j in range(5, 9))
    h = h_ref[...] + _affine(jnp.dot(ym_ref[...].astype(_BF16), wout_ref[...],
                                     preferred_element_type=_F32), g2)
    n3 = _affine(_rms(h), 1.0 + sc3, sh3).astype(_BF16)
    h = h + _affine(_swiglu(n3, wgu_ref, wdn_ref, tf), 0.5 * g3)
    if final:
        h = _rms(h) * fg_ref[...]
    o_ref[...] = h


def _out_ffn_call(groups, mod, wout, wgu, wdn, fg, *, tf, final, stage):
    d = wdn.shape[1]
    return _dense_call(functools.partial(_out_ffn_body, tf=tf, final=final), "outproj_ffn2", groups, mod,
                       [wout, wgu, wdn], [fg], [d], stage=stage)[0]


def _rows_bf16(*xs):
    return jnp.concatenate([x.astype(_BF16) for x in xs], axis=0)


def _block_diag(y, mask_bf):
    reps = mask_bf.shape[0] // y.shape[0]
    return jnp.concatenate([y.astype(_BF16)] * reps, axis=0) * mask_bf


def _nilpotent_inverse(n, bd, order):
    every = range(len(n))
    c = n[0].shape[0]
    tm = n
    levels = order.bit_length() - 2
    if levels <= 0:
        return tm
    nb = bd(n)
    npow = [_dot(n[i], nb[i]) for i in every]
    for lvl in range(levels):
        nb = bd(npow)
        if lvl + 1 < levels:
            res = [_dot(jnp.concatenate([tm[i], npow[i]], axis=0), nb[i]) for i in every]
            tm = [tm[i] + npow[i] + res[i][:c] for i in every]
            npow = [res[i][c:] for i in every]
        else:
            tm = [tm[i] + npow[i] + _dot(tm[i], nb[i]) for i in every]
    return tm


def _wkv_local(rt, at, bt, kt, vv, mask_bf, strict, incl, diag):
    n = len(rt)
    c = rt[0].shape[0]
    every = range(n)
    bd = lambda xs: [_block_diag(x, mask_bf) for x in xs]
    x = [_rows_bf16(at[i], rt[i]) for i in every]
    bb, kb = bd(bt), bd(kt)
    xb = [_dot_nt(x[i], bb[i]) for i in every]
    xk = [_dot_nt(x[i], kb[i]) for i in every]
    lab = [xb[i][:c] * strict for i in every]
    mrb = [xb[i][c:] * incl for i in every]
    lak = [xk[i][:c] * strict for i in every]
    mrk = [xk[i][c:] * incl for i in every]
    vb = bd(vv)
    wy = [_dot(jnp.concatenate([lak[i], mrk[i]], axis=0), vb[i]) for i in every]
    w1 = [wy[i][:c] for i in every]
    y0 = [wy[i][c:] for i in every]
    if isinstance(diag, int):
        tm = _nilpotent_inverse(lab, bd, diag)
    else:
        assert c == 4 * INV_BLOCK
        nd = [lab[i] * diag for i in every]
        noff = [lab[i] - nd[i] for i in every]
        tdm = _nilpotent_inverse(nd, bd, INV_BLOCK)
        noffb = bd(noff)
        m1 = [noff[i] + _dot(tdm[i], noffb[i]) for i in every]
        m1b = bd(m1)
        m2 = [_dot(m1[i], m1b[i]) for i in every]
        m2b = bd(m2)
        q = [m1[i] + m2[i] + _dot(m1[i], m2b[i]) for i in every]
        tdmb = bd(tdm)
        tm = [q[i] + tdm[i] + _dot(q[i], tdmb[i]) for i in every]
    atb = bd(at)
    ap = [at[i] + _dot(tm[i], atb[i]) for i in every]
    w1b = bd(w1)
    u0 = [w1[i] + _dot(tm[i], w1b[i]) for i in every]
    return [(ap[i], u0[i], mrb[i], y0[i]) for i in every]


def _wkv_state_step(ap, rt, u0, vv, bdec, kdec, pc, st, mask):
    every = range(len(rt))
    c = rt[0].shape[0]
    ar = [_dot_nt(_rows_bf16(ap[i], rt[i]), st[i]) for i in every]
    u = [ar[i][:c] + u0[i] for i in every]
    upd = [_dot_tn(_rows_bf16(u[i], vv[i]), _rows_bf16(bdec[i], kdec[i])) for i in every]
    st_new = [st[i] * pc[i] + upd[i] * mask for i in every]
    return u, [ar[i][c:] for i in every], st_new


def _token_features(p, first, trow, prm, segsum, rw):
    mu_ref, w0_ref, a0_ref, wwa_ref, wg_ref, kk_ref, ka_ref = prm
    r_cols = mu_ref.shape[1]
    pr = p[:, :r_cols]
    prev = jnp.where(trow == 0, first, pltpu.roll(pr, 1, 0))
    xm = pr + (prev - pr) * mu_ref[...]
    r = xm[:, 0:rw]
    k = xm[:, rw:2 * rw]
    v = xm[:, 2 * rw:3 * rw]
    lo_w = wwa_ref.shape[0]
    x_lo = xm[:, 3 * rw:3 * rw + lo_w]
    gd = xm[:, 3 * rw + lo_w:r_cols]
    lane = lax.broadcasted_iota(jnp.int32, (1, lo_w), 1)
    x_lo = jnp.where(lane < lo_w // 2, jnp.tanh(x_lo), x_lo)
    wa = _dot(x_lo, wwa_ref[...])
    wpre = w0_ref[...] + wa[:, :rw]
    lw = -DECAY_SCALE * _sigmoid(wpre)
    a = _sigmoid(a0_ref[...] + wa[:, rw:])
    gate = _dot(_sigmoid(gd), wg_ref[...])
    kkv = k * kk_ref[...]
    kkn = kkv * lax.rsqrt(jnp.maximum(segsum(kkv * kkv), 1e-24))
    k2 = k * (1.0 + (a - 1.0) * ka_ref[...])
    return dict(r=r, v=v, lw=lw, a=a, gate=gate, kkn=kkn, k2=k2)


def _decay_factors(f, cum, tot):
    pinc = jnp.exp(cum)
    pinv = jnp.exp(-cum)
    pend = jnp.exp(tot - cum)
    beta = f["kkn"] * f["a"]
    ops = dict(rt=f["r"] * pinc, at=-f["kkn"] * jnp.exp(cum - f["lw"]), bt=beta * pinv, kt=f["k2"] * pinv,
               bdec=beta * pend, kdec=f["k2"] * pend, v=f["v"])
    return dict({name: x.astype(_BF16) for name, x in ops.items()}, pinc=pinc)


def _rwkv_output(yw, f, rk_ref, gnw_ref, gnb_ref, segsum, head_dim):
    inv_n = 1.0 / head_dim
    mean = segsum(yw) * inv_n
    dev = yw - mean
    var = segsum(dev * dev) * inv_n
    yn = dev * lax.rsqrt(var + GN_EPS) * gnw_ref[...] + gnb_ref[...]
    bonus = segsum(f["r"] * f["k2"] * rk_ref[...]) * f["v"]
    return (yn + bonus) * f["gate"]


def _gmlp_branch(p, r_cols, lng_ref, lnb_ref, ws2_ref, tril2, bsf_ref):
    gwid = lng_ref.shape[1]
    ct = ws2_ref.shape[1]
    rows = p.shape[0]
    pu = p[:, r_cols:r_cols + gwid]
    pv = p[:, r_cols + gwid:r_cols + 2 * gwid]
    pm = jnp.mean(pv, -1, keepdims=True)
    pd = pv - pm
    pvar = jnp.mean(pd * pd, -1, keepdims=True)
    vn = pd * lax.rsqrt(pvar + LN_EPS) * lng_ref[...] + lnb_ref[...]
    n_pair = ws2_ref.shape[0]
    pw = gwid // n_pair
    first_half = lax.broadcasted_iota(jnp.int32, (1, pw), 1) < pw // 2
    ws_tril = [(ws2_ref[j] * tril2).astype(_BF16) for j in range(n_pair)]
    mixed_rows = []
    for q in range(rows // ct):
        outs = []
        for j in range(n_pair):
            vp = vn[q * ct:(q + 1) * ct, j * pw:(j + 1) * pw]
            rhs = jnp.concatenate([jnp.where(first_half, vp, 0.0), jnp.where(first_half, 0.0, vp)], axis=0)
            outs.append(_dot(ws_tril[j], rhs))
        mixed_rows.append(jnp.concatenate(outs, axis=1) + bsf_ref[...])
    mixed = mixed_rows[0] if len(mixed_rows) == 1 else jnp.concatenate(mixed_rows, axis=0)
    return pu * mixed, vn


def _state_from_heads(wkv_ref, s, g, head_dim):
    zero = jnp.zeros((head_dim, head_dim), _F32)
    blocks = []
    for h in range(HEAD_GROUP):
        blk = wkv_ref[s, g * HEAD_GROUP + h]
        blocks.append(jnp.concatenate([zero] * h + [blk] + [zero] * (HEAD_GROUP - 1 - h), axis=1))
    return jnp.concatenate(blocks, axis=0)


def _state_to_heads(wkv_ref, s, g, st, head_dim):
    for h in range(HEAD_GROUP):
        wkv_ref[s, g * HEAD_GROUP + h] = st[h * head_dim:(h + 1) * head_dim, h * head_dim:(h + 1) * head_dim]


def _mixer_kernel(p_ref, mu_ref, w0_ref, a0_ref, wwa_ref, wg_ref, kk_ref, ka_ref, rk_ref, gnw_ref, gnb_ref,
                  lng_ref, lnb_ref, ws2_ref, bsf_ref, hmask_ref, hmaskb_ref, tmask_ref, segm_ref, tril2_ref,
                  y_ref, shift_ref, wkv_ref, st_s, carry_s, *, n_heads, head_dim):
    nb, tr, _ = p_ref.shape
    c = WKV_CHUNK
    gw = HEAD_GROUP * head_dim
    n_groups = n_heads // HEAD_GROUP
    rw = n_heads * head_dim
    rows = nb * tr
    ti = pl.program_id(1)
    n_t = pl.num_programs(1)
    r_cols = mu_ref.shape[1]

    @pl.when(ti == 0)
    def _():
        carry_s[...] = jnp.zeros_like(carry_s)
        st_s[...] = jnp.zeros_like(st_s)

    mask = hmask_ref[...]
    mask_bf = hmaskb_ref[...]
    strict, incl, diag = tmask_ref[0], tmask_ref[1], tmask_ref[2]
    tri_bf = segm_ref[...]

    def segsum(x):
        return jnp.concatenate([_dot(x[:, g * gw:(g + 1) * gw], mask_bf) for g in range(n_groups)], axis=1)

    p = p_ref[0] if nb == 1 else jnp.concatenate([p_ref[s] for s in range(nb)], axis=0)
    trow = _mod_pow2(lax.broadcasted_iota(jnp.int32, (rows, 1), 0), tr)
    if nb == 1:
        first = carry_s[0]
    else:
        first = jnp.concatenate([jnp.broadcast_to(carry_s[s], (tr, r_cols)) for s in range(nb)], axis=0)
    for s in range(nb):
        carry_s[s] = p_ref[s, tr - 1:tr, :r_cols]
    f = _token_features(p, first, trow, (mu_ref, w0_ref, a0_ref, wwa_ref, wg_ref, kk_ref, ka_ref), segsum, rw)
    sr = tri_bf.shape[0]
    cum = jnp.concatenate([_dot_sel_lhs(tri_bf, f["lw"][i * sr:(i + 1) * sr]) for i in range(rows // sr)],
                          axis=0)
    tot = jnp.concatenate([jnp.broadcast_to(cum[j * c + c - 1:j * c + c], (c, rw)) for j in range(rows // c)],
                          axis=0)
    d = _decay_factors(f, cum, tot)

    n_chunks = rows // c
    chunks_per_seq = tr // c

    def blk(x, j, g):
        return x[j * c:(j + 1) * c, g * gw:(g + 1) * gw]

    probs = [(j, g) for j in range(n_chunks) for g in range(n_groups)]
    cut = lambda x, ps: [blk(x, j, g) for j, g in ps]
    local = dict(zip(probs, _wkv_local(cut(d["rt"], probs), cut(d["at"], probs), cut(d["bt"], probs),
                                       cut(d["kt"], probs), cut(d["v"], probs), mask_bf, strict, incl, diag)))
    y_blk = {}
    for jj in range(chunks_per_seq):
        ps = [(s * chunks_per_seq + jj, g) for s in range(nb) for g in range(n_groups)]
        loc = [local[pr_] for pr_ in ps]
        sts = [st_s[j // chunks_per_seq, g] for j, g in ps]
        pcs = [blk(d["pinc"], j, g)[c - 1:c] for j, g in ps]
        u, ys, st_new = _wkv_state_step([l[0] for l in loc], cut(d["rt"], ps), [l[1] for l in loc],
                                        cut(d["v"], ps), cut(d["bdec"], ps), cut(d["kdec"], ps), pcs, sts, mask)
        for i, (j, g) in enumerate(ps):
            st_s[j // chunks_per_seq, g] = st_new[i]
            y_blk[j, g] = ys[i] + _dot(loc[i][2], _block_diag(u[i], mask_bf)) + loc[i][3]
    y_rows = [jnp.concatenate([y_blk[j, g] for g in range(n_groups)], axis=1) for j in range(n_chunks)]
    yw = y_rows[0] if n_chunks == 1 else jnp.concatenate(y_rows, axis=0)

    y_r = _rwkv_output(yw, f, rk_ref, gnw_ref, gnb_ref, segsum, head_dim)
    y_c, _ = _gmlp_branch(p, r_cols, lng_ref, lnb_ref, ws2_ref, tril2_ref[...], bsf_ref)
    y_all = jnp.concatenate([y_r, y_c], axis=1)
    for s in range(nb):
        y_ref[s] = y_all[s * tr:(s + 1) * tr]

    @pl.when(ti == n_t - 1)
    def _():
        shift_ref[...] = carry_s[...]
        for s in range(nb):
            for g in range(n_groups):
                _state_to_heads(wkv_ref, s, g, st_s[s, g], head_dim)


def _mixer_short_kernel(p_ref, shift0_ref, wkv0_ref, mu_ref, w0_ref, a0_ref, wwa_ref, wg_ref, kk_ref, ka_ref,
                        rk_ref, gnw_ref, gnb_ref, lng_ref, lnb_ref, ws2_ref, bsf_ref, hmask_ref, hmaskb_ref,
                        tmask_ref, segm_ref, tril2_ref, y_ref, shift_ref, wkv_ref, vn_ref, *, n_heads, head_dim):
    nb, ts, _ = p_ref.shape
    c = WKV_CHUNK
    assert nb * ts == c
    gw = HEAD_GROUP * head_dim
    n_groups = n_heads // HEAD_GROUP
    rw = n_heads * head_dim
    r_cols = mu_ref.shape[1]

    mask = hmask_ref[...]
    mask_bf = hmaskb_ref[...]
    strict, incl = tmask_ref[0], tmask_ref[1]

    def segsum(x):
        return jnp.concatenate([_dot(x[:, g * gw:(g + 1) * gw], mask_bf) for g in range(n_groups)], axis=1)

    p = p_ref[...].reshape(c, p_ref.shape[2])
    trow = _mod_pow2(lax.broadcasted_iota(jnp.int32, (c, 1), 0), ts)
    first = jnp.concatenate([jnp.broadcast_to(shift0_ref[s], (ts, r_cols)) for s in range(nb)], axis=0)
    for s in range(nb):
        shift_ref[s] = p_ref[s, ts - 1:ts, :r_cols]
    f = _token_features(p, first, trow, (mu_ref, w0_ref, a0_ref, wwa_ref, wg_ref, kk_ref, ka_ref), segsum, rw)
    cum = _dot_sel_lhs(segm_ref[...], f["lw"])
    tot = jnp.concatenate([jnp.broadcast_to(cum[s * ts + ts - 1:(s + 1) * ts], (ts, rw)) for s in range(nb)],
                          axis=0)
    d = _decay_factors(f, cum, tot)

    groups = range(n_groups)
    gcut = lambda x: [x[:, g * gw:(g + 1) * gw] for g in groups]
    local = _wkv_local(gcut(d["rt"]), gcut(d["at"]), gcut(d["bt"]), gcut(d["kt"]), gcut(d["v"]), mask_bf,
                       strict, incl, ts)

    ps = [(s, g) for s in range(nb) for g in groups]
    scut = lambda x: [x[s * ts:(s + 1) * ts, g * gw:(g + 1) * gw] for s, g in ps]
    lcut = lambda idx: [local[g][idx][s * ts:(s + 1) * ts] for s, g in ps]
    sts = [_state_from_heads(wkv0_ref, s, g, head_dim) for s, g in ps]
    pcs = [d["pinc"][s * ts + ts - 1:(s + 1) * ts, g * gw:(g + 1) * gw] for s, g in ps]
    u, ys, st_new = _wkv_state_step(lcut(0), scut(d["rt"]), lcut(1), scut(d["v"]), scut(d["bdec"]),
                                    scut(d["kdec"]), pcs, sts, mask)
    for i, (s, g) in enumerate(ps):
        _state_to_heads(wkv_ref, s, g, st_new[i], head_dim)
    rows_of = lambda xs, g: jnp.concatenate([xs[i] for i, (s, gg) in enumerate(ps) if gg == g], axis=0)
    yw = jnp.concatenate([rows_of(ys, g) + _dot(local[g][2], _block_diag(rows_of(u, g), mask_bf)) + local[g][3]
                          for g in groups], axis=1)

    y_r = _rwkv_output(yw, f, rk_ref, gnw_ref, gnb_ref, segsum, head_dim)
    y_c, vn = _gmlp_branch(p, r_cols, lng_ref, lnb_ref, ws2_ref, tril2_ref[...], bsf_ref)
    y_ref[...] = jnp.concatenate([y_r, y_c], axis=1).reshape(nb, ts, -1)
    vn_ref[...] = vn.reshape(nb, ts, -1)


def _mixer_masks(rows, seq, ct, head_dim):
    c, gw = WKV_CHUNK, HEAD_GROUP * head_dim
    blk = np.arange(gw) // head_dim
    hmask = (blk[:, None] == blk[None, :]).astype(np.float32)
    t = np.arange(c)[:, None]
    s = np.arange(HEAD_GROUP * c)[None, :] % c
    strict = (s < t) & (s // seq == t // seq)
    incl = (s <= t) & (s // seq == t // seq)
    tmask = np.stack([strict, incl, strict & (s // INV_BLOCK == t // INV_BLOCK)]).astype(np.float32)
    r = np.arange(rows)
    seg = min(c, seq)
    same = (r[:, None] // seg) == (r[None, :] // seg)
    segm = (same & (r[None, :] <= r[:, None])).astype(np.float32)
    ctm = np.arange(ct)
    tril1 = ((ctm[None, :] <= ctm[:, None]) & (ctm[None, :] // seq == ctm[:, None] // seq))
    tril2 = np.concatenate([tril1, tril1], axis=1).astype(np.float32)
    return [jnp.asarray(hmask), jnp.asarray(hmask, _BF16), jnp.asarray(tmask), jnp.asarray(segm, _BF16),
            jnp.asarray(tril2)]


def _gmlp_weights(w_s, b_s, ct, seq, gwid):
    hc = w_s.shape[0]
    n_seq = ct // min(seq, ct)
    ws = w_s[:, :ct // n_seq, :ct // n_seq]
    bs = b_s[:, :ct // n_seq]
    if n_seq > 1:
        ws = jnp.einsum("st,hij->hsitj", jnp.eye(n_seq, dtype=ws.dtype), ws).reshape(hc, ct, ct)
        bs = jnp.tile(bs, (1, n_seq))
    ws2 = ws.reshape(hc // 2, 2, ct, ct).transpose(0, 2, 1, 3).reshape(hc // 2, ct, 2 * ct)
    return ws2, jnp.repeat(bs.T, gwid // hc, axis=1)


def _mixer_params(prm, rows, seq, ct, head_dim):
    gwid = prm["lng"].shape[1]
    ws2, bsf = _gmlp_weights(prm["w_s"], prm["b_s"], ct, seq, gwid)
    params = [prm["mu"], prm["w0"], prm["a0"], prm["wwa"], prm["wg"], prm["k_k"], prm["k_a"], prm["r_k"],
              prm["gn_w"], prm["gn_b"], prm["lng"], prm["lnb"], ws2, bsf]
    return params + _mixer_masks(rows, seq, ct, head_dim)


def _full_spec(a):
    nd = a.ndim
    return pl.BlockSpec(a.shape, lambda b, i: (0,) * nd)


def _mixer_call(p, prm, *, nb, tr, n_heads, head_dim):
    nseq, t, pcols = p.shape
    rw = n_heads * head_dim
    r_cols = prm["mu"].shape[1]
    gwid = prm["lng"].shape[1]
    gw = HEAD_GROUP * head_dim
    n_groups = n_heads // HEAD_GROUP
    rows = nb * tr
    assert nseq % nb == 0 and t % tr == 0 and tr % GMLP_CHUNK == 0 and rows % SEG_ROWS == 0
    params = _mixer_params(prm, SEG_ROWS, t, GMLP_CHUNK, head_dim)
    return pl.pallas_call(
        functools.partial(_mixer_kernel, n_heads=n_heads, head_dim=head_dim),
        out_shape=(jax.ShapeDtypeStruct((nseq, t, rw + gwid), _F32),
                   jax.ShapeDtypeStruct((nseq, 1, r_cols), _F32),
                   jax.ShapeDtypeStruct((nseq, n_heads, head_dim, head_dim), _F32)),
        grid=(nseq // nb, t // tr),
        in_specs=[pl.BlockSpec((nb, tr, pcols), lambda b, i: (b, i, 0))] + [_full_spec(a) for a in params],
        out_specs=(pl.BlockSpec((nb, tr, rw + gwid), lambda b, i: (b, i, 0)),
                   pl.BlockSpec((nb, 1, r_cols), lambda b, i: (b, 0, 0)),
                   pl.BlockSpec((nb, n_heads, head_dim, head_dim), lambda b, i: (b, 0, 0, 0))),
        scratch_shapes=[pltpu.VMEM((nb, n_groups, gw, gw), _F32), pltpu.VMEM((nb, 1, r_cols), _F32)],
        compiler_params=pltpu.CompilerParams(dimension_semantics=("arbitrary", "arbitrary"),
                                             vmem_limit_bytes=_VMEM_LIMIT),
        name="mixers_fresh",
    )(p, *params)


def _mixer_short_call(p, shift0, wkv0, prm, *, n_heads, head_dim):
    nseq, ts, pcols = p.shape
    nb = WKV_CHUNK // ts
    assert nb * ts == WKV_CHUNK and ts % SUBLANES == 0 and nseq % nb == 0
    rw = n_heads * head_dim
    r_cols = prm["mu"].shape[1]
    gwid = prm["lng"].shape[1]
    params = _mixer_params(prm, WKV_CHUNK, ts, WKV_CHUNK, head_dim)
    seqs = lambda *tail: pl.BlockSpec((nb,) + tail, lambda b, i: (b,) + (0,) * len(tail))
    return pl.pallas_call(
        functools.partial(_mixer_short_kernel, n_heads=n_heads, head_dim=head_dim),
        out_shape=(jax.ShapeDtypeStruct((nseq, ts, rw + gwid), _F32),
                   jax.ShapeDtypeStruct((nseq, 1, r_cols), _F32),
                   jax.ShapeDtypeStruct((nseq, n_heads, head_dim, head_dim), _F32),
                   jax.ShapeDtypeStruct((nseq, ts, gwid), _F32)),
        grid=(nseq // nb, 1),
        in_specs=[seqs(ts, pcols), seqs(1, r_cols), seqs(n_heads, head_dim, head_dim)]
                 + [_full_spec(a) for a in params],
        out_specs=(seqs(ts, rw + gwid), seqs(1, r_cols), seqs(n_heads, head_dim, head_dim), seqs(ts, gwid)),
        compiler_params=pltpu.CompilerParams(dimension_semantics=("arbitrary", "arbitrary"),
                                             vmem_limit_bytes=_VMEM_LIMIT),
        name="mixers_state",
    )(p, shift0, wkv0, *params)


def _pick_tile(n, pref):
    t = min(n, pref)
    while n % t:
        t //= 2
    return t


def kernel(x_prompt, x_sample, state_shift, state_wkv, c_prompt, c_sample, w_ada, b_ada, ffn1_gu, ffn1_dn, w_in, mu_shift, w0, w_lora_up, a0, a_lora_up, g_lora_up, k_k, k_a, r_k, gn_w, gn_b, ln_v_g, ln_v_b, w_s, b_s, w_out, ffn2_gu, ffn2_dn, final_g):
    depth = w_ada.shape[0]
    bp, tp, d = x_prompt.shape
    bs, ts, _ = x_sample.shape
    n_heads, head_dim = r_k.shape[1], r_k.shape[2]
    rw = n_heads * head_dim
    w_rank, a_rank = w_lora_up.shape[1], a_lora_up.shape[1]
    assert w_rank == a_rank and n_heads % HEAD_GROUP == 0 and head_dim == WKV_CHUNK
    d_ff = ffn1_dn.shape[1]
    tf = 256 if d_ff % 256 == 0 else 128

    xp = x_prompt.reshape(bp * tp, d)
    xs = x_sample.reshape(bs * ts, d)
    assert bs % SUBLANES == 0
    n_c = bs + bp
    n_c_pad = -(-n_c // 16) * 16
    c_all = jnp.concatenate([c_sample, c_prompt, jnp.zeros((n_c_pad - n_c, d), _F32)], axis=0)

    tm_p = _pick_tile(tp, 512)
    tr_p = _pick_tile(tp, 256)
    tm_s = ts * _pick_tile(bs, 64)
    nb_p = _pick_tile(bp, 2)

    shp, wkp, shs, wks, cvs = [], [], [], [], []
    for l in range(depth):
        mod = _mod_call(c_all, w_ada[l], b_ada[l])
        grp_p = dict(tm=tm_p, seq0=bs, seqs_per_tile=1, tiles_per_seq=tp // tm_p)
        grp_s = dict(tm=tm_s, seq0=0, seqs_per_tile=tm_s // ts, tiles_per_seq=1)

        zw = jnp.zeros((w_rank, rw), _F32)
        wwa = jnp.concatenate([jnp.concatenate([w_lora_up[l], zw], axis=1),
                               jnp.concatenate([zw, a_lora_up[l]], axis=1)], axis=0).astype(_BF16)
        row = lambda a: a.reshape(1, -1)
        prm = dict(mu=row(mu_shift[l]), w0=row(w0[l]), a0=row(a0[l]), wwa=wwa,
                   wg=g_lora_up[l].astype(_BF16), k_k=row(k_k[l]), k_a=row(k_a[l]), r_k=row(r_k[l]),
                   gn_w=row(gn_w[l]), gn_b=row(gn_b[l]), lng=row(ln_v_g[l]), lnb=row(ln_v_b[l]),
                   w_s=w_s[l], b_s=b_s[l])
        fg = final_g.reshape(1, d)
        final = l == depth - 1

        ((hp, pp),), w1_bf = _ffn_in_call([dict(grp_p, acts=[xp])], mod, ffn1_gu[l], ffn1_dn[l], w_in[l],
                                          tf=tf, stage=WEIGHT_STAGE_STEPS, export=True)
        ((hs, ps),), _ = _ffn_in_call([dict(grp_s, acts=[xs])], mod, *w1_bf, tf=tf, stage=0)
        ymp, sh_p, wk_p = _mixer_call(pp.reshape(bp, tp, -1), prm, nb=nb_p, tr=tr_p,
                                      n_heads=n_heads, head_dim=head_dim)
        yms, sh_s, wk_s, v_s = _mixer_short_call(ps.reshape(bs, ts, -1), state_shift[l][:, None, :], state_wkv[l], prm,
                                                 n_heads=n_heads, head_dim=head_dim)
        half_s = dict(grp_s, tm=tm_s // 2, seqs_per_tile=tm_s // ts // 2)
        (xp,), (xs,) = _out_ffn_call([dict(grp_p, acts=[ymp.reshape(bp * tp, -1), hp]),
                                      dict(half_s, acts=[yms.reshape(bs * ts, -1), hs])], mod,
                                     w_out[l], ffn2_gu[l], ffn2_dn[l], fg, tf=tf, final=final,
                                     stage=WEIGHT_STAGE_STEPS)

        shp.append(sh_p[:, 0]); wkp.append(wk_p); shs.append(sh_s[:, 0]); wks.append(wk_s); cvs.append(v_s)

    y_prompt = xp.reshape(bp, tp, d)
    y_sample = xs.reshape(bs, ts, d)
    return (y_prompt, y_sample, jnp.stack(shp), jnp.stack(wkp), jnp.stack(shs), jnp.stack(wks), jnp.stack(cvs))
```

```python
import functools
import math

import jax
import jax.numpy as jnp
import numpy as np
from jax import lax
from jax.experimental import pallas as pl
from jax.experimental.pallas import tpu as pltpu

_F32 = jnp.float32
_BF16 = jnp.bfloat16

RMS_EPS = 1e-6
LN_EPS = 1e-5
GN_EPS = 64e-5
N_MOD = 9
WKV_CHUNK = 64
HEAD_GROUP = 4
INV_BLOCK = 16
GMLP_CHUNK = 128
SEG_ROWS = 256
SUBLANES = 8
MOD_PER_STEP = 3
WEIGHT_STAGE_STEPS = 16
DECAY_SCALE = math.exp(-0.5)

_V7X_VMEM_BYTES = 64 * 1024 * 1024
_VMEM_LIMIT = _V7X_VMEM_BYTES - 8 * 1024 * 1024


def _dot(a, b):
    return jnp.dot(a.astype(_BF16), b.astype(_BF16), preferred_element_type=_F32)


def _dot_nt(a, b):
    return lax.dot_general(a.astype(_BF16), b.astype(_BF16), (((1,), (1,)), ((), ())),
                           preferred_element_type=_F32)


def _dot_tn(a, b):
    return lax.dot_general(a.astype(_BF16), b.astype(_BF16), (((0,), (0,)), ((), ())),
                           preferred_element_type=_F32)


def _split_hi_lo(x):
    hi = x.astype(_BF16)
    lo = (x - hi.astype(_F32)).astype(_BF16)
    return hi, lo


def _dot_sel_lhs(sel, x):
    hi, lo = _split_hi_lo(x)
    return (jnp.dot(sel, hi, preferred_element_type=_F32)
            + jnp.dot(sel, lo, preferred_element_type=_F32))


def _mod_pow2(x, n):
    assert n & (n - 1) == 0
    return lax.bitwise_and(x, jnp.int32(n - 1))


def _rms(x):
    return x * lax.rsqrt(jnp.mean(x * x, -1, keepdims=True) + RMS_EPS)


def _sigmoid(x):
    return 1.0 / (1.0 + jnp.exp(-x))


def _affine(x, scale, shift=None):
    mb = scale.shape[0]
    if mb == 1:
        y = x * scale
        return y if shift is None else y + shift
    tm, d = x.shape
    y = x.reshape(mb, tm // mb, d) * scale[:, None, :]
    if shift is not None:
        y = y + shift[:, None, :]
    return y.reshape(tm, d)


def _swiglu(n_bf, wgu_ref, wdn_ref, tf):
    d_ff = wdn_ref.shape[0]
    acc = None
    for j in range(d_ff // tf):
        g = jnp.dot(n_bf, wgu_ref[:, j * tf:(j + 1) * tf], preferred_element_type=_F32)
        u = jnp.dot(n_bf, wgu_ref[:, d_ff + j * tf:d_ff + (j + 1) * tf], preferred_element_type=_F32)
        hm = (g * _sigmoid(g) * u).astype(_BF16)
        part = jnp.dot(hm, wdn_ref[j * tf:(j + 1) * tf, :], preferred_element_type=_F32)
        acc = part if acc is None else acc + part
    return acc


def _mod_kernel(c_ref, w_ref, b_ref, o_ref):
    c = c_ref[...]
    s = c * _sigmoid(c)
    o = _dot(s, w_ref[...]) + b_ref[...]
    d = c.shape[1]
    for k in range(o_ref.shape[0]):
        o_ref[k] = o[:, k * d:(k + 1) * d]


def _mod_call(c_all, w_ada, b_ada):
    n, d = c_all.shape
    per_step = MOD_PER_STEP if N_MOD % MOD_PER_STEP == 0 else 1
    return pl.pallas_call(
        _mod_kernel,
        out_shape=jax.ShapeDtypeStruct((N_MOD, n, d), _F32),
        grid=(N_MOD // per_step,),
        in_specs=[pl.BlockSpec((n, d), lambda j: (0, 0)),
                  pl.BlockSpec((d, per_step * d), lambda j: (0, j)),
                  pl.BlockSpec((1, per_step * d), lambda j: (0, j))],
        out_specs=pl.BlockSpec((per_step, n, d), lambda j: (j, 0, 0)),
        compiler_params=pltpu.CompilerParams(dimension_semantics=("arbitrary",),
                                             vmem_limit_bytes=_VMEM_LIMIT),
        name="adaln_mod",
    )(c_all, w_ada, b_ada.reshape(1, N_MOD * d))


def _mod_rows(mod_ref, pick, tile):
    if pick is None:
        return lambda j: mod_ref[j]
    seq0, tiles_per_seq = pick
    row = lax.rem(seq0 + lax.div(tile, jnp.int32(tiles_per_seq)), jnp.int32(SUBLANES))
    return lambda j: mod_ref[j, pl.ds(row, 1), :]


def _mod_spec(mod, seq0, seqs_per_tile, tiles_per_seq, tile):
    n_mod, _, d = mod.shape
    if seqs_per_tile > 1:
        assert seqs_per_tile % SUBLANES == 0 and seq0 % seqs_per_tile == 0 and tiles_per_seq == 1
        first = seq0 // seqs_per_tile
        return pl.BlockSpec((n_mod, seqs_per_tile, d), lambda i: (0, first + tile(i), 0)), None
    return (pl.BlockSpec((n_mod, SUBLANES, d), lambda i: (0, (seq0 + tile(i) // tiles_per_seq) // SUBLANES, 0)),
            (seq0, tiles_per_seq))


def _resident(shape):
    nd = len(shape)
    return pl.BlockSpec(shape, lambda *_: (0,) * nd, pipeline_mode=pl.Buffered(1))


def _dense_call(body, name, groups, mod, weights, extra, out_widths, *, stage, export=False):
    starts, n_steps = [], stage
    for g in groups:
        starts.append(n_steps)
        n_steps += g["acts"][0].shape[0] // g["tm"]
    n_w, n_x, n_out = len(weights), len(extra), len(out_widths)

    def tile_of(k):
        last = groups[k]["acts"][0].shape[0] // groups[k]["tm"] - 1
        return lambda i: jnp.clip(i - starts[k], 0, last)

    in_specs, picks, inputs = [], [], []
    for k, g in enumerate(groups):
        tile = tile_of(k)
        mod_spec, pick = _mod_spec(mod, g["seq0"], g["seqs_per_tile"], g["tiles_per_seq"], tile)
        picks.append(pick)
        in_specs += [pl.BlockSpec((g["tm"], a.shape[1]), lambda i, tile=tile: (tile(i), 0)) for a in g["acts"]]
        in_specs.append(mod_spec)
        inputs += list(g["acts"]) + [mod]
    out_shape, out_specs = [], []
    for k, g in enumerate(groups):
        tile = tile_of(k)
        out_shape += [jax.ShapeDtypeStruct((g["acts"][0].shape[0], w), _F32) for w in out_widths]
        out_specs += [pl.BlockSpec((g["tm"], w), lambda i, tile=tile: (tile(i), 0)) for w in out_widths]
    scratch = []
    if stage:
        assert all(w.shape[0] % (stage * 2 * SUBLANES) == 0 for w in weights)
        chunk_spec = lambda w: pl.BlockSpec((w.shape[0] // stage, w.shape[1]),
                                            lambda i: (jnp.minimum(i, stage - 1), 0))
        in_specs += [chunk_spec(w) for w in weights]
        scratch = [pltpu.VMEM(w.shape, _BF16) for w in weights]
        if export:
            out_shape += [jax.ShapeDtypeStruct(w.shape, _BF16) for w in weights]
            out_specs += [chunk_spec(w) for w in weights]
    else:
        in_specs += [_resident(w.shape) for w in weights]
    in_specs += [_resident(e.shape) for e in extra]

    def kernel(*refs):
        it = iter(refs)
        act_refs, mod_refs = [], []
        for g in groups:
            act_refs.append([next(it) for _ in g["acts"]])
            mod_refs.append(next(it))
        w_in_refs = [next(it) for _ in range(n_w)]
        x_refs = [next(it) for _ in range(n_x)]
        out_refs = [[next(it) for _ in range(n_out)] for _ in groups]
        w_out_refs = [next(it) for _ in range(n_w)] if stage and export else [None] * n_w
        w_refs = [next(it) for _ in range(n_w)] if stage else w_in_refs
        step = pl.program_id(0)

        if stage:
            @pl.when(step < stage)
            def _():
                for src, dst, out in zip(w_in_refs, w_refs, w_out_refs):
                    chunk = src.shape[0]
                    v = src[...].astype(_BF16)
                    dst[pl.ds(pl.multiple_of(step * chunk, chunk), chunk), :] = v
                    if out is not None:
                        out[...] = v

        for k in range(len(groups)):
            end = starts[k + 1] if k + 1 < len(groups) else n_steps

            def run(k=k):
                body(act_refs[k], _mod_rows(mod_refs[k], picks[k], step - starts[k]), w_refs, x_refs, out_refs[k])

            if len(groups) == 1 and not stage:
                run()
            else:
                pl.when((step >= starts[k]) & (step < end))(run)

    outs = pl.pallas_call(
        kernel,
        out_shape=tuple(out_shape),
        grid=(n_steps,),
        in_specs=in_specs,
        out_specs=tuple(out_specs),
        scratch_shapes=scratch,
        compiler_params=pltpu.CompilerParams(dimension_semantics=("arbitrary",),
                                             vmem_limit_bytes=_VMEM_LIMIT),
        name=name,
    )(*inputs, *weights, *extra)
    per_group = [outs[k * n_out:(k + 1) * n_out] for k in range(len(groups))]
    return per_group, list(outs[len(groups) * n_out:])


def _ffn_in_body(acts, mod, weights, extra, outs, *, tf):
    (x_ref,), (wgu_ref, wdn_ref, win_ref), (h_ref, p_ref) = acts, weights, outs
    x = x_ref[...]
    sh1, sc1, g1, sh2, sc2 = (mod(j) for j in range(5))
    n1 = _affine(_rms(x), 1.0 + sc1, sh1).astype(_BF16)
    h = x + _affine(_swiglu(n1, wgu_ref, wdn_ref, tf), 0.5 * g1)
    h_ref[...] = h
    n2 = _affine(_rms(h), 1.0 + sc2, sh2).astype(_BF16)
    p_ref[...] = jnp.dot(n2, win_ref[...], preferred_element_type=_F32)


def _ffn_in_call(groups, mod, wgu, wdn, win, *, tf, stage, export=False):
    d = groups[0]["acts"][0].shape[1]
    return _dense_call(functools.partial(_ffn_in_body, tf=tf), "ffn1_inproj", groups, mod, [wgu, wdn, win], [],
                       [d, win.shape[1]], stage=stage, export=export)


def _out_ffn_body(acts, mod, weights, extra, outs, *, tf, final):
    (ym_ref, h_ref), (wout_ref, wgu_ref, wdn_ref), (fg_ref,), (o_ref,) = acts, weights, extra, outs
    g2, sh3, sc3, g3 = (mod(j) for j in range(5, 9))
    h = h_ref[...] + _affine(jnp.dot(ym_ref[...].astype(_BF16), wout_ref[...],
                                     preferred_element_type=_F32), g2)
    n3 = _affine(_rms(h), 1.0 + sc3, sh3).astype(_BF16)
    h = h + _affine(_swiglu(n3, wgu_ref, wdn_ref, tf), 0.5 * g3)
    if final:
        h = _rms(h) * fg_ref[...]
    o_ref[...] = h


def _out_ffn_call(groups, mod, wout, wgu, wdn, fg, *, tf, final, stage):
    d = wdn.shape[1]
    return _dense_call(functools.partial(_out_ffn_body, tf=tf, final=final), "outproj_ffn2", groups, mod,
                       [wout, wgu, wdn], [fg], [d], stage=stage)[0]


def _rows_bf16(*xs):
    return jnp.concatenate([x.astype(_BF16) for x in xs], axis=0)


def _block_diag(y, mask_bf):
    reps = mask_bf.shape[0] // y.shape[0]
    return jnp.concatenate([y.astype(_BF16)] * reps, axis=0) * mask_bf


def _nilpotent_inverse(n, bd, order):
    every = range(len(n))
    c = n[0].shape[0]
    tm = n
    levels = order.bit_length() - 2
    if levels <= 0:
        return tm
    nb = bd(n)
    npow = [_dot(n[i], nb[i]) for i in every]
    for lvl in range(levels):
        nb = bd(npow)
        if lvl + 1 < levels:
            res = [_dot(jnp.concatenate([tm[i], npow[i]], axis=0), nb[i]) for i in every]
            tm = [tm[i] + npow[i] + res[i][:c] for i in every]
            npow = [res[i][c:] for i in every]
        else:
            tm = [tm[i] + npow[i] + _dot(tm[i], nb[i]) for i in every]
    return tm


def _wkv_local(rt, at, bt, kt, vv, mask_bf, strict, incl, diag):
    n = len(rt)
    c = rt[0].shape[0]
    every = range(n)
    bd = lambda xs: [_block_diag(x, mask_bf) for x in xs]
    x = [_rows_bf16(at[i], rt[i]) for i in every]
    bb, kb = bd(bt), bd(kt)
    xb = [_dot_nt(x[i], bb[i]) for i in every]
    xk = [_dot_nt(x[i], kb[i]) for i in every]
    lab = [xb[i][:c] * strict for i in every]
    mrb = [xb[i][c:] * incl for i in every]
    lak = [xk[i][:c] * strict for i in every]
    mrk = [xk[i][c:] * incl for i in every]
    vb = bd(vv)
    wy = [_dot(jnp.concatenate([lak[i], mrk[i]], axis=0), vb[i]) for i in every]
    w1 = [wy[i][:c] for i in every]
    y0 = [wy[i][c:] for i in every]
    if isinstance(diag, int):
        tm = _nilpotent_inverse(lab, bd, diag)
    else:
        assert c == 4 * INV_BLOCK
        nd = [lab[i] * diag for i in every]
        noff = [lab[i] - nd[i] for i in every]
        tdm = _nilpotent_inverse(nd, bd, INV_BLOCK)
        noffb = bd(noff)
        m1 = [noff[i] + _dot(tdm[i], noffb[i]) for i in every]
        m1b = bd(m1)
        m2 = [_dot(m1[i], m1b[i]) for i in every]
        m2b = bd(m2)
        q = [m1[i] + m2[i] + _dot(m1[i], m2b[i]) for i in every]
        tdmb = bd(tdm)
        tm = [q[i] + tdm[i] + _dot(q[i], tdmb[i]) for i in every]
    atb = bd(at)
    ap = [at[i] + _dot(tm[i], atb[i]) for i in every]
    w1b = bd(w1)
    u0 = [w1[i] + _dot(tm[i], w1b[i]) for i in every]
    return [(ap[i], u0[i], mrb[i], y0[i]) for i in every]


def _wkv_state_step(ap, rt, u0, vv, bdec, kdec, pc, st, mask):
    every = range(len(rt))
    c = rt[0].shape[0]
    ar = [_dot_nt(_rows_bf16(ap[i], rt[i]), st[i]) for i in every]
    u = [ar[i][:c] + u0[i] for i in every]
    upd = [_dot_tn(_rows_bf16(u[i], vv[i]), _rows_bf16(bdec[i], kdec[i])) for i in every]
    st_new = [st[i] * pc[i] + upd[i] * mask for i in every]
    return u, [ar[i][c:] for i in every], st_new


def _token_features(p, first, trow, prm, segsum, rw):
    mu_ref, w0_ref, a0_ref, wwa_ref, wg_ref, kk_ref, ka_ref = prm
    r_cols = mu_ref.shape[1]
    pr = p[:, :r_cols]
    prev = jnp.where(trow == 0, first, pltpu.roll(pr, 1, 0))
    xm = pr + (prev - pr) * mu_ref[...]
    r = xm[:, 0:rw]
    k = xm[:, rw:2 * rw]
    v = xm[:, 2 * rw:3 * rw]
    lo_w = wwa_ref.shape[0]
    x_lo = xm[:, 3 * rw:3 * rw + lo_w]
    gd = xm[:, 3 * rw + lo_w:r_cols]
    lane = lax.broadcasted_iota(jnp.int32, (1, lo_w), 1)
    x_lo = jnp.where(lane < lo_w // 2, jnp.tanh(x_lo), x_lo)
    wa = _dot(x_lo, wwa_ref[...])
    wpre = w0_ref[...] + wa[:, :rw]
    lw = -DECAY_SCALE * _sigmoid(wpre)
    a = _sigmoid(a0_ref[...] + wa[:, rw:])
    gate = _dot(_sigmoid(gd), wg_ref[...])
    kkv = k * kk_ref[...]
    kkn = kkv * lax.rsqrt(jnp.maximum(segsum(kkv * kkv), 1e-24))
    k2 = k * (1.0 + (a - 1.0) * ka_ref[...])
    return dict(r=r, v=v, lw=lw, a=a, gate=gate, kkn=kkn, k2=k2)


def _decay_factors(f, cum, tot):
    pinc = jnp.exp(cum)
    pinv = jnp.exp(-cum)
    pend = jnp.exp(tot - cum)
    beta = f["kkn"] * f["a"]
    ops = dict(rt=f["r"] * pinc, at=-f["kkn"] * jnp.exp(cum - f["lw"]), bt=beta * pinv, kt=f["k2"] * pinv,
               bdec=beta * pend, kdec=f["k2"] * pend, v=f["v"])
    return dict({name: x.astype(_BF16) for name, x in ops.items()}, pinc=pinc)


def _rwkv_output(yw, f, rk_ref, gnw_ref, gnb_ref, segsum, head_dim):
    inv_n = 1.0 / head_dim
    mean = segsum(yw) * inv_n
    dev = yw - mean
    var = segsum(dev * dev) * inv_n
    yn = dev * lax.rsqrt(var + GN_EPS) * gnw_ref[...] + gnb_ref[...]
    bonus = segsum(f["r"] * f["k2"] * rk_ref[...]) * f["v"]
    return (yn + bonus) * f["gate"]


def _gmlp_branch(p, r_cols, lng_ref, lnb_ref, ws2_ref, tril2, bsf_ref):
    gwid = lng_ref.shape[1]
    ct = ws2_ref.shape[1]
    rows = p.shape[0]
    pu = p[:, r_cols:r_cols + gwid]
    pv = p[:, r_cols + gwid:r_cols + 2 * gwid]
    pm = jnp.mean(pv, -1, keepdims=True)
    pd = pv - pm
    pvar = jnp.mean(pd * pd, -1, keepdims=True)
    vn = pd * lax.rsqrt(pvar + LN_EPS) * lng_ref[...] + lnb_ref[...]
    n_pair = ws2_ref.shape[0]
    pw = gwid // n_pair
    first_half = lax.broadcasted_iota(jnp.int32, (1, pw), 1) < pw // 2
    ws_tril = [(ws2_ref[j] * tril2).astype(_BF16) for j in range(n_pair)]
    mixed_rows = []
    for q in range(rows // ct):
        outs = []
        for j in range(n_pair):
            vp = vn[q * ct:(q + 1) * ct, j * pw:(j + 1) * pw]
            rhs = jnp.concatenate([jnp.where(first_half, vp, 0.0), jnp.where(first_half, 0.0, vp)], axis=0)
            outs.append(_dot(ws_tril[j], rhs))
        mixed_rows.append(jnp.concatenate(outs, axis=1) + bsf_ref[...])
    mixed = mixed_rows[0] if len(mixed_rows) == 1 else jnp.concatenate(mixed_rows, axis=0)
    return pu * mixed, vn


def _state_from_heads(wkv_ref, s, g, head_dim):
    zero = jnp.zeros((head_dim, head_dim), _F32)
    blocks = []
    for h in range(HEAD_GROUP):
        blk = wkv_ref[s, g * HEAD_GROUP + h]
        blocks.append(jnp.concatenate([zero] * h + [blk] + [zero] * (HEAD_GROUP - 1 - h), axis=1))
    return jnp.concatenate(blocks, axis=0)


def _state_to_heads(wkv_ref, s, g, st, head_dim):
    for h in range(HEAD_GROUP):
        wkv_ref[s, g * HEAD_GROUP + h] = st[h * head_dim:(h + 1) * head_dim, h * head_dim:(h + 1) * head_dim]


def _mixer_kernel(p_ref, mu_ref, w0_ref, a0_ref, wwa_ref, wg_ref, kk_ref, ka_ref, rk_ref, gnw_ref, gnb_ref,
                  lng_ref, lnb_ref, ws2_ref, bsf_ref, hmask_ref, hmaskb_ref, tmask_ref, segm_ref, tril2_ref,
                  y_ref, shift_ref, wkv_ref, st_s, carry_s, *, n_heads, head_dim):
    nb, tr, _ = p_ref.shape
    c = WKV_CHUNK
    gw = HEAD_GROUP * head_dim
    n_groups = n_heads // HEAD_GROUP
    rw = n_heads * head_dim
    rows = nb * tr
    ti = pl.program_id(1)
    n_t = pl.num_programs(1)
    r_cols = mu_ref.shape[1]

    @pl.when(ti == 0)
    def _():
        carry_s[...] = jnp.zeros_like(carry_s)
        st_s[...] = jnp.zeros_like(st_s)

    mask = hmask_ref[...]
    mask_bf = hmaskb_ref[...]
    strict, incl, diag = tmask_ref[0], tmask_ref[1], tmask_ref[2]
    tri_bf = segm_ref[...]

    def segsum(x):
        return jnp.concatenate([_dot(x[:, g * gw:(g + 1) * gw], mask_bf) for g in range(n_groups)], axis=1)

    p = p_ref[0] if nb == 1 else jnp.concatenate([p_ref[s] for s in range(nb)], axis=0)
    trow = _mod_pow2(lax.broadcasted_iota(jnp.int32, (rows, 1), 0), tr)
    if nb == 1:
        first = carry_s[0]
    else:
        first = jnp.concatenate([jnp.broadcast_to(carry_s[s], (tr, r_cols)) for s in range(nb)], axis=0)
    for s in range(nb):
        carry_s[s] = p_ref[s, tr - 1:tr, :r_cols]
    f = _token_features(p, first, trow, (mu_ref, w0_ref, a0_ref, wwa_ref, wg_ref, kk_ref, ka_ref), segsum, rw)
    sr = tri_bf.shape[0]
    cum = jnp.concatenate([_dot_sel_lhs(tri_bf, f["lw"][i * sr:(i + 1) * sr]) for i in range(rows // sr)],
                          axis=0)
    tot = jnp.concatenate([jnp.broadcast_to(cum[j * c + c - 1:j * c + c], (c, rw)) for j in range(rows // c)],
                          axis=0)
    d = _decay_factors(f, cum, tot)

    n_chunks = rows // c
    chunks_per_seq = tr // c

    def blk(x, j, g):
        return x[j * c:(j + 1) * c, g * gw:(g + 1) * gw]

    probs = [(j, g) for j in range(n_chunks) for g in range(n_groups)]
    cut = lambda x, ps: [blk(x, j, g) for j, g in ps]
    local = dict(zip(probs, _wkv_local(cut(d["rt"], probs), cut(d["at"], probs), cut(d["bt"], probs),
                                       cut(d["kt"], probs), cut(d["v"], probs), mask_bf, strict, incl, diag)))
    y_c, _ = _gmlp_branch(p, r_cols, lng_ref, lnb_ref, ws2_ref, tril2_ref[...], bsf_ref)
    y_blk = {}
    for jj in range(chunks_per_seq):
        ps = [(s * chunks_per_seq + jj, g) for s in range(nb) for g in range(n_groups)]
        loc = [local[pr_] for pr_ in ps]
        sts = [st_s[j // chunks_per_seq, g] for j, g in ps]
        pcs = [blk(d["pinc"], j, g)[c - 1:c] for j, g in ps]
        u, ys, st_new = _wkv_state_step([l[0] for l in loc], cut(d["rt"], ps), [l[1] for l in loc],
                                        cut(d["v"], ps), cut(d["bdec"], ps), cut(d["kdec"], ps), pcs, sts, mask)
        for i, (j, g) in enumerate(ps):
            st_s[j // chunks_per_seq, g] = st_new[i]
            y_blk[j, g] = ys[i] + _dot(loc[i][2], _block_diag(u[i], mask_bf)) + loc[i][3]
    y_rows = [jnp.concatenate([y_blk[j, g] for g in range(n_groups)], axis=1) for j in range(n_chunks)]
    yw = y_rows[0] if n_chunks == 1 else jnp.concatenate(y_rows, axis=0)

    y_r = _rwkv_output(yw, f, rk_ref, gnw_ref, gnb_ref, segsum, head_dim)
    y_all = jnp.concatenate([y_r, y_c], axis=1)
    for s in range(nb):
        y_ref[s] = y_all[s * tr:(s + 1) * tr]

    @pl.when(ti == n_t - 1)
    def _():
        shift_ref[...] = carry_s[...]
        for s in range(nb):
            for g in range(n_groups):
                _state_to_heads(wkv_ref, s, g, st_s[s, g], head_dim)


def _mixer_short_kernel(p_ref, shift0_ref, wkv0_ref, mu_ref, w0_ref, a0_ref, wwa_ref, wg_ref, kk_ref, ka_ref,
                        rk_ref, gnw_ref, gnb_ref, lng_ref, lnb_ref, ws2_ref, bsf_ref, hmask_ref, hmaskb_ref,
                        tmask_ref, segm_ref, tril2_ref, y_ref, shift_ref, wkv_ref, vn_ref, *, n_heads, head_dim):
    nb, ts, _ = p_ref.shape
    c = WKV_CHUNK
    assert nb * ts == c
    gw = HEAD_GROUP * head_dim
    n_groups = n_heads // HEAD_GROUP
    rw = n_heads * head_dim
    r_cols = mu_ref.shape[1]

    mask = hmask_ref[...]
    mask_bf = hmaskb_ref[...]
    strict, incl = tmask_ref[0], tmask_ref[1]

    def segsum(x):
        return jnp.concatenate([_dot(x[:, g * gw:(g + 1) * gw], mask_bf) for g in range(n_groups)], axis=1)

    p = p_ref[...].reshape(c, p_ref.shape[2])
    trow = _mod_pow2(lax.broadcasted_iota(jnp.int32, (c, 1), 0), ts)
    first = jnp.concatenate([jnp.broadcast_to(shift0_ref[s], (ts, r_cols)) for s in range(nb)], axis=0)
    for s in range(nb):
        shift_ref[s] = p_ref[s, ts - 1:ts, :r_cols]
    f = _token_features(p, first, trow, (mu_ref, w0_ref, a0_ref, wwa_ref, wg_ref, kk_ref, ka_ref), segsum, rw)
    cum = _dot_sel_lhs(segm_ref[...], f["lw"])
    tot = jnp.concatenate([jnp.broadcast_to(cum[s * ts + ts - 1:(s + 1) * ts], (ts, rw)) for s in range(nb)],
                          axis=0)
    d = _decay_factors(f, cum, tot)

    groups = range(n_groups)
    gcut = lambda x: [x[:, g * gw:(g + 1) * gw] for g in groups]
    local = _wkv_local(gcut(d["rt"]), gcut(d["at"]), gcut(d["bt"]), gcut(d["kt"]), gcut(d["v"]), mask_bf,
                       strict, incl, ts)

    ps = [(s, g) for s in range(nb) for g in groups]
    scut = lambda x: [x[s * ts:(s + 1) * ts, g * gw:(g + 1) * gw] for s, g in ps]
    lcut = lambda idx: [local[g][idx][s * ts:(s + 1) * ts] for s, g in ps]
    sts = [_state_from_heads(wkv0_ref, s, g, head_dim) for s, g in ps]
    pcs = [d["pinc"][s * ts + ts - 1:(s + 1) * ts, g * gw:(g + 1) * gw] for s, g in ps]
    u, ys, st_new = _wkv_state_step(lcut(0), scut(d["rt"]), lcut(1), scut(d["v"]), scut(d["bdec"]),
                                    scut(d["kdec"]), pcs, sts, mask)
    for i, (s, g) in enumerate(ps):
        _state_to_heads(wkv_ref, s, g, st_new[i], head_dim)
    rows_of = lambda xs, g: jnp.concatenate([xs[i] for i, (s, gg) in enumerate(ps) if gg == g], axis=0)
    yw = jnp.concatenate([rows_of(ys, g) + _dot(local[g][2], _block_diag(rows_of(u, g), mask_bf)) + local[g][3]
                          for g in groups], axis=1)

    y_r = _rwkv_output(yw, f, rk_ref, gnw_ref, gnb_ref, segsum, head_dim)
    y_c, vn = _gmlp_branch(p, r_cols, lng_ref, lnb_ref, ws2_ref, tril2_ref[...], bsf_ref)
    y_ref[...] = jnp.concatenate([y_r, y_c], axis=1).reshape(nb, ts, -1)
    vn_ref[...] = vn.reshape(nb, ts, -1)


def _mixer_masks(rows, seq, ct, head_dim):
    c, gw = WKV_CHUNK, HEAD_GROUP * head_dim
    blk = np.arange(gw) // head_dim
    hmask = (blk[:, None] == blk[None, :]).astype(np.float32)
    t = np.arange(c)[:, None]
    s = np.arange(HEAD_GROUP * c)[None, :] % c
    strict = (s < t) & (s // seq == t // seq)
    incl = (s <= t) & (s // seq == t // seq)
    tmask = np.stack([strict, incl, strict & (s // INV_BLOCK == t // INV_BLOCK)]).astype(np.float32)
    r = np.arange(rows)
    seg = min(c, seq)
    same = (r[:, None] // seg) == (r[None, :] // seg)
    segm = (same & (r[None, :] <= r[:, None])).astype(np.float32)
    ctm = np.arange(ct)
    tril1 = ((ctm[None, :] <= ctm[:, None]) & (ctm[None, :] // seq == ctm[:, None] // seq))
    tril2 = np.concatenate([tril1, tril1], axis=1).astype(np.float32)
    return [jnp.asarray(hmask), jnp.asarray(hmask, _BF16), jnp.asarray(tmask), jnp.asarray(segm, _BF16),
            jnp.asarray(tril2)]


def _gmlp_weights(w_s, b_s, ct, seq, gwid):
    hc = w_s.shape[0]
    n_seq = ct // min(seq, ct)
    ws = w_s[:, :ct // n_seq, :ct // n_seq]
    bs = b_s[:, :ct // n_seq]
    if n_seq > 1:
        ws = jnp.einsum("st,hij->hsitj", jnp.eye(n_seq, dtype=ws.dtype), ws).reshape(hc, ct, ct)
        bs = jnp.tile(bs, (1, n_seq))
    ws2 = ws.reshape(hc // 2, 2, ct, ct).transpose(0, 2, 1, 3).reshape(hc // 2, ct, 2 * ct)
    return ws2, jnp.repeat(bs.T, gwid // hc, axis=1)


def _mixer_params(prm, rows, seq, ct, head_dim):
    gwid = prm["lng"].shape[1]
    ws2, bsf = _gmlp_weights(prm["w_s"], prm["b_s"], ct, seq, gwid)
    params = [prm["mu"], prm["w0"], prm["a0"], prm["wwa"], prm["wg"], prm["k_k"], prm["k_a"], prm["r_k"],
              prm["gn_w"], prm["gn_b"], prm["lng"], prm["lnb"], ws2, bsf]
    return params + _mixer_masks(rows, seq, ct, head_dim)


def _full_spec(a):
    nd = a.ndim
    return pl.BlockSpec(a.shape, lambda b, i: (0,) * nd)


def _mixer_call(p, prm, *, nb, tr, n_heads, head_dim):
    nseq, t, pcols = p.shape
    rw = n_heads * head_dim
    r_cols = prm["mu"].shape[1]
    gwid = prm["lng"].shape[1]
    gw = HEAD_GROUP * head_dim
    n_groups = n_heads // HEAD_GROUP
    rows = nb * tr
    assert nseq % nb == 0 and t % tr == 0 and tr % GMLP_CHUNK == 0 and rows % SEG_ROWS == 0
    params = _mixer_params(prm, SEG_ROWS, t, GMLP_CHUNK, head_dim)
    return pl.pallas_call(
        functools.partial(_mixer_kernel, n_heads=n_heads, head_dim=head_dim),
        out_shape=(jax.ShapeDtypeStruct((nseq, t, rw + gwid), _F32),
                   jax.ShapeDtypeStruct((nseq, 1, r_cols), _F32),
                   jax.ShapeDtypeStruct((nseq, n_heads, head_dim, head_dim), _F32)),
        grid=(nseq // nb, t // tr),
        in_specs=[pl.BlockSpec((nb, tr, pcols), lambda b, i: (b, i, 0))] + [_full_spec(a) for a in params],
        out_specs=(pl.BlockSpec((nb, tr, rw + gwid), lambda b, i: (b, i, 0)),
                   pl.BlockSpec((nb, 1, r_cols), lambda b, i: (b, 0, 0)),
                   pl.BlockSpec((nb, n_heads, head_dim, head_dim), lambda b, i: (b, 0, 0, 0))),
        scratch_shapes=[pltpu.VMEM((nb, n_groups, gw, gw), _F32), pltpu.VMEM((nb, 1, r_cols), _F32)],
        compiler_params=pltpu.CompilerParams(dimension_semantics=("arbitrary", "arbitrary"),
                                             vmem_limit_bytes=_VMEM_LIMIT),
        name="mixers_fresh",
    )(p, *params)


def _mixer_short_call(p, shift0, wkv0, prm, *, n_heads, head_dim):
    nseq, ts, pcols = p.shape
    nb = WKV_CHUNK // ts
    assert nb * ts == WKV_CHUNK and ts % SUBLANES == 0 and nseq % nb == 0
    rw = n_heads * head_dim
    r_cols = prm["mu"].shape[1]
    gwid = prm["lng"].shape[1]
    params = _mixer_params(prm, WKV_CHUNK, ts, WKV_CHUNK, head_dim)
    seqs = lambda *tail: pl.BlockSpec((nb,) + tail, lambda b, i: (b,) + (0,) * len(tail))
    return pl.pallas_call(
        functools.partial(_mixer_short_kernel, n_heads=n_heads, head_dim=head_dim),
        out_shape=(jax.ShapeDtypeStruct((nseq, ts, rw + gwid), _F32),
                   jax.ShapeDtypeStruct((nseq, 1, r_cols), _F32),
                   jax.ShapeDtypeStruct((nseq, n_heads, head_dim, head_dim), _F32),
                   jax.ShapeDtypeStruct((nseq, ts, gwid), _F32)),
        grid=(nseq // nb, 1),
        in_specs=[seqs(ts, pcols), seqs(1, r_cols), seqs(n_heads, head_dim, head_dim)]
                 + [_full_spec(a) for a in params],
        out_specs=(seqs(ts, rw + gwid), seqs(1, r_cols), seqs(n_heads, head_dim, head_dim), seqs(ts, gwid)),
        compiler_params=pltpu.CompilerParams(dimension_semantics=("arbitrary", "arbitrary"),
                                             vmem_limit_bytes=_VMEM_LIMIT),
        name="mixers_state",
    )(p, shift0, wkv0, *params)


def _pick_tile(n, pref):
    t = min(n, pref)
    while n % t:
        t //= 2
    return t


def kernel(x_prompt, x_sample, state_shift, state_wkv, c_prompt, c_sample, w_ada, b_ada, ffn1_gu, ffn1_dn, w_in, mu_shift, w0, w_lora_up, a0, a_lora_up, g_lora_up, k_k, k_a, r_k, gn_w, gn_b, ln_v_g, ln_v_b, w_s, b_s, w_out, ffn2_gu, ffn2_dn, final_g):
    depth = w_ada.shape[0]
    bp, tp, d = x_prompt.shape
    bs, ts, _ = x_sample.shape
    n_heads, head_dim = r_k.shape[1], r_k.shape[2]
    rw = n_heads * head_dim
    w_rank, a_rank = w_lora_up.shape[1], a_lora_up.shape[1]
    assert w_rank == a_rank and n_heads % HEAD_GROUP == 0 and head_dim == WKV_CHUNK
    d_ff = ffn1_dn.shape[1]
    tf = 256 if d_ff % 256 == 0 else 128

    xp = x_prompt.reshape(bp * tp, d)
    xs = x_sample.reshape(bs * ts, d)
    assert bs % SUBLANES == 0
    n_c = bs + bp
    n_c_pad = -(-n_c // 16) * 16
    c_all = jnp.concatenate([c_sample, c_prompt, jnp.zeros((n_c_pad - n_c, d), _F32)], axis=0)

    tm_p = _pick_tile(tp, 512)
    tr_p = _pick_tile(tp, 256)
    tm_s = ts * _pick_tile(bs, 64)
    nb_p = _pick_tile(bp, 2)

    shp, wkp, shs, wks, cvs = [], [], [], [], []
    for l in range(depth):
        mod = _mod_call(c_all, w_ada[l], b_ada[l])
        grp_p = dict(tm=tm_p, seq0=bs, seqs_per_tile=1, tiles_per_seq=tp // tm_p)
        grp_s = dict(tm=tm_s, seq0=0, seqs_per_tile=tm_s // ts, tiles_per_seq=1)

        zw = jnp.zeros((w_rank, rw), _F32)
        wwa = jnp.concatenate([jnp.concatenate([w_lora_up[l], zw], axis=1),
                               jnp.concatenate([zw, a_lora_up[l]], axis=1)], axis=0).astype(_BF16)
        row = lambda a: a.reshape(1, -1)
        prm = dict(mu=row(mu_shift[l]), w0=row(w0[l]), a0=row(a0[l]), wwa=wwa,
                   wg=g_lora_up[l].astype(_BF16), k_k=row(k_k[l]), k_a=row(k_a[l]), r_k=row(r_k[l]),
                   gn_w=row(gn_w[l]), gn_b=row(gn_b[l]), lng=row(ln_v_g[l]), lnb=row(ln_v_b[l]),
                   w_s=w_s[l], b_s=b_s[l])
        fg = final_g.reshape(1, d)
        final = l == depth - 1

        ((hp, pp),), w1_bf = _ffn_in_call([dict(grp_p, acts=[xp])], mod, ffn1_gu[l], ffn1_dn[l], w_in[l],
                                          tf=tf, stage=WEIGHT_STAGE_STEPS, export=True)
        ((hs, ps),), _ = _ffn_in_call([dict(grp_s, acts=[xs])], mod, *w1_bf, tf=tf, stage=0)
        ymp, sh_p, wk_p = _mixer_call(pp.reshape(bp, tp, -1), prm, nb=nb_p, tr=tr_p,
                                      n_heads=n_heads, head_dim=head_dim)
        yms, sh_s, wk_s, v_s = _mixer_short_call(ps.reshape(bs, ts, -1), state_shift[l][:, None, :], state_wkv[l], prm,
                                                 n_heads=n_heads, head_dim=head_dim)
        half_s = dict(grp_s, tm=tm_s // 2, seqs_per_tile=tm_s // ts // 2)
        (xp,), (xs,) = _out_ffn_call([dict(grp_p, acts=[ymp.reshape(bp * tp, -1), hp]),
                                      dict(half_s, acts=[yms.reshape(bs * ts, -1), hs])], mod,
                                     w_out[l], ffn2_gu[l], ffn2_dn[l], fg, tf=tf, final=final,
                                     stage=WEIGHT_STAGE_STEPS)

        shp.append(sh_p[:, 0]); wkp.append(wk_p); shs.append(sh_s[:, 0]); wks.append(wk_s); cvs.append(v_s)

    y_prompt = xp.reshape(bp, tp, d)
    y_sample = xs.reshape(bs, ts, d)
    return (y_prompt, y_sample, jnp.stack(shp), jnp.stack(wkp), jnp.stack(shs), jnp.stack(wks), jnp.stack(cvs))
```

```python
import functools
import math

import jax
import jax.numpy as jnp
import numpy as np
from jax import lax
from jax.experimental import pallas as pl
from jax.experimental.pallas import tpu as pltpu

_F32 = jnp.float32
_BF16 = jnp.bfloat16

RMS_EPS = 1e-6
LN_EPS = 1e-5
GN_EPS = 64e-5
N_MOD = 9
WKV_CHUNK = 64
HEAD_GROUP = 4
INV_BLOCK = 16
GMLP_CHUNK = 128
SEG_ROWS = 256
SUBLANES = 8
MOD_PER_STEP = 3
WEIGHT_STAGE_STEPS = 16
DECAY_SCALE = math.exp(-0.5)

_V7X_VMEM_BYTES = 64 * 1024 * 1024
_VMEM_LIMIT = _V7X_VMEM_BYTES - 8 * 1024 * 1024


def _dot(a, b):
    return jnp.dot(a.astype(_BF16), b.astype(_BF16), preferred_element_type=_F32)


def _dot_nt(a, b):
    return lax.dot_general(a.astype(_BF16), b.astype(_BF16), (((1,), (1,)), ((), ())),
                           preferred_element_type=_F32)


def _dot_tn(a, b):
    return lax.dot_general(a.astype(_BF16), b.astype(_BF16), (((0,), (0,)), ((), ())),
                           preferred_element_type=_F32)


def _split_hi_lo(x):
    hi = x.astype(_BF16)
    lo = (x - hi.astype(_F32)).astype(_BF16)
    return hi, lo


def _dot_sel_lhs(sel, x):
    hi, lo = _split_hi_lo(x)
    return (jnp.dot(sel, hi, preferred_element_type=_F32)
            + jnp.dot(sel, lo, preferred_element_type=_F32))


def _mod_pow2(x, n):
    assert n & (n - 1) == 0
    return lax.bitwise_and(x, jnp.int32(n - 1))


def _rms(x):
    return x * lax.rsqrt(jnp.mean(x * x, -1, keepdims=True) + RMS_EPS)


def _sigmoid(x):
    return 1.0 / (1.0 + jnp.exp(-x))


def _affine(x, scale, shift=None):
    mb = scale.shape[0]
    if mb == 1:
        y = x * scale
        return y if shift is None else y + shift
    tm, d = x.shape
    y = x.reshape(mb, tm // mb, d) * scale[:, None, :]
    if shift is not None:
        y = y + shift[:, None, :]
    return y.reshape(tm, d)


def _swiglu(n_bf, wgu_ref, wdn_ref, tf):
    d_ff = wdn_ref.shape[0]
    acc = None
    for j in range(d_ff // tf):
        g = jnp.dot(n_bf, wgu_ref[:, j * tf:(j + 1) * tf], preferred_element_type=_F32)
        u = jnp.dot(n_bf, wgu_ref[:, d_ff + j * tf:d_ff + (j + 1) * tf], preferred_element_type=_F32)
        hm = (g * _sigmoid(g) * u).astype(_BF16)
        part = jnp.dot(hm, wdn_ref[j * tf:(j + 1) * tf, :], preferred_element_type=_F32)
        acc = part if acc is None else acc + part
    return acc


def _mod_kernel(c_ref, w_ref, b_ref, o_ref):
    c = c_ref[...]
    s = c * _sigmoid(c)
    o = _dot(s, w_ref[...]) + b_ref[...]
    d = c.shape[1]
    for k in range(o_ref.shape[0]):
        o_ref[k] = o[:, k * d:(k + 1) * d]


def _mod_call(c_all, w_ada, b_ada):
    n, d = c_all.shape
    per_step = MOD_PER_STEP if N_MOD % MOD_PER_STEP == 0 else 1
    return pl.pallas_call(
        _mod_kernel,
        out_shape=jax.ShapeDtypeStruct((N_MOD, n, d), _F32),
        grid=(N_MOD // per_step,),
        in_specs=[pl.BlockSpec((n, d), lambda j: (0, 0)),
                  pl.BlockSpec((d, per_step * d), lambda j: (0, j)),
                  pl.BlockSpec((1, per_step * d), lambda j: (0, j))],
        out_specs=pl.BlockSpec((per_step, n, d), lambda j: (j, 0, 0)),
        compiler_params=pltpu.CompilerParams(dimension_semantics=("arbitrary",),
                                             vmem_limit_bytes=_VMEM_LIMIT),
        name="adaln_mod",
    )(c_all, w_ada, b_ada.reshape(1, N_MOD * d))


def _mod_rows(mod_ref, pick, tile):
    if pick is None:
        return lambda j: mod_ref[j]
    seq0, tiles_per_seq = pick
    row = lax.rem(seq0 + lax.div(tile, jnp.int32(tiles_per_seq)), jnp.int32(SUBLANES))
    return lambda j: mod_ref[j, pl.ds(row, 1), :]


def _mod_spec(mod, seq0, seqs_per_tile, tiles_per_seq, tile):
    n_mod, _, d = mod.shape
    if seqs_per_tile > 1:
        assert seqs_per_tile % SUBLANES == 0 and seq0 % seqs_per_tile == 0 and tiles_per_seq == 1
        first = seq0 // seqs_per_tile
        return pl.BlockSpec((n_mod, seqs_per_tile, d), lambda i: (0, first + tile(i), 0)), None
    return (pl.BlockSpec((n_mod, SUBLANES, d), lambda i: (0, (seq0 + tile(i) // tiles_per_seq) // SUBLANES, 0)),
            (seq0, tiles_per_seq))


def _resident(shape):
    nd = len(shape)
    return pl.BlockSpec(shape, lambda *_: (0,) * nd, pipeline_mode=pl.Buffered(1))


def _dense_call(body, name, groups, mod, weights, extra, out_widths, *, stage, export=False):
    starts, n_steps = [], stage
    for g in groups:
        starts.append(n_steps)
        n_steps += g["acts"][0].shape[0] // g["tm"]
    n_w, n_x, n_out = len(weights), len(extra), len(out_widths)

    def tile_of(k):
        last = groups[k]["acts"][0].shape[0] // groups[k]["tm"] - 1
        return lambda i: jnp.clip(i - starts[k], 0, last)

    in_specs, picks, inputs = [], [], []
    for k, g in enumerate(groups):
        tile = tile_of(k)
        mod_spec, pick = _mod_spec(mod, g["seq0"], g["seqs_per_tile"], g["tiles_per_seq"], tile)
        picks.append(pick)
        in_specs += [pl.BlockSpec((g["tm"], a.shape[1]), lambda i, tile=tile: (tile(i), 0)) for a in g["acts"]]
        in_specs.append(mod_spec)
        inputs += list(g["acts"]) + [mod]
    out_shape, out_specs = [], []
    for k, g in enumerate(groups):
        tile = tile_of(k)
        out_shape += [jax.ShapeDtypeStruct((g["acts"][0].shape[0], w), _F32) for w in out_widths]
        out_specs += [pl.BlockSpec((g["tm"], w), lambda i, tile=tile: (tile(i), 0)) for w in out_widths]
    scratch = []
    if stage:
        assert all(w.shape[0] % (stage * 2 * SUBLANES) == 0 for w in weights)
        chunk_spec = lambda w: pl.BlockSpec((w.shape[0] // stage, w.shape[1]),
                                            lambda i: (jnp.minimum(i, stage - 1), 0))
        in_specs += [chunk_spec(w) for w in weights]
        scratch = [pltpu.VMEM(w.shape, _BF16) for w in weights]
        if export:
            out_shape += [jax.ShapeDtypeStruct(w.shape, _BF16) for w in weights]
            out_specs += [chunk_spec(w) for w in weights]
    else:
        in_specs += [_resident(w.shape) for w in weights]
    in_specs += [_resident(e.shape) for e in extra]

    def kernel(*refs):
        it = iter(refs)
        act_refs, mod_refs = [], []
        for g in groups:
            act_refs.append([next(it) for _ in g["acts"]])
            mod_refs.append(next(it))
        w_in_refs = [next(it) for _ in range(n_w)]
        x_refs = [next(it) for _ in range(n_x)]
        out_refs = [[next(it) for _ in range(n_out)] for _ in groups]
        w_out_refs = [next(it) for _ in range(n_w)] if stage and export else [None] * n_w
        w_refs = [next(it) for _ in range(n_w)] if stage else w_in_refs
        step = pl.program_id(0)

        if stage:
            @pl.when(step < stage)
            def _():
                for src, dst, out in zip(w_in_refs, w_refs, w_out_refs):
                    chunk = src.shape[0]
                    v = src[...].astype(_BF16)
                    dst[pl.ds(pl.multiple_of(step * chunk, chunk), chunk), :] = v
                    if out is not None:
                        out[...] = v

        for k in range(len(groups)):
            end = starts[k + 1] if k + 1 < len(groups) else n_steps

            def run(k=k):
                body(act_refs[k], _mod_rows(mod_refs[k], picks[k], step - starts[k]), w_refs, x_refs, out_refs[k])

            if len(groups) == 1 and not stage:
                run()
            else:
                pl.when((step >= starts[k]) & (step < end))(run)

    outs = pl.pallas_call(
        kernel,
        out_shape=tuple(out_shape),
        grid=(n_steps,),
        in_specs=in_specs,
        out_specs=tuple(out_specs),
        scratch_shapes=scratch,
        compiler_params=pltpu.CompilerParams(dimension_semantics=("arbitrary",),
                                             vmem_limit_bytes=_VMEM_LIMIT),
        name=name,
    )(*inputs, *weights, *extra)
    per_group = [outs[k * n_out:(k + 1) * n_out] for k in range(len(groups))]
    return per_group, list(outs[len(groups) * n_out:])


def _ffn_in_body(acts, mod, weights, extra, outs, *, tf):
    (x_ref,), (wgu_ref, wdn_ref, win_ref), (h_ref, p_ref) = acts, weights, outs
    x = x_ref[...]
    sh1, sc1, g1, sh2, sc2 = (mod(j) for j in range(5))
    n1 = _affine(_rms(x), 1.0 + sc1, sh1).astype(_BF16)
    h = x + _affine(_swiglu(n1, wgu_ref, wdn_ref, tf), 0.5 * g1)
    h_ref[...] = h
    n2 = _affine(_rms(h), 1.0 + sc2, sh2).astype(_BF16)
    p_ref[...] = jnp.dot(n2, win_ref[...], preferred_element_type=_F32)


def _ffn_in_call(groups, mod, wgu, wdn, win, *, tf, stage, export=False):
    d = groups[0]["acts"][0].shape[1]
    return _dense_call(functools.partial(_ffn_in_body, tf=tf), "ffn1_inproj", groups, mod, [wgu, wdn, win], [],
                       [d, win.shape[1]], stage=stage, export=export)


def _out_ffn_body(acts, mod, weights, extra, outs, *, tf, final):
    (ym_ref, h_ref), (wout_ref, wgu_ref, wdn_ref), (fg_ref,), (o_ref,) = acts, weights, extra, outs
    g2, sh3, sc3, g3 = (mod(j) for j in range(5, 9))
    h = h_ref[...] + _affine(jnp.dot(ym_ref[...].astype(_BF16), wout_ref[...],
                                     preferred_element_type=_F32), g2)
    n3 = _affine(_rms(h), 1.0 + sc3, sh3).astype(_BF16)
    h = h + _affine(_swiglu(n3, wgu_ref, wdn_ref, tf), 0.5 * g3)
    if final:
        h = _rms(h) * fg_ref[...]
    o_ref[...] = h


def _out_ffn_call(groups, mod, wout, wgu, wdn, fg, *, tf, final, stage):
    d = wdn.shape[1]
    return _dense_call(functools.partial(_out_ffn_body, tf=tf, final=final), "outproj_ffn2", groups, mod,
                       [wout, wgu, wdn], [fg], [d], stage=stage)[0]


def _rows_bf16(*xs):
    return jnp.concatenate([x.astype(_BF16) for x in xs], axis=0)


def _block_diag(y, mask_bf):
    reps = mask_bf.shape[0] // y.shape[0]
    return jnp.concatenate([y.astype(_BF16)] * reps, axis=0) * mask_bf


def _nilpotent_inverse(n, bd, order):
    every = range(len(n))
    c = n[0].shape[0]
    tm = n
    levels = order.bit_length() - 2
    if levels <= 0:
        return tm
    nb = bd(n)
    npow = [_dot(n[i], nb[i]) for i in every]
    for lvl in range(levels):
        nb = bd(npow)
        if lvl + 1 < levels:
            res = [_dot(jnp.concatenate([tm[i], npow[i]], axis=0), nb[i]) for i in every]
            tm = [tm[i] + npow[i] + res[i][:c] for i in every]
            npow = [res[i][c:] for i in every]
        else:
            tm = [tm[i] + npow[i] + _dot(tm[i], nb[i]) for i in every]
    return tm


def _wkv_local(rt, at, bt, kt, vv, mask_bf, strict, incl, diag):
    n = len(rt)
    c = rt[0].shape[0]
    every = range(n)
    bd = lambda xs: [_block_diag(x, mask_bf) for x in xs]
    x = [_rows_bf16(at[i], rt[i]) for i in every]
    bb, kb = bd(bt), bd(kt)
    xb = [_dot_nt(x[i], bb[i]) for i in every]
    xk = [_dot_nt(x[i], kb[i]) for i in every]
    lab = [xb[i][:c] * strict for i in every]
    mrb = [xb[i][c:] * incl for i in every]
    lak = [xk[i][:c] * strict for i in every]
    mrk = [xk[i][c:] * incl for i in every]
    vb = bd(vv)
    wy = [_dot(jnp.concatenate([lak[i], mrk[i]], axis=0), vb[i]) for i in every]
    w1 = [wy[i][:c] for i in every]
    y0 = [wy[i][c:] for i in every]
    if isinstance(diag, int):
        tm = _nilpotent_inverse(lab, bd, diag)
    else:
        assert c == 4 * INV_BLOCK
        nd = [lab[i] * diag for i in every]
        noff = [lab[i] - nd[i] for i in every]
        tdm = _nilpotent_inverse(nd, bd, INV_BLOCK)
        noffb = bd(noff)
        m1 = [noff[i] + _dot(tdm[i], noffb[i]) for i in every]
        m1b = bd(m1)
        m2 = [_dot(m1[i], m1b[i]) for i in every]
        m2b = bd(m2)
        q = [m1[i] + m2[i] + _dot(m1[i], m2b[i]) for i in every]
        tdmb = bd(tdm)
        tm = [q[i] + tdm[i] + _dot(q[i], tdmb[i]) for i in every]
    atb = bd(at)
    ap = [at[i] + _dot(tm[i], atb[i]) for i in every]
    w1b = bd(w1)
    u0 = [w1[i] + _dot(tm[i], w1b[i]) for i in every]
    return [(ap[i], u0[i], mrb[i], y0[i]) for i in every]


def _wkv_state_step(ap, rt, u0, vv, bdec, kdec, pc, st, mask):
    every = range(len(rt))
    c = rt[0].shape[0]
    ar = [_dot_nt(_rows_bf16(ap[i], rt[i]), st[i]) for i in every]
    u = [ar[i][:c] + u0[i] for i in every]
    upd = [_dot_tn(_rows_bf16(u[i], vv[i]), _rows_bf16(bdec[i], kdec[i])) for i in every]
    st_new = [st[i] * pc[i] + upd[i] * mask for i in every]
    return u, [ar[i][c:] for i in every], st_new


def _token_features(p, first, trow, prm, segsum, rw):
    mu_ref, w0_ref, a0_ref, wwa_ref, wg_ref, kk_ref, ka_ref = prm
    r_cols = mu_ref.shape[1]
    pr = p[:, :r_cols]
    prev = jnp.where(trow == 0, first, pltpu.roll(pr, 1, 0))
    xm = pr + (prev - pr) * mu_ref[...]
    r = xm[:, 0:rw]
    k = xm[:, rw:2 * rw]
    v = xm[:, 2 * rw:3 * rw]
    lo_w = wwa_ref.shape[0]
    x_lo = xm[:, 3 * rw:3 * rw + lo_w]
    gd = xm[:, 3 * rw + lo_w:r_cols]
    lane = lax.broadcasted_iota(jnp.int32, (1, lo_w), 1)
    x_lo = jnp.where(lane < lo_w // 2, jnp.tanh(x_lo), x_lo)
    wa = _dot(x_lo, wwa_ref[...])
    wpre = w0_ref[...] + wa[:, :rw]
    lw = -DECAY_SCALE * _sigmoid(wpre)
    a = _sigmoid(a0_ref[...] + wa[:, rw:])
    gate = _dot(_sigmoid(gd), wg_ref[...])
    kkv = k * kk_ref[...]
    kkn = kkv * lax.rsqrt(jnp.maximum(segsum(kkv * kkv), 1e-24))
    k2 = k * (1.0 + (a - 1.0) * ka_ref[...])
    return dict(r=r, v=v, lw=lw, a=a, gate=gate, kkn=kkn, k2=k2)


def _decay_factors(f, cum, tot):
    pinc = jnp.exp(cum)
    pinv = jnp.exp(-cum)
    pend = jnp.exp(tot - cum)
    beta = f["kkn"] * f["a"]
    ops = dict(rt=f["r"] * pinc, at=-f["kkn"] * jnp.exp(cum - f["lw"]), bt=beta * pinv, kt=f["k2"] * pinv,
               bdec=beta * pend, kdec=f["k2"] * pend, v=f["v"])
    return dict({name: x.astype(_BF16) for name, x in ops.items()}, pinc=pinc)


def _rwkv_output(yw, f, rk_ref, gnw_ref, gnb_ref, segsum, head_dim):
    inv_n = 1.0 / head_dim
    mean = segsum(yw) * inv_n
    dev = yw - mean
    var = segsum(dev * dev) * inv_n
    yn = dev * lax.rsqrt(var + GN_EPS) * gnw_ref[...] + gnb_ref[...]
    bonus = segsum(f["r"] * f["k2"] * rk_ref[...]) * f["v"]
    return (yn + bonus) * f["gate"]


def _gmlp_branch(p, r_cols, lng_ref, lnb_ref, ws2_ref, tril2, bsf_ref):
    gwid = lng_ref.shape[1]
    ct = ws2_ref.shape[1]
    rows = p.shape[0]
    pu = p[:, r_cols:r_cols + gwid]
    pv = p[:, r_cols + gwid:r_cols + 2 * gwid]
    pm = jnp.mean(pv, -1, keepdims=True)
    pd = pv - pm
    pvar = jnp.mean(pd * pd, -1, keepdims=True)
    vn = pd * lax.rsqrt(pvar + LN_EPS) * lng_ref[...] + lnb_ref[...]
    n_pair = ws2_ref.shape[0]
    pw = gwid // n_pair
    first_half = lax.broadcasted_iota(jnp.int32, (1, pw), 1) < pw // 2
    ws_tril = [(ws2_ref[j] * tril2).astype(_BF16) for j in range(n_pair)]
    mixed_rows = []
    for q in range(rows // ct):
        outs = []
        for j in range(n_pair):
            vp = vn[q * ct:(q + 1) * ct, j * pw:(j + 1) * pw]
            rhs = jnp.concatenate([jnp.where(first_half, vp, 0.0), jnp.where(first_half, 0.0, vp)], axis=0)
            outs.append(_dot(ws_tril[j], rhs))
        mixed_rows.append(jnp.concatenate(outs, axis=1) + bsf_ref[...])
    mixed = mixed_rows[0] if len(mixed_rows) == 1 else jnp.concatenate(mixed_rows, axis=0)
    return pu * mixed, vn


def _state_from_heads(wkv_ref, s, g, head_dim):
    zero = jnp.zeros((head_dim, head_dim), _F32)
    blocks = []
    for h in range(HEAD_GROUP):
        blk = wkv_ref[s, g * HEAD_GROUP + h]
        blocks.append(jnp.concatenate([zero] * h + [blk] + [zero] * (HEAD_GROUP - 1 - h), axis=1))
    return jnp.concatenate(blocks, axis=0)


def _state_to_heads(wkv_ref, s, g, st, head_dim):
    for h in range(HEAD_GROUP):
        wkv_ref[s, g * HEAD_GROUP + h] = st[h * head_dim:(h + 1) * head_dim, h * head_dim:(h + 1) * head_dim]


def _mixer_kernel(p_ref, mu_ref, w0_ref, a0_ref, wwa_ref, wg_ref, kk_ref, ka_ref, rk_ref, gnw_ref, gnb_ref,
                  lng_ref, lnb_ref, ws2_ref, bsf_ref, hmask_ref, hmaskb_ref, tmask_ref, segm_ref, tril2_ref,
                  y_ref, shift_ref, wkv_ref, st_s, carry_s, *, n_heads, head_dim):
    nb, tr, _ = p_ref.shape
    c = WKV_CHUNK
    gw = HEAD_GROUP * head_dim
    n_groups = n_heads // HEAD_GROUP
    rw = n_heads * head_dim
    rows = nb * tr
    ti = pl.program_id(1)
    n_t = pl.num_programs(1)
    r_cols = mu_ref.shape[1]

    @pl.when(ti == 0)
    def _():
        carry_s[...] = jnp.zeros_like(carry_s)
        st_s[...] = jnp.zeros_like(st_s)

    mask = hmask_ref[...]
    mask_bf = hmaskb_ref[...]
    strict, incl, diag = tmask_ref[0], tmask_ref[1], tmask_ref[2]
    tri_bf = segm_ref[...]

    def segsum(x):
        return jnp.concatenate([_dot(x[:, g * gw:(g + 1) * gw], mask_bf) for g in range(n_groups)], axis=1)

    p = p_ref[0] if nb == 1 else jnp.concatenate([p_ref[s] for s in range(nb)], axis=0)
    trow = _mod_pow2(lax.broadcasted_iota(jnp.int32, (rows, 1), 0), tr)
    if nb == 1:
        first = carry_s[0]
    else:
        first = jnp.concatenate([jnp.broadcast_to(carry_s[s], (tr, r_cols)) for s in range(nb)], axis=0)
    for s in range(nb):
        carry_s[s] = p_ref[s, tr - 1:tr, :r_cols]
    f = _token_features(p, first, trow, (mu_ref, w0_ref, a0_ref, wwa_ref, wg_ref, kk_ref, ka_ref), segsum, rw)
    sr = tri_bf.shape[0]
    cum = jnp.concatenate([_dot_sel_lhs(tri_bf, f["lw"][i * sr:(i + 1) * sr]) for i in range(rows // sr)],
                          axis=0)
    tot = jnp.concatenate([jnp.broadcast_to(cum[j * c + c - 1:j * c + c], (c, rw)) for j in range(rows // c)],
                          axis=0)
    d = _decay_factors(f, cum, tot)

    n_chunks = rows // c
    chunks_per_seq = tr // c

    def blk(x, j, g):
        return x[j * c:(j + 1) * c, g * gw:(g + 1) * gw]

    probs = [(j, g) for j in range(n_chunks) for g in range(n_groups)]
    cut = lambda x, ps: [blk(x, j, g) for j, g in ps]
    local = dict(zip(probs, _wkv_local(cut(d["rt"], probs), cut(d["at"], probs), cut(d["bt"], probs),
                                       cut(d["kt"], probs), cut(d["v"], probs), mask_bf, strict, incl, diag)))
    y_c, _ = _gmlp_branch(p, r_cols, lng_ref, lnb_ref, ws2_ref, tril2_ref[...], bsf_ref)
    y_blk = {}
    for jj in range(chunks_per_seq):
        ps = [(s * chunks_per_seq + jj, g) for s in range(nb) for g in range(n_groups)]
        loc = [local[pr_] for pr_ in ps]
        sts = [st_s[j // chunks_per_seq, g] for j, g in ps]
        pcs = [blk(d["pinc"], j, g)[c - 1:c] for j, g in ps]
        u, ys, st_new = _wkv_state_step([l[0] for l in loc], cut(d["rt"], ps), [l[1] for l in loc],
                                        cut(d["v"], ps), cut(d["bdec"], ps), cut(d["kdec"], ps), pcs, sts, mask)
        for i, (j, g) in enumerate(ps):
            st_s[j // chunks_per_seq, g] = st_new[i]
            y_blk[j, g] = ys[i] + _dot(loc[i][2], _block_diag(u[i], mask_bf)) + loc[i][3]
    y_rows = [jnp.concatenate([y_blk[j, g] for g in range(n_groups)], axis=1) for j in range(n_chunks)]
    yw = y_rows[0] if n_chunks == 1 else jnp.concatenate(y_rows, axis=0)

    y_r = _rwkv_output(yw, f, rk_ref, gnw_ref, gnb_ref, segsum, head_dim)
    y_all = jnp.concatenate([y_r, y_c], axis=1)
    for s in range(nb):
        y_ref[s] = y_all[s * tr:(s + 1) * tr]

    @pl.when(ti == n_t - 1)
    def _():
        shift_ref[...] = carry_s[...]
        for s in range(nb):
            for g in range(n_groups):
                _state_to_heads(wkv_ref, s, g, st_s[s, g], head_dim)


def _mixer_short_kernel(p_ref, shift0_ref, wkv0_ref, mu_ref, w0_ref, a0_ref, wwa_ref, wg_ref, kk_ref, ka_ref,
                        rk_ref, gnw_ref, gnb_ref, lng_ref, lnb_ref, ws2_ref, bsf_ref, hmask_ref, hmaskb_ref,
                        tmask_ref, segm_ref, tril2_ref, y_ref, shift_ref, wkv_ref, vn_ref, *, n_heads, head_dim):
    nb, ts, _ = p_ref.shape
    c = WKV_CHUNK
    assert nb * ts == c
    gw = HEAD_GROUP * head_dim
    n_groups = n_heads // HEAD_GROUP
    rw = n_heads * head_dim
    r_cols = mu_ref.shape[1]

    mask = hmask_ref[...]
    mask_bf = hmaskb_ref[...]
    strict, incl = tmask_ref[0], tmask_ref[1]

    def segsum(x):
        return jnp.concatenate([_dot(x[:, g * gw:(g + 1) * gw], mask_bf) for g in range(n_groups)], axis=1)

    p = p_ref[...].reshape(c, p_ref.shape[2])
    trow = _mod_pow2(lax.broadcasted_iota(jnp.int32, (c, 1), 0), ts)
    first = jnp.concatenate([jnp.broadcast_to(shift0_ref[s], (ts, r_cols)) for s in range(nb)], axis=0)
    for s in range(nb):
        shift_ref[s] = p_ref[s, ts - 1:ts, :r_cols]
    f = _token_features(p, first, trow, (mu_ref, w0_ref, a0_ref, wwa_ref, wg_ref, kk_ref, ka_ref), segsum, rw)
    cum = _dot_sel_lhs(segm_ref[...], f["lw"])
    tot = jnp.concatenate([jnp.broadcast_to(cum[s * ts + ts - 1:(s + 1) * ts], (ts, rw)) for s in range(nb)],
                          axis=0)
    d = _decay_factors(f, cum, tot)

    groups = range(n_groups)
    gcut = lambda x: [x[:, g * gw:(g + 1) * gw] for g in groups]
    local = _wkv_local(gcut(d["rt"]), gcut(d["at"]), gcut(d["bt"]), gcut(d["kt"]), gcut(d["v"]), mask_bf,
                       strict, incl, ts)

    ps = [(s, g) for s in range(nb) for g in groups]
    scut = lambda x: [x[s * ts:(s + 1) * ts, g * gw:(g + 1) * gw] for s, g in ps]
    lcut = lambda idx: [local[g][idx][s * ts:(s + 1) * ts] for s, g in ps]
    sts = [_state_from_heads(wkv0_ref, s, g, head_dim) for s, g in ps]
    pcs = [d["pinc"][s * ts + ts - 1:(s + 1) * ts, g * gw:(g + 1) * gw] for s, g in ps]
    u, ys, st_new = _wkv_state_step(lcut(0), scut(d["rt"]), lcut(1), scut(d["v"]), scut(d["bdec"]),
                                    scut(d["kdec"]), pcs, sts, mask)
    for i, (s, g) in enumerate(ps):
        _state_to_heads(wkv_ref, s, g, st_new[i], head_dim)
    rows_of = lambda xs, g: jnp.concatenate([xs[i] for i, (s, gg) in enumerate(ps) if gg == g], axis=0)
    yw = jnp.concatenate([rows_of(ys, g) + _dot(local[g][2], _block_diag(rows_of(u, g), mask_bf)) + local[g][3]
                          for g in groups], axis=1)

    y_r = _rwkv_output(yw, f, rk_ref, gnw_ref, gnb_ref, segsum, head_dim)
    y_c, vn = _gmlp_branch(p, r_cols, lng_ref, lnb_ref, ws2_ref, tril2_ref[...], bsf_ref)
    y_ref[...] = jnp.concatenate([y_r, y_c], axis=1).reshape(nb, ts, -1)
    vn_ref[...] = vn.reshape(nb, ts, -1)


def _mixer_masks(rows, seq, ct, head_dim):
    c, gw = WKV_CHUNK, HEAD_GROUP * head_dim
    blk = np.arange(gw) // head_dim
    hmask = (blk[:, None] == blk[None, :]).astype(np.float32)
    t = np.arange(c)[:, None]
    s = np.arange(HEAD_GROUP * c)[None, :] % c
    strict = (s < t) & (s // seq == t // seq)
    incl = (s <= t) & (s // seq == t // seq)
    tmask = np.stack([strict, incl, strict & (s // INV_BLOCK == t // INV_BLOCK)]).astype(np.float32)
    r = np.arange(rows)
    seg = min(c, seq)
    same = (r[:, None] // seg) == (r[None, :] // seg)
    segm = (same & (r[None, :] <= r[:, None])).astype(np.float32)
    ctm = np.arange(ct)
    tril1 = ((ctm[None, :] <= ctm[:, None]) & (ctm[None, :] // seq == ctm[:, None] // seq))
    tril2 = np.concatenate([tril1, tril1], axis=1).astype(np.float32)
    return [jnp.asarray(hmask), jnp.asarray(hmask, _BF16), jnp.asarray(tmask), jnp.asarray(segm, _BF16),
            jnp.asarray(tril2)]


def _gmlp_weights(w_s, b_s, ct, seq, gwid):
    hc = w_s.shape[0]
    n_seq = ct // min(seq, ct)
    ws = w_s[:, :ct // n_seq, :ct // n_seq]
    bs = b_s[:, :ct // n_seq]
    if n_seq > 1:
        ws = jnp.einsum("st,hij->hsitj", jnp.eye(n_seq, dtype=ws.dtype), ws).reshape(hc, ct, ct)
        bs = jnp.tile(bs, (1, n_seq))
    ws2 = ws.reshape(hc // 2, 2, ct, ct).transpose(0, 2, 1, 3).reshape(hc // 2, ct, 2 * ct)
    return ws2, jnp.repeat(bs.T, gwid // hc, axis=1)


def _mixer_params(prm, rows, seq, ct, head_dim):
    gwid = prm["lng"].shape[1]
    ws2, bsf = _gmlp_weights(prm["w_s"], prm["b_s"], ct, seq, gwid)
    params = [prm["mu"], prm["w0"], prm["a0"], prm["wwa"], prm["wg"], prm["k_k"], prm["k_a"], prm["r_k"],
              prm["gn_w"], prm["gn_b"], prm["lng"], prm["lnb"], ws2, bsf]
    return params + _mixer_masks(rows, seq, ct, head_dim)


def _full_spec(a):
    nd = a.ndim
    return pl.BlockSpec(a.shape, lambda b, i: (0,) * nd)


def _mixer_call(p, prm, *, nb, tr, n_heads, head_dim):
    nseq, t, pcols = p.shape
    rw = n_heads * head_dim
    r_cols = prm["mu"].shape[1]
    gwid = prm["lng"].shape[1]
    gw = HEAD_GROUP * head_dim
    n_groups = n_heads // HEAD_GROUP
    rows = nb * tr
    assert nseq % nb == 0 and t % tr == 0 and tr % GMLP_CHUNK == 0 and rows % SEG_ROWS == 0
    params = _mixer_params(prm, SEG_ROWS, t, GMLP_CHUNK, head_dim)
    return pl.pallas_call(
        functools.partial(_mixer_kernel, n_heads=n_heads, head_dim=head_dim),
        out_shape=(jax.ShapeDtypeStruct((nseq, t, rw + gwid), _F32),
                   jax.ShapeDtypeStruct((nseq, 1, r_cols), _F32),
                   jax.ShapeDtypeStruct((nseq, n_heads, head_dim, head_dim), _F32)),
        grid=(nseq // nb, t // tr),
        in_specs=[pl.BlockSpec((nb, tr, pcols), lambda b, i: (b, i, 0))] + [_full_spec(a) for a in params],
        out_specs=(pl.BlockSpec((nb, tr, rw + gwid), lambda b, i: (b, i, 0)),
                   pl.BlockSpec((nb, 1, r_cols), lambda b, i: (b, 0, 0)),
                   pl.BlockSpec((nb, n_heads, head_dim, head_dim), lambda b, i: (b, 0, 0, 0))),
        scratch_shapes=[pltpu.VMEM((nb, n_groups, gw, gw), _F32), pltpu.VMEM((nb, 1, r_cols), _F32)],
        compiler_params=pltpu.CompilerParams(dimension_semantics=("arbitrary", "arbitrary"),
                                             vmem_limit_bytes=_VMEM_LIMIT),
        name="mixers_fresh",
    )(p, *params)


def _mixer_short_call(p, shift0, wkv0, prm, *, n_heads, head_dim):
    nseq, ts, pcols = p.shape
    nb = WKV_CHUNK // ts
    assert nb * ts == WKV_CHUNK and ts % SUBLANES == 0 and nseq % nb == 0
    rw = n_heads * head_dim
    r_cols = prm["mu"].shape[1]
    gwid = prm["lng"].shape[1]
    params = _mixer_params(prm, WKV_CHUNK, ts, WKV_CHUNK, head_dim)
    seqs = lambda *tail: pl.BlockSpec((nb,) + tail, lambda b, i: (b,) + (0,) * len(tail))
    return pl.pallas_call(
        functools.partial(_mixer_short_kernel, n_heads=n_heads, head_dim=head_dim),
        out_shape=(jax.ShapeDtypeStruct((nseq, ts, rw + gwid), _F32),
                   jax.ShapeDtypeStruct((nseq, 1, r_cols), _F32),
                   jax.ShapeDtypeStruct((nseq, n_heads, head_dim, head_dim), _F32),
                   jax.ShapeDtypeStruct((nseq, ts, gwid), _F32)),
        grid=(nseq // nb, 1),
        in_specs=[seqs(ts, pcols), seqs(1, r_cols), seqs(n_heads, head_dim, head_dim)]
                 + [_full_spec(a) for a in params],
        out_specs=(seqs(ts, rw + gwid), seqs(1, r_cols), seqs(n_heads, head_dim, head_dim), seqs(ts, gwid)),
        compiler_params=pltpu.CompilerParams(dimension_semantics=("arbitrary", "arbitrary"),
                                             vmem_limit_bytes=_VMEM_LIMIT),
        name="mixers_state",
    )(p, shift0, wkv0, *params)


def _pick_tile(n, pref):
    t = min(n, pref)
    while n % t:
        t //= 2
    return t


def kernel(x_prompt, x_sample, state_shift, state_wkv, c_prompt, c_sample, w_ada, b_ada, ffn1_gu, ffn1_dn, w_in, mu_shift, w0, w_lora_up, a0, a_lora_up, g_lora_up, k_k, k_a, r_k, gn_w, gn_b, ln_v_g, ln_v_b, w_s, b_s, w_out, ffn2_gu, ffn2_dn, final_g):
    depth = w_ada.shape[0]
    bp, tp, d = x_prompt.shape
    bs, ts, _ = x_sample.shape
    n_heads, head_dim = r_k.shape[1], r_k.shape[2]
    rw = n_heads * head_dim
    w_rank, a_rank = w_lora_up.shape[1], a_lora_up.shape[1]
    assert w_rank == a_rank and n_heads % HEAD_GROUP == 0 and head_dim == WKV_CHUNK
    d_ff = ffn1_dn.shape[1]
    tf = 256 if d_ff % 256 == 0 else 128

    xp = x_prompt.reshape(bp * tp, d)
    xs = x_sample.reshape(bs * ts, d)
    assert bs % SUBLANES == 0
    n_c = bs + bp
    n_c_pad = -(-n_c // 16) * 16
    c_all = jnp.concatenate([c_sample, c_prompt, jnp.zeros((n_c_pad - n_c, d), _F32)], axis=0)

    tm_p = _pick_tile(tp, 512)
    tr_p = _pick_tile(tp, 128)
    tm_s = ts * _pick_tile(bs, 64)
    nb_p = _pick_tile(bp, 4)

    shp, wkp, shs, wks, cvs = [], [], [], [], []
    for l in range(depth):
        mod = _mod_call(c_all, w_ada[l], b_ada[l])
        grp_p = dict(tm=tm_p, seq0=bs, seqs_per_tile=1, tiles_per_seq=tp // tm_p)
        grp_s = dict(tm=tm_s, seq0=0, seqs_per_tile=tm_s // ts, tiles_per_seq=1)

        zw = jnp.zeros((w_rank, rw), _F32)
        wwa = jnp.concatenate([jnp.concatenate([w_lora_up[l], zw], axis=1),
                               jnp.concatenate([zw, a_lora_up[l]], axis=1)], axis=0).astype(_BF16)
        row = lambda a: a.reshape(1, -1)
        prm = dict(mu=row(mu_shift[l]), w0=row(w0[l]), a0=row(a0[l]), wwa=wwa,
                   wg=g_lora_up[l].astype(_BF16), k_k=row(k_k[l]), k_a=row(k_a[l]), r_k=row(r_k[l]),
                   gn_w=row(gn_w[l]), gn_b=row(gn_b[l]), lng=row(ln_v_g[l]), lnb=row(ln_v_b[l]),
                   w_s=w_s[l], b_s=b_s[l])
        fg = final_g.reshape(1, d)
        final = l == depth - 1

        ((hp, pp),), w1_bf = _ffn_in_call([dict(grp_p, acts=[xp])], mod, ffn1_gu[l], ffn1_dn[l], w_in[l],
                                          tf=tf, stage=WEIGHT_STAGE_STEPS, export=True)
        ((hs, ps),), _ = _ffn_in_call([dict(grp_s, acts=[xs])], mod, *w1_bf, tf=tf, stage=0)
        ymp, sh_p, wk_p = _mixer_call(pp.reshape(bp, tp, -1), prm, nb=nb_p, tr=tr_p,
                                      n_heads=n_heads, head_dim=head_dim)
        yms, sh_s, wk_s, v_s = _mixer_short_call(ps.reshape(bs, ts, -1), state_shift[l][:, None, :], state_wkv[l], prm,
                                                 n_heads=n_heads, head_dim=head_dim)
        half_s = dict(grp_s, tm=tm_s // 2, seqs_per_tile=tm_s // ts // 2)
        (xp,), (xs,) = _out_ffn_call([dict(grp_p, acts=[ymp.reshape(bp * tp, -1), hp]),
                                      dict(half_s, acts=[yms.reshape(bs * ts, -1), hs])], mod,
                                     w_out[l], ffn2_gu[l], ffn2_dn[l], fg, tf=tf, final=final,
                                     stage=WEIGHT_STAGE_STEPS)

        shp.append(sh_p[:, 0]); wkp.append(wk_p); shs.append(sh_s[:, 0]); wks.append(wk_s); cvs.append(v_s)

    y_prompt = xp.reshape(bp, tp, d)
    y_sample = xs.reshape(bs, ts, d)
    return (y_prompt, y_sample, jnp.stack(shp), jnp.stack(wkp), jnp.stack(shs), jnp.stack(wks), jnp.stack(cvs))
```

```python
import functools
import math

import jax
import jax.numpy as jnp
import numpy as np
from jax import lax
from jax.experimental import pallas as pl
from jax.experimental.pallas import tpu as pltpu

_F32 = jnp.float32
_BF16 = jnp.bfloat16

RMS_EPS = 1e-6
LN_EPS = 1e-5
GN_EPS = 64e-5
N_MOD = 9
WKV_CHUNK = 64
HEAD_GROUP = 4
INV_BLOCK = 16
GMLP_CHUNK = 128
SEG_ROWS = 256
SUBLANES = 8
MOD_PER_STEP = 3
WEIGHT_STAGE_STEPS = 16
DECAY_SCALE = math.exp(-0.5)

_V7X_VMEM_BYTES = 64 * 1024 * 1024
_VMEM_LIMIT = _V7X_VMEM_BYTES - 8 * 1024 * 1024


def _dot(a, b):
    return jnp.dot(a.astype(_BF16), b.astype(_BF16), preferred_element_type=_F32)


def _dot_nt(a, b):
    return lax.dot_general(a.astype(_BF16), b.astype(_BF16), (((1,), (1,)), ((), ())),
                           preferred_element_type=_F32)


def _dot_tn(a, b):
    return lax.dot_general(a.astype(_BF16), b.astype(_BF16), (((0,), (0,)), ((), ())),
                           preferred_element_type=_F32)


def _split_hi_lo(x):
    hi = x.astype(_BF16)
    lo = (x - hi.astype(_F32)).astype(_BF16)
    return hi, lo


def _dot_sel_lhs(sel, x):
    hi, lo = _split_hi_lo(x)
    return (jnp.dot(sel, hi, preferred_element_type=_F32)
            + jnp.dot(sel, lo, preferred_element_type=_F32))


def _mod_pow2(x, n):
    assert n & (n - 1) == 0
    return lax.bitwise_and(x, jnp.int32(n - 1))


def _rms(x):
    return x * lax.rsqrt(jnp.mean(x * x, -1, keepdims=True) + RMS_EPS)


def _sigmoid(x):
    return 1.0 / (1.0 + jnp.exp(-x))


def _affine(x, scale, shift=None):
    mb = scale.shape[0]
    if mb == 1:
        y = x * scale
        return y if shift is None else y + shift
    tm, d = x.shape
    y = x.reshape(mb, tm // mb, d) * scale[:, None, :]
    if shift is not None:
        y = y + shift[:, None, :]
    return y.reshape(tm, d)


def _swiglu(n_bf, wgu_ref, wdn_ref, tf):
    d_ff = wdn_ref.shape[0]
    acc = None
    for j in range(d_ff // tf):
        g = jnp.dot(n_bf, wgu_ref[:, j * tf:(j + 1) * tf], preferred_element_type=_F32)
        u = jnp.dot(n_bf, wgu_ref[:, d_ff + j * tf:d_ff + (j + 1) * tf], preferred_element_type=_F32)
        hm = (g * _sigmoid(g) * u).astype(_BF16)
        part = jnp.dot(hm, wdn_ref[j * tf:(j + 1) * tf, :], preferred_element_type=_F32)
        acc = part if acc is None else acc + part
    return acc


def _mod_kernel(c_ref, w_ref, b_ref, o_ref):
    c = c_ref[...]
    s = c * _sigmoid(c)
    o = _dot(s, w_ref[...]) + b_ref[...]
    d = c.shape[1]
    for k in range(o_ref.shape[0]):
        o_ref[k] = o[:, k * d:(k + 1) * d]


def _mod_call(c_all, w_ada, b_ada):
    n, d = c_all.shape
    per_step = MOD_PER_STEP if N_MOD % MOD_PER_STEP == 0 else 1
    return pl.pallas_call(
        _mod_kernel,
        out_shape=jax.ShapeDtypeStruct((N_MOD, n, d), _F32),
        grid=(N_MOD // per_step,),
        in_specs=[pl.BlockSpec((n, d), lambda j: (0, 0)),
                  pl.BlockSpec((d, per_step * d), lambda j: (0, j)),
                  pl.BlockSpec((1, per_step * d), lambda j: (0, j))],
        out_specs=pl.BlockSpec((per_step, n, d), lambda j: (j, 0, 0)),
        compiler_params=pltpu.CompilerParams(dimension_semantics=("arbitrary",),
                                             vmem_limit_bytes=_VMEM_LIMIT),
        name="adaln_mod",
    )(c_all, w_ada, b_ada.reshape(1, N_MOD * d))


def _mod_rows(mod_ref, pick, tile):
    if pick is None:
        return lambda j: mod_ref[j]
    seq0, tiles_per_seq = pick
    row = lax.rem(seq0 + lax.div(tile, jnp.int32(tiles_per_seq)), jnp.int32(SUBLANES))
    return lambda j: mod_ref[j, pl.ds(row, 1), :]


def _mod_spec(mod, seq0, seqs_per_tile, tiles_per_seq, tile):
    n_mod, _, d = mod.shape
    if seqs_per_tile > 1:
        assert seqs_per_tile % SUBLANES == 0 and seq0 % seqs_per_tile == 0 and tiles_per_seq == 1
        first = seq0 // seqs_per_tile
        return pl.BlockSpec((n_mod, seqs_per_tile, d), lambda i: (0, first + tile(i), 0)), None
    return (pl.BlockSpec((n_mod, SUBLANES, d), lambda i: (0, (seq0 + tile(i) // tiles_per_seq) // SUBLANES, 0)),
            (seq0, tiles_per_seq))


def _resident(shape):
    nd = len(shape)
    return pl.BlockSpec(shape, lambda *_: (0,) * nd, pipeline_mode=pl.Buffered(1))


def _dense_call(body, name, groups, mod, weights, extra, out_widths, *, stage, export=False):
    starts, n_steps = [], stage
    for g in groups:
        starts.append(n_steps)
        n_steps += g["acts"][0].shape[0] // g["tm"]
    n_w, n_x, n_out = len(weights), len(extra), len(out_widths)

    def tile_of(k):
        last = groups[k]["acts"][0].shape[0] // groups[k]["tm"] - 1
        return lambda i: jnp.clip(i - starts[k], 0, last)

    in_specs, picks, inputs = [], [], []
    for k, g in enumerate(groups):
        tile = tile_of(k)
        mod_spec, pick = _mod_spec(mod, g["seq0"], g["seqs_per_tile"], g["tiles_per_seq"], tile)
        picks.append(pick)
        in_specs += [pl.BlockSpec((g["tm"], a.shape[1]), lambda i, tile=tile: (tile(i), 0)) for a in g["acts"]]
        in_specs.append(mod_spec)
        inputs += list(g["acts"]) + [mod]
    out_shape, out_specs = [], []
    for k, g in enumerate(groups):
        tile = tile_of(k)
        out_shape += [jax.ShapeDtypeStruct((g["acts"][0].shape[0], w), _F32) for w in out_widths]
        out_specs += [pl.BlockSpec((g["tm"], w), lambda i, tile=tile: (tile(i), 0)) for w in out_widths]
    scratch = []
    if stage:
        assert all(w.shape[0] % (stage * 2 * SUBLANES) == 0 for w in weights)
        chunk_spec = lambda w: pl.BlockSpec((w.shape[0] // stage, w.shape[1]),
                                            lambda i: (jnp.minimum(i, stage - 1), 0))
        in_specs += [chunk_spec(w) for w in weights]
        scratch = [pltpu.VMEM(w.shape, _BF16) for w in weights]
        if export:
            out_shape += [jax.ShapeDtypeStruct(w.shape, _BF16) for w in weights]
            out_specs += [chunk_spec(w) for w in weights]
    else:
        in_specs += [_resident(w.shape) for w in weights]
    in_specs += [_resident(e.shape) for e in extra]

    def kernel(*refs):
        it = iter(refs)
        act_refs, mod_refs = [], []
        for g in groups:
            act_refs.append([next(it) for _ in g["acts"]])
            mod_refs.append(next(it))
        w_in_refs = [next(it) for _ in range(n_w)]
        x_refs = [next(it) for _ in range(n_x)]
        out_refs = [[next(it) for _ in range(n_out)] for _ in groups]
        w_out_refs = [next(it) for _ in range(n_w)] if stage and export else [None] * n_w
        w_refs = [next(it) for _ in range(n_w)] if stage else w_in_refs
        step = pl.program_id(0)

        if stage:
            @pl.when(step < stage)
            def _():
                for src, dst, out in zip(w_in_refs, w_refs, w_out_refs):
                    chunk = src.shape[0]
                    v = src[...].astype(_BF16)
                    dst[pl.ds(pl.multiple_of(step * chunk, chunk), chunk), :] = v
                    if out is not None:
                        out[...] = v

        for k in range(len(groups)):
            end = starts[k + 1] if k + 1 < len(groups) else n_steps

            def run(k=k):
                body(act_refs[k], _mod_rows(mod_refs[k], picks[k], step - starts[k]), w_refs, x_refs, out_refs[k])

            if len(groups) == 1 and not stage:
                run()
            else:
                pl.when((step >= starts[k]) & (step < end))(run)

    outs = pl.pallas_call(
        kernel,
        out_shape=tuple(out_shape),
        grid=(n_steps,),
        in_specs=in_specs,
        out_specs=tuple(out_specs),
        scratch_shapes=scratch,
        compiler_params=pltpu.CompilerParams(dimension_semantics=("arbitrary",),
                                             vmem_limit_bytes=_VMEM_LIMIT),
        name=name,
    )(*inputs, *weights, *extra)
    per_group = [outs[k * n_out:(k + 1) * n_out] for k in range(len(groups))]
    return per_group, list(outs[len(groups) * n_out:])


def _ffn_in_body(acts, mod, weights, extra, outs, *, tf):
    (x_ref,), (wgu_ref, wdn_ref, win_ref), (h_ref, p_ref) = acts, weights, outs
    x = x_ref[...]
    sh1, sc1, g1, sh2, sc2 = (mod(j) for j in range(5))
    n1 = _affine(_rms(x), 1.0 + sc1, sh1).astype(_BF16)
    h = x + _affine(_swiglu(n1, wgu_ref, wdn_ref, tf), 0.5 * g1)
    h_ref[...] = h
    n2 = _affine(_rms(h), 1.0 + sc2, sh2).astype(_BF16)
    p_ref[...] = jnp.dot(n2, win_ref[...], preferred_element_type=_F32)


def _ffn_in_call(groups, mod, wgu, wdn, win, *, tf, stage, export=False):
    d = groups[0]["acts"][0].shape[1]
    return _dense_call(functools.partial(_ffn_in_body, tf=tf), "ffn1_inproj", groups, mod, [wgu, wdn, win], [],
                       [d, win.shape[1]], stage=stage, export=export)


def _out_ffn_body(acts, mod, weights, extra, outs, *, tf, final):
    (ym_ref, h_ref), (wout_ref, wgu_ref, wdn_ref), (fg_ref,), (o_ref,) = acts, weights, extra, outs
    g2, sh3, sc3, g3 = (mod(j) for j in range(5, 9))
    h = h_ref[...] + _affine(jnp.dot(ym_ref[...].astype(_BF16), wout_ref[...],
                                     preferred_element_type=_F32), g2)
    n3 = _affine(_rms(h), 1.0 + sc3, sh3).astype(_BF16)
    h = h + _affine(_swiglu(n3, wgu_ref, wdn_ref, tf), 0.5 * g3)
    if final:
        h = _rms(h) * fg_ref[...]
    o_ref[...] = h


def _out_ffn_call(groups, mod, wout, wgu, wdn, fg, *, tf, final, stage):
    d = wdn.shape[1]
    return _dense_call(functools.partial(_out_ffn_body, tf=tf, final=final), "outproj_ffn2", groups, mod,
                       [wout, wgu, wdn], [fg], [d], stage=stage)[0]


def _rows_bf16(*xs):
    return jnp.concatenate([x.astype(_BF16) for x in xs], axis=0)


def _block_diag(y, mask_bf):
    reps = mask_bf.shape[0] // y.shape[0]
    return jnp.concatenate([y.astype(_BF16)] * reps, axis=0) * mask_bf


def _nilpotent_inverse(n, bd, order):
    every = range(len(n))
    c = n[0].shape[0]
    tm = n
    levels = order.bit_length() - 2
    if levels <= 0:
        return tm
    nb = bd(n)
    npow = [_dot(n[i], nb[i]) for i in every]
    for lvl in range(levels):
        nb = bd(npow)
        if lvl + 1 < levels:
            res = [_dot(jnp.concatenate([tm[i], npow[i]], axis=0), nb[i]) for i in every]
            tm = [tm[i] + npow[i] + res[i][:c] for i in every]
            npow = [res[i][c:] for i in every]
        else:
            tm = [tm[i] + npow[i] + _dot(tm[i], nb[i]) for i in every]
    return tm


def _wkv_local(rt, at, bt, kt, vv, mask_bf, strict, incl, diag):
    n = len(rt)
    c = rt[0].shape[0]
    every = range(n)
    bd = lambda xs: [_block_diag(x, mask_bf) for x in xs]
    x = [_rows_bf16(at[i], rt[i]) for i in every]
    bb, kb = bd(bt), bd(kt)
    xb = [_dot_nt(x[i], bb[i]) for i in every]
    xk = [_dot_nt(x[i], kb[i]) for i in every]
    lab = [xb[i][:c] * strict for i in every]
    mrb = [xb[i][c:] * incl for i in every]
    lak = [xk[i][:c] * strict for i in every]
    mrk = [xk[i][c:] * incl for i in every]
    vb = bd(vv)
    wy = [_dot(jnp.concatenate([lak[i], mrk[i]], axis=0), vb[i]) for i in every]
    w1 = [wy[i][:c] for i in every]
    y0 = [wy[i][c:] for i in every]
    if isinstance(diag, int):
        tm = _nilpotent_inverse(lab, bd, diag)
    else:
        assert c == 4 * INV_BLOCK
        nd = [lab[i] * diag for i in every]
        noff = [lab[i] - nd[i] for i in every]
        tdm = _nilpotent_inverse(nd, bd, INV_BLOCK)
        noffb = bd(noff)
        m1 = [noff[i] + _dot(tdm[i], noffb[i]) for i in every]
        m1b = bd(m1)
        m2 = [_dot(m1[i], m1b[i]) for i in every]
        m2b = bd(m2)
        q = [m1[i] + m2[i] + _dot(m1[i], m2b[i]) for i in every]
        tdmb = bd(tdm)
        tm = [q[i] + tdm[i] + _dot(q[i], tdmb[i]) for i in every]
    atb = bd(at)
    ap = [at[i] + _dot(tm[i], atb[i]) for i in every]
    w1b = bd(w1)
    u0 = [w1[i] + _dot(tm[i], w1b[i]) for i in every]
    return [(ap[i], u0[i], mrb[i], y0[i]) for i in every]


def _wkv_state_step(ap, rt, u0, vv, bdec, kdec, pc, st, mask):
    every = range(len(rt))
    c = rt[0].shape[0]
    ar = [_dot_nt(_rows_bf16(ap[i], rt[i]), st[i]) for i in every]
    u = [ar[i][:c] + u0[i] for i in every]
    upd = [_dot_tn(_rows_bf16(u[i], vv[i]), _rows_bf16(bdec[i], kdec[i])) for i in every]
    st_new = [st[i] * pc[i] + upd[i] * mask for i in every]
    return u, [ar[i][c:] for i in every], st_new


def _token_features(p, first, trow, prm, segsum, rw):
    mu_ref, w0_ref, a0_ref, wwa_ref, wg_ref, kk_ref, ka_ref = prm
    r_cols = mu_ref.shape[1]
    pr = p[:, :r_cols]
    prev = jnp.where(trow == 0, first, pltpu.roll(pr, 1, 0))
    xm = pr + (prev - pr) * mu_ref[...]
    r = xm[:, 0:rw]
    k = xm[:, rw:2 * rw]
    v = xm[:, 2 * rw:3 * rw]
    lo_w = wwa_ref.shape[0]
    x_lo = xm[:, 3 * rw:3 * rw + lo_w]
    gd = xm[:, 3 * rw + lo_w:r_cols]
    lane = lax.broadcasted_iota(jnp.int32, (1, lo_w), 1)
    x_lo = jnp.where(lane < lo_w // 2, jnp.tanh(x_lo), x_lo)
    wa = _dot(x_lo, wwa_ref[...])
    wpre = w0_ref[...] + wa[:, :rw]
    lw = -DECAY_SCALE * _sigmoid(wpre)
    a = _sigmoid(a0_ref[...] + wa[:, rw:])
    gate = _dot(_sigmoid(gd), wg_ref[...])
    kkv = k * kk_ref[...]
    kkn = kkv * lax.rsqrt(jnp.maximum(segsum(kkv * kkv), 1e-24))
    k2 = k * (1.0 + (a - 1.0) * ka_ref[...])
    return dict(r=r, v=v, lw=lw, a=a, gate=gate, kkn=kkn, k2=k2)


def _decay_factors(f, cum, tot):
    pinc = jnp.exp(cum)
    pinv = jnp.exp(-cum)
    pend = jnp.exp(tot - cum)
    beta = f["kkn"] * f["a"]
    ops = dict(rt=f["r"] * pinc, at=-f["kkn"] * jnp.exp(cum - f["lw"]), bt=beta * pinv, kt=f["k2"] * pinv,
               bdec=beta * pend, kdec=f["k2"] * pend, v=f["v"])
    return dict({name: x.astype(_BF16) for name, x in ops.items()}, pinc=pinc)


def _rwkv_output(yw, f, rk_ref, gnw_ref, gnb_ref, segsum, head_dim):
    inv_n = 1.0 / head_dim
    mean = segsum(yw) * inv_n
    dev = yw - mean
    var = segsum(dev * dev) * inv_n
    yn = dev * lax.rsqrt(var + GN_EPS) * gnw_ref[...] + gnb_ref[...]
    bonus = segsum(f["r"] * f["k2"] * rk_ref[...]) * f["v"]
    return (yn + bonus) * f["gate"]


def _gmlp_branch(p, r_cols, lng_ref, lnb_ref, ws2_ref, tril2, bsf_ref):
    gwid = lng_ref.shape[1]
    ct = ws2_ref.shape[1]
    rows = p.shape[0]
    pu = p[:, r_cols:r_cols + gwid]
    pv = p[:, r_cols + gwid:r_cols + 2 * gwid]
    pm = jnp.mean(pv, -1, keepdims=True)
    pd = pv - pm
    pvar = jnp.mean(pd * pd, -1, keepdims=True)
    vn = pd * lax.rsqrt(pvar + LN_EPS) * lng_ref[...] + lnb_ref[...]
    n_pair = ws2_ref.shape[0]
    pw = gwid // n_pair
    first_half = lax.broadcasted_iota(jnp.int32, (1, pw), 1) < pw // 2
    ws_tril = [(ws2_ref[j] * tril2).astype(_BF16) for j in range(n_pair)]
    mixed_rows = []
    for q in range(rows // ct):
        outs = []
        for j in range(n_pair):
            vp = vn[q * ct:(q + 1) * ct, j * pw:(j + 1) * pw]
            rhs = jnp.concatenate([jnp.where(first_half, vp, 0.0), jnp.where(first_half, 0.0, vp)], axis=0)
            outs.append(_dot(ws_tril[j], rhs))
        mixed_rows.append(jnp.concatenate(outs, axis=1) + bsf_ref[...])
    mixed = mixed_rows[0] if len(mixed_rows) == 1 else jnp.concatenate(mixed_rows, axis=0)
    return pu * mixed, vn


def _state_from_heads(wkv_ref, s, g, head_dim):
    zero = jnp.zeros((head_dim, head_dim), _F32)
    blocks = []
    for h in range(HEAD_GROUP):
        blk = wkv_ref[s, g * HEAD_GROUP + h]
        blocks.append(jnp.concatenate([zero] * h + [blk] + [zero] * (HEAD_GROUP - 1 - h), axis=1))
    return jnp.concatenate(blocks, axis=0)


def _state_to_heads(wkv_ref, s, g, st, head_dim):
    for h in range(HEAD_GROUP):
        wkv_ref[s, g * HEAD_GROUP + h] = st[h * head_dim:(h + 1) * head_dim, h * head_dim:(h + 1) * head_dim]


def _mixer_kernel(p_ref, mu_ref, w0_ref, a0_ref, wwa_ref, wg_ref, kk_ref, ka_ref, rk_ref, gnw_ref, gnb_ref,
                  lng_ref, lnb_ref, ws2_ref, bsf_ref, hmask_ref, hmaskb_ref, tmask_ref, segm_ref, tril2_ref,
                  y_ref, shift_ref, wkv_ref, st_s, carry_s, *, n_heads, head_dim):
    nb, tr, _ = p_ref.shape
    c = WKV_CHUNK
    gw = HEAD_GROUP * head_dim
    n_groups = n_heads // HEAD_GROUP
    rw = n_heads * head_dim
    rows = nb * tr
    ti = pl.program_id(1)
    n_t = pl.num_programs(1)
    r_cols = mu_ref.shape[1]

    @pl.when(ti == 0)
    def _():
        carry_s[...] = jnp.zeros_like(carry_s)
        st_s[...] = jnp.zeros_like(st_s)

    mask = hmask_ref[...]
    mask_bf = hmaskb_ref[...]
    strict, incl, diag = tmask_ref[0], tmask_ref[1], tmask_ref[2]
    tri_bf = segm_ref[...]

    def segsum(x):
        return jnp.concatenate([_dot(x[:, g * gw:(g + 1) * gw], mask_bf) for g in range(n_groups)], axis=1)

    p = p_ref[0] if nb == 1 else jnp.concatenate([p_ref[s] for s in range(nb)], axis=0)
    trow = _mod_pow2(lax.broadcasted_iota(jnp.int32, (rows, 1), 0), tr)
    if nb == 1:
        first = carry_s[0]
    else:
        first = jnp.concatenate([jnp.broadcast_to(carry_s[s], (tr, r_cols)) for s in range(nb)], axis=0)
    for s in range(nb):
        carry_s[s] = p_ref[s, tr - 1:tr, :r_cols]
    f = _token_features(p, first, trow, (mu_ref, w0_ref, a0_ref, wwa_ref, wg_ref, kk_ref, ka_ref), segsum, rw)
    sr = tri_bf.shape[0]
    cum = jnp.concatenate([_dot_sel_lhs(tri_bf, f["lw"][i * sr:(i + 1) * sr]) for i in range(rows // sr)],
                          axis=0)
    tot = jnp.concatenate([jnp.broadcast_to(cum[j * c + c - 1:j * c + c], (c, rw)) for j in range(rows // c)],
                          axis=0)
    d = _decay_factors(f, cum, tot)

    n_chunks = rows // c
    chunks_per_seq = tr // c

    def blk(x, j, g):
        return x[j * c:(j + 1) * c, g * gw:(g + 1) * gw]

    probs = [(j, g) for j in range(n_chunks) for g in range(n_groups)]
    cut = lambda x, ps: [blk(x, j, g) for j, g in ps]
    local = dict(zip(probs, _wkv_local(cut(d["rt"], probs), cut(d["at"], probs), cut(d["bt"], probs),
                                       cut(d["kt"], probs), cut(d["v"], probs), mask_bf, strict, incl, diag)))
    y_c, _ = _gmlp_branch(p, r_cols, lng_ref, lnb_ref, ws2_ref, tril2_ref[...], bsf_ref)
    y_blk = {}
    for jj in range(chunks_per_seq):
        ps = [(s * chunks_per_seq + jj, g) for s in range(nb) for g in range(n_groups)]
        loc = [local[pr_] for pr_ in ps]
        sts = [st_s[j // chunks_per_seq, g] for j, g in ps]
        pcs = [blk(d["pinc"], j, g)[c - 1:c] for j, g in ps]
        u, ys, st_new = _wkv_state_step([l[0] for l in loc], cut(d["rt"], ps), [l[1] for l in loc],
                                        cut(d["v"], ps), cut(d["bdec"], ps), cut(d["kdec"], ps), pcs, sts, mask)
        for i, (j, g) in enumerate(ps):
            st_s[j // chunks_per_seq, g] = st_new[i]
            y_blk[j, g] = ys[i] + _dot(loc[i][2], _block_diag(u[i], mask_bf)) + loc[i][3]
    y_rows = [jnp.concatenate([y_blk[j, g] for g in range(n_groups)], axis=1) for j in range(n_chunks)]
    yw = y_rows[0] if n_chunks == 1 else jnp.concatenate(y_rows, axis=0)

    y_r = _rwkv_output(yw, f, rk_ref, gnw_ref, gnb_ref, segsum, head_dim)
    y_all = jnp.concatenate([y_r, y_c], axis=1)
    for s in range(nb):
        y_ref[s] = y_all[s * tr:(s + 1) * tr]

    @pl.when(ti == n_t - 1)
    def _():
        shift_ref[...] = carry_s[...]
        for s in range(nb):
            for g in range(n_groups):
                _state_to_heads(wkv_ref, s, g, st_s[s, g], head_dim)


def _mixer_short_kernel(p_ref, shift0_ref, wkv0_ref, mu_ref, w0_ref, a0_ref, wwa_ref, wg_ref, kk_ref, ka_ref,
                        rk_ref, gnw_ref, gnb_ref, lng_ref, lnb_ref, ws2_ref, bsf_ref, hmask_ref, hmaskb_ref,
                        tmask_ref, segm_ref, tril2_ref, y_ref, shift_ref, wkv_ref, vn_ref, *, n_heads, head_dim):
    nb, ts, _ = p_ref.shape
    c = WKV_CHUNK
    assert nb * ts == c
    gw = HEAD_GROUP * head_dim
    n_groups = n_heads // HEAD_GROUP
    rw = n_heads * head_dim
    r_cols = mu_ref.shape[1]

    mask = hmask_ref[...]
    mask_bf = hmaskb_ref[...]
    strict, incl = tmask_ref[0], tmask_ref[1]

    def segsum(x):
        return jnp.concatenate([_dot(x[:, g * gw:(g + 1) * gw], mask_bf) for g in range(n_groups)], axis=1)

    p = p_ref[...].reshape(c, p_ref.shape[2])
    trow = _mod_pow2(lax.broadcasted_iota(jnp.int32, (c, 1), 0), ts)
    first = jnp.concatenate([jnp.broadcast_to(shift0_ref[s], (ts, r_cols)) for s in range(nb)], axis=0)
    for s in range(nb):
        shift_ref[s] = p_ref[s, ts - 1:ts, :r_cols]
    f = _token_features(p, first, trow, (mu_ref, w0_ref, a0_ref, wwa_ref, wg_ref, kk_ref, ka_ref), segsum, rw)
    cum = _dot_sel_lhs(segm_ref[...], f["lw"])
    tot = jnp.concatenate([jnp.broadcast_to(cum[s * ts + ts - 1:(s + 1) * ts], (ts, rw)) for s in range(nb)],
                          axis=0)
    d = _decay_factors(f, cum, tot)

    groups = range(n_groups)
    gcut = lambda x: [x[:, g * gw:(g + 1) * gw] for g in groups]
    local = _wkv_local(gcut(d["rt"]), gcut(d["at"]), gcut(d["bt"]), gcut(d["kt"]), gcut(d["v"]), mask_bf,
                       strict, incl, ts)

    ps = [(s, g) for s in range(nb) for g in groups]
    scut = lambda x: [x[s * ts:(s + 1) * ts, g * gw:(g + 1) * gw] for s, g in ps]
    lcut = lambda idx: [local[g][idx][s * ts:(s + 1) * ts] for s, g in ps]
    sts = [_state_from_heads(wkv0_ref, s, g, head_dim) for s, g in ps]
    pcs = [d["pinc"][s * ts + ts - 1:(s + 1) * ts, g * gw:(g + 1) * gw] for s, g in ps]
    u, ys, st_new = _wkv_state_step(lcut(0), scut(d["rt"]), lcut(1), scut(d["v"]), scut(d["bdec"]),
                                    scut(d["kdec"]), pcs, sts, mask)
    for i, (s, g) in enumerate(ps):
        _state_to_heads(wkv_ref, s, g, st_new[i], head_dim)
    rows_of = lambda xs, g: jnp.concatenate([xs[i] for i, (s, gg) in enumerate(ps) if gg == g], axis=0)
    yw = jnp.concatenate([rows_of(ys, g) + _dot(local[g][2], _block_diag(rows_of(u, g), mask_bf)) + local[g][3]
                          for g in groups], axis=1)

    y_r = _rwkv_output(yw, f, rk_ref, gnw_ref, gnb_ref, segsum, head_dim)
    y_c, vn = _gmlp_branch(p, r_cols, lng_ref, lnb_ref, ws2_ref, tril2_ref[...], bsf_ref)
    y_ref[...] = jnp.concatenate([y_r, y_c], axis=1).reshape(nb, ts, -1)
    vn_ref[...] = vn.reshape(nb, ts, -1)


def _mixer_masks(rows, seq, ct, head_dim):
    c, gw = WKV_CHUNK, HEAD_GROUP * head_dim
    blk = np.arange(gw) // head_dim
    hmask = (blk[:, None] == blk[None, :]).astype(np.float32)
    t = np.arange(c)[:, None]
    s = np.arange(HEAD_GROUP * c)[None, :] % c
    strict = (s < t) & (s // seq == t // seq)
    incl = (s <= t) & (s // seq == t // seq)
    tmask = np.stack([strict, incl, strict & (s // INV_BLOCK == t // INV_BLOCK)]).astype(np.float32)
    r = np.arange(rows)
    seg = min(c, seq)
    same = (r[:, None] // seg) == (r[None, :] // seg)
    segm = (same & (r[None, :] <= r[:, None])).astype(np.float32)
    ctm = np.arange(ct)
    tril1 = ((ctm[None, :] <= ctm[:, None]) & (ctm[None, :] // seq == ctm[:, None] // seq))
    tril2 = np.concatenate([tril1, tril1], axis=1).astype(np.float32)
    return [jnp.asarray(hmask), jnp.asarray(hmask, _BF16), jnp.asarray(tmask), jnp.asarray(segm, _BF16),
            jnp.asarray(tril2)]


def _gmlp_weights(w_s, b_s, ct, seq, gwid):
    hc = w_s.shape[0]
    n_seq = ct // min(seq, ct)
    ws = w_s[:, :ct // n_seq, :ct // n_seq]
    bs = b_s[:, :ct // n_seq]
    if n_seq > 1:
        ws = jnp.einsum("st,hij->hsitj", jnp.eye(n_seq, dtype=ws.dtype), ws).reshape(hc, ct, ct)
        bs = jnp.tile(bs, (1, n_seq))
    ws2 = ws.reshape(hc // 2, 2, ct, ct).transpose(0, 2, 1, 3).reshape(hc // 2, ct, 2 * ct)
    return ws2, jnp.repeat(bs.T, gwid // hc, axis=1)


def _mixer_params(prm, rows, seq, ct, head_dim):
    gwid = prm["lng"].shape[1]
    ws2, bsf = _gmlp_weights(prm["w_s"], prm["b_s"], ct, seq, gwid)
    params = [prm["mu"], prm["w0"], prm["a0"], prm["wwa"], prm["wg"], prm["k_k"], prm["k_a"], prm["r_k"],
              prm["gn_w"], prm["gn_b"], prm["lng"], prm["lnb"], ws2, bsf]
    return params + _mixer_masks(rows, seq, ct, head_dim)


def _full_spec(a):
    nd = a.ndim
    return pl.BlockSpec(a.shape, lambda b, i: (0,) * nd)


def _mixer_call(p, prm, *, nb, tr, n_heads, head_dim):
    nseq, t, pcols = p.shape
    rw = n_heads * head_dim
    r_cols = prm["mu"].shape[1]
    gwid = prm["lng"].shape[1]
    gw = HEAD_GROUP * head_dim
    n_groups = n_heads // HEAD_GROUP
    rows = nb * tr
    assert nseq % nb == 0 and t % tr == 0 and tr % GMLP_CHUNK == 0 and rows % SEG_ROWS == 0
    params = _mixer_params(prm, SEG_ROWS, t, GMLP_CHUNK, head_dim)
    return pl.pallas_call(
        functools.partial(_mixer_kernel, n_heads=n_heads, head_dim=head_dim),
        out_shape=(jax.ShapeDtypeStruct((nseq, t, rw + gwid), _F32),
                   jax.ShapeDtypeStruct((nseq, 1, r_cols), _F32),
                   jax.ShapeDtypeStruct((nseq, n_heads, head_dim, head_dim), _F32)),
        grid=(nseq // nb, t // tr),
        in_specs=[pl.BlockSpec((nb, tr, pcols), lambda b, i: (b, i, 0))] + [_full_spec(a) for a in params],
        out_specs=(pl.BlockSpec((nb, tr, rw + gwid), lambda b, i: (b, i, 0)),
                   pl.BlockSpec((nb, 1, r_cols), lambda b, i: (b, 0, 0)),
                   pl.BlockSpec((nb, n_heads, head_dim, head_dim), lambda b, i: (b, 0, 0, 0))),
        scratch_shapes=[pltpu.VMEM((nb, n_groups, gw, gw), _F32), pltpu.VMEM((nb, 1, r_cols), _F32)],
        compiler_params=pltpu.CompilerParams(dimension_semantics=("arbitrary", "arbitrary"),
                                             vmem_limit_bytes=_VMEM_LIMIT),
        name="mixers_fresh",
    )(p, *params)


def _mixer_short_call(p, shift0, wkv0, prm, *, n_heads, head_dim):
    nseq, ts, pcols = p.shape
    nb = WKV_CHUNK // ts
    assert nb * ts == WKV_CHUNK and ts % SUBLANES == 0 and nseq % nb == 0
    rw = n_heads * head_dim
    r_cols = prm["mu"].shape[1]
    gwid = prm["lng"].shape[1]
    params = _mixer_params(prm, WKV_CHUNK, ts, WKV_CHUNK, head_dim)
    seqs = lambda *tail: pl.BlockSpec((nb,) + tail, lambda b, i: (b,) + (0,) * len(tail))
    return pl.pallas_call(
        functools.partial(_mixer_short_kernel, n_heads=n_heads, head_dim=head_dim),
        out_shape=(jax.ShapeDtypeStruct((nseq, ts, rw + gwid), _F32),
                   jax.ShapeDtypeStruct((nseq, 1, r_cols), _F32),
                   jax.ShapeDtypeStruct((nseq, n_heads, head_dim, head_dim), _F32),
                   jax.ShapeDtypeStruct((nseq, ts, gwid), _F32)),
        grid=(nseq // nb, 1),
        in_specs=[seqs(ts, pcols), seqs(1, r_cols), seqs(n_heads, head_dim, head_dim)]
                 + [_full_spec(a) for a in params],
        out_specs=(seqs(ts, rw + gwid), seqs(1, r_cols), seqs(n_heads, head_dim, head_dim), seqs(ts, gwid)),
        compiler_params=pltpu.CompilerParams(dimension_semantics=("arbitrary", "arbitrary"),
                                             vmem_limit_bytes=_VMEM_LIMIT),
        name="mixers_state",
    )(p, shift0, wkv0, *params)


def _pick_tile(n, pref):
    t = min(n, pref)
    while n % t:
        t //= 2
    return t


def kernel(x_prompt, x_sample, state_shift, state_wkv, c_prompt, c_sample, w_ada, b_ada, ffn1_gu, ffn1_dn, w_in, mu_shift, w0, w_lora_up, a0, a_lora_up, g_lora_up, k_k, k_a, r_k, gn_w, gn_b, ln_v_g, ln_v_b, w_s, b_s, w_out, ffn2_gu, ffn2_dn, final_g):
    depth = w_ada.shape[0]
    bp, tp, d = x_prompt.shape
    bs, ts, _ = x_sample.shape
    n_heads, head_dim = r_k.shape[1], r_k.shape[2]
    rw = n_heads * head_dim
    w_rank, a_rank = w_lora_up.shape[1], a_lora_up.shape[1]
    assert w_rank == a_rank and n_heads % HEAD_GROUP == 0 and head_dim == WKV_CHUNK
    d_ff = ffn1_dn.shape[1]
    tf = 256 if d_ff % 256 == 0 else 128

    xp = x_prompt.reshape(bp * tp, d)
    xs = x_sample.reshape(bs * ts, d)
    assert bs % SUBLANES == 0
    n_c = bs + bp
    n_c_pad = -(-n_c // 16) * 16
    c_all = jnp.concatenate([c_sample, c_prompt, jnp.zeros((n_c_pad - n_c, d), _F32)], axis=0)

    tm_p = _pick_tile(tp, 512)
    tr_p = _pick_tile(tp, 128)
    tm_s = ts * _pick_tile(bs, 64)
    nb_p = _pick_tile(bp, 4)

    shp, wkp, shs, wks, cvs = [], [], [], [], []
    for l in range(depth):
        mod = _mod_call(c_all, w_ada[l], b_ada[l])
        grp_p = dict(tm=tm_p, seq0=bs, seqs_per_tile=1, tiles_per_seq=tp // tm_p)
        grp_s = dict(tm=tm_s, seq0=0, seqs_per_tile=tm_s // ts, tiles_per_seq=1)

        zw = jnp.zeros((w_rank, rw), _F32)
        wwa = jnp.concatenate([jnp.concatenate([w_lora_up[l], zw], axis=1),
                               jnp.concatenate([zw, a_lora_up[l]], axis=1)], axis=0).astype(_BF16)
        row = lambda a: a.reshape(1, -1)
        prm = dict(mu=row(mu_shift[l]), w0=row(w0[l]), a0=row(a0[l]), wwa=wwa,
                   wg=g_lora_up[l].astype(_BF16), k_k=row(k_k[l]), k_a=row(k_a[l]), r_k=row(r_k[l]),
                   gn_w=row(gn_w[l]), gn_b=row(gn_b[l]), lng=row(ln_v_g[l]), lnb=row(ln_v_b[l]),
                   w_s=w_s[l], b_s=b_s[l])
        fg = final_g.reshape(1, d)
        final = l == depth - 1

        quarter_s = dict(grp_s, tm=tm_s // 4, seqs_per_tile=tm_s // ts // 4)
        (hp, pp), (hs, ps) = _ffn_in_call([dict(grp_p, acts=[xp]), dict(quarter_s, acts=[xs])], mod, ffn1_gu[l],
                                          ffn1_dn[l], w_in[l], tf=tf, stage=WEIGHT_STAGE_STEPS, export=False)[0]
        ymp, sh_p, wk_p = _mixer_call(pp.reshape(bp, tp, -1), prm, nb=nb_p, tr=tr_p,
                                      n_heads=n_heads, head_dim=head_dim)
        yms, sh_s, wk_s, v_s = _mixer_short_call(ps.reshape(bs, ts, -1), state_shift[l][:, None, :], state_wkv[l], prm,
                                                 n_heads=n_heads, head_dim=head_dim)
        half_s = dict(grp_s, tm=tm_s // 2, seqs_per_tile=tm_s // ts // 2)
        (xp,), (xs,) = _out_ffn_call([dict(grp_p, acts=[ymp.reshape(bp * tp, -1), hp]),
                                      dict(half_s, acts=[yms.reshape(bs * ts, -1), hs])], mod,
                                     w_out[l], ffn2_gu[l], ffn2_dn[l], fg, tf=tf, final=final,
                                     stage=WEIGHT_STAGE_STEPS)

        shp.append(sh_p[:, 0]); wkp.append(wk_p); shs.append(sh_s[:, 0]); wks.append(wk_s); cvs.append(v_s)

    y_prompt = xp.reshape(bp, tp, d)
    y_sample = xs.reshape(bs, ts, d)
    return (y_prompt, y_sample, jnp.stack(shp), jnp.stack(wkp), jnp.stack(shs), jnp.stack(wks), jnp.stack(cvs))
```
